```python
import jax, jax.numpy as jnp
from jax import lax
import numpy as np

D_MODEL = 1024
BATCH = 8
SEQ = 2048
DEPTH = 4
DEC_BATCH = 128
DEC_SEQ = 8
PAST_LEN = 16384
PAGE_SIZE = 128

EPS = 1e-6
NEG_BIG = -1e30
D_PLE = 256
GROUP_WIDTH = D_MODEL // 4
A_HEADS = 4
A_DK = GROUP_WIDTH // A_HEADS
A_DV = GROUP_WIDTH // A_HEADS
A_WIDTH = A_HEADS * A_DV
A_CHUNK = 16
B_HEADS = 4
B_DK = GROUP_WIDTH // B_HEADS
B_DV = GROUP_WIDTH // B_HEADS
B_WIDTH = B_HEADS * B_DV
B_QKV = 2 * B_HEADS * B_DK + B_WIDTH
B_CONV = 4
B_CHUNK = 64
C_WIDTH = GROUP_WIDTH
C_CONV = 31
D_GROUPS = 4
D_WIDTH = GROUP_WIDTH
D_GC = D_WIDTH // D_GROUPS
POOL_WINDOWS = (2, 4, 8, 16)
POOL_BUF = 15
MIX_WIDTH = A_WIDTH + B_WIDTH + C_WIDTH + D_WIDTH
IN_SIZES = (A_HEADS * A_DK, A_HEADS * A_DK, A_WIDTH, A_WIDTH, B_QKV, B_WIDTH, B_HEADS, B_HEADS, 2 * C_WIDTH, D_WIDTH)
IN_COLS = 2 * A_HEADS * A_DK + 2 * A_WIDTH + B_QKV + B_WIDTH + 2 * B_HEADS + 2 * C_WIDTH + D_WIDTH
PEER_HEADS = 8
PEER_NKEYS = 128
PEER_N = PEER_NKEYS * PEER_NKEYS
PEER_DKEY = 128
PEER_HALF = PEER_DKEY // 2
PEER_TOPK = 16
PEER_BLOCK = 128

kernel_name = 'hymba_style_hgrn2_gdn_conformer_pool_peer_step'


def rmsnorm(x, g):
    xf = x.astype(jnp.float32)
    y = xf * lax.rsqrt(jnp.mean(xf * xf, axis=-1, keepdims=True) + EPS)
    return (y * g.astype(jnp.float32)).astype(x.dtype)


def layernorm(x, g, b):
    xf = x.astype(jnp.float32)
    mu = jnp.mean(xf, axis=-1, keepdims=True)
    xc = xf - mu
    var = jnp.mean(xc * xc, axis=-1, keepdims=True)
    return xc * lax.rsqrt(var + EPS) * g.astype(jnp.float32) + b.astype(jnp.float32)


def l2norm(x):
    return x * lax.rsqrt(jnp.sum(x * x, axis=-1, keepdims=True) + EPS)


def split_in(z):
    idx = np.cumsum(IN_SIZES)[:-1].tolist()
    return jnp.split(z, idx, axis=-1)


def pad_seq(t, mult):
    pad = (-t.shape[1]) % mult
    return jnp.pad(t, [(0, 0), (0, pad)] + [(0, 0)] * (t.ndim - 2))


def causal_dwconv(x, buf, w):
    xp = jnp.concatenate([buf.astype(x.dtype), x], axis=1)
    y = lax.conv_general_dilated(xp, w[:, None, :].astype(x.dtype), window_strides=(1,), padding='VALID',
                                 dimension_numbers=('NWC', 'WIO', 'NWC'), feature_group_count=x.shape[-1])
    return y, xp[:, xp.shape[1] - (w.shape[0] - 1):]


def hgrn2_chunked(q, logf, k, v, S0):
    Bn, L = q.shape[:2]
    q, logf, k, v = [pad_seq(t, A_CHUNK) for t in (q, logf, k, v)]
    N = q.shape[1] // A_CHUNK
    q, logf, k, v = [t.reshape(Bn, N, A_CHUNK, *t.shape[2:]) for t in (q, logf, k, v)]
    la = jnp.cumsum(logf, axis=2)
    causal = jnp.tril(jnp.ones((A_CHUNK, A_CHUNK), bool))
    diff = la[:, :, :, None] - la[:, :, None, :]
    dec = jnp.exp(jnp.where(causal[:, :, None, None], diff, NEG_BIG))
    scores = jnp.einsum('bnthk,bnshk,bntshk->bnhts', q, k, dec)
    o_intra = jnp.einsum('bnhts,bnshv->bnthv', scores, v)
    q_dec = q * jnp.exp(la)
    k_dec = k * jnp.exp(la[:, :, -1:] - la)
    total = jnp.exp(la[:, :, -1])

    def step(S, xs):
        qd, kd, vv, tot = xs
        o = jnp.einsum('bthk,bhkv->bthv', qd, S)
        S = tot[..., None] * S + jnp.einsum('bshk,bshv->bhkv', kd, vv)
        return S, o

    S, o_inter = lax.scan(step, S0, tuple(jnp.moveaxis(t, 1, 0) for t in (q_dec, k_dec, v, total)))
    o = o_intra + jnp.moveaxis(o_inter, 0, 1)
    o = o.reshape(Bn, N * A_CHUNK, *o.shape[3:])[:, :L]
    return o, S


def gdn_chunked(q, k, v, beta, loga, S0):
    Bn, L = q.shape[:2]
    q, k, v, beta, loga = [pad_seq(t, B_CHUNK) for t in (q, k, v, beta, loga)]
    N = q.shape[1] // B_CHUNK
    q, k, v, beta, loga = [jnp.swapaxes(t.reshape(Bn, N, B_CHUNK, *t.shape[2:]), 2, 3)
                           for t in (q, k, v, beta, loga)]
    g = jnp.cumsum(loga, axis=-1)
    ar = jnp.arange(B_CHUNK)
    causal = ar[:, None] >= ar[None, :]
    strict = ar[:, None] > ar[None, :]
    dec = jnp.exp(jnp.where(causal, g[..., :, None] - g[..., None, :], NEG_BIG))
    kk = jnp.einsum('bnhtk,bnhsk->bnhts', k, k)
    A = jnp.where(strict, beta[..., :, None] * kk * dec, 0.0)
    M = A + jnp.eye(B_CHUNK, dtype=A.dtype)
    rhs = jnp.concatenate([beta[..., None] * v, (beta * jnp.exp(g))[..., None] * k], axis=-1)
    W = lax.linalg.triangular_solve(M, rhs, left_side=True, lower=True, unit_diagonal=True)
    W1, W2 = W[..., :B_DV], W[..., B_DV:]
    qk = jnp.einsum('bnhtk,bnhsk->bnhts', q, k) * dec
    q_dec = q * jnp.exp(g)[..., None]
    k_dec = k * jnp.exp(g[..., -1:] - g)[..., None]
    total = jnp.exp(g[..., -1])

    def step(S, xs):
        w1, w2, qd, kd, qkm, tot = xs
        u = w1 - jnp.einsum('bhck,bhkv->bhcv', w2, S)
        o = jnp.einsum('bhck,bhkv->bhcv', qd, S) + jnp.einsum('bhts,bhsv->bhtv', qkm, u)
        S = tot[..., None, None] * S + jnp.einsum('bhsk,bhsv->bhkv', kd, u)
        return S, o

    S, o = lax.scan(step, S0, tuple(jnp.moveaxis(t, 1, 0) for t in (W1, W2, q_dec, k_dec, qk, total)))
    o = jnp.swapaxes(jnp.moveaxis(o, 0, 1), 2, 3)
    o = o.reshape(Bn, N * B_CHUNK, B_HEADS, B_DV)[:, :L]
    return o, S


def mixer_hgrn2(zq, zf, zi, zg, lb, norm_w, S0):
    Bn, L = zq.shape[:2]
    hd = lambda t, d: t.astype(jnp.float32).reshape(Bn, L, A_HEADS, d)
    q = jax.nn.silu(hd(zq, A_DK))
    lbh = lb.reshape(A_HEADS, A_DK)
    zff = hd(zf, A_DK)
    f = lbh + (1.0 - lbh) * jax.nn.sigmoid(zff)
    logf = jnp.log(f)
    k = (1.0 - lbh) * jax.nn.sigmoid(-zff)
    v = hd(zi, A_DV)
    o, S = hgrn2_chunked(q, logf, k, v, S0.astype(jnp.float32))
    o = rmsnorm(o, norm_w.reshape(A_HEADS, A_DV)) * jax.nn.sigmoid(hd(zg, A_DV))
    return o.reshape(Bn, L, A_WIDTH), S


def mixer_gdn(zqkv, zg, zb, za, conv_w, a_log, dt_bias, norm_w, S0, buf):
    Bn, L = zqkv.shape[:2]
    qkv, new_buf = causal_dwconv(zqkv, buf, conv_w)
    qkv = jax.nn.silu(qkv.astype(jnp.float32))
    q, k, v = jnp.split(qkv, [B_HEADS * B_DK, 2 * B_HEADS * B_DK], axis=-1)
    q = l2norm(q.reshape(Bn, L, B_HEADS, B_DK)) * (B_DK ** -0.5)
    k = l2norm(k.reshape(Bn, L, B_HEADS, B_DK))
    v = v.reshape(Bn, L, B_HEADS, B_DV)
    beta = jax.nn.sigmoid(zb.astype(jnp.float32))
    loga = -jnp.exp(a_log.astype(jnp.float32)) * jax.nn.softplus(za.astype(jnp.float32) + dt_bias.astype(jnp.float32))
    o, S = gdn_chunked(q, k, v, beta, loga, S0.astype(jnp.float32))
    o = rmsnorm(o, norm_w) * jax.nn.silu(zg.astype(jnp.float32).reshape(Bn, L, B_HEADS, B_DV))
    return o.reshape(Bn, L, B_WIDTH), S, new_buf


def mixer_conformer(zc, dw_w, dw_b, ln_g, ln_b, buf):
    a, b = jnp.split(zc, 2, axis=-1)
    u = a * jax.nn.sigmoid(b)
    y, new_buf = causal_dwconv(u, buf, dw_w)
    y = jax.nn.silu(layernorm(y + dw_b.astype(y.dtype), ln_g, ln_b))
    return y, new_buf


def mixer_pool(zd, w_pool, scale, buf, pos0):
    Bn, L, C = zd.shape
    xf = zd.astype(jnp.float32)
    xp = jnp.concatenate([buf.astype(jnp.float32), xf], axis=1)
    cs = jnp.concatenate([jnp.zeros((Bn, 1, C), jnp.float32), jnp.cumsum(xp, axis=1)], axis=1)
    end = cs[:, POOL_BUF + 1:]
    pos = pos0 + jnp.arange(L)
    means = []
    for gi, w in enumerate(POOL_WINDOWS):
        sl = slice(gi * D_GC, (gi + 1) * D_GC)
        s = end[..., sl] - cs[:, POOL_BUF + 1 - w:POOL_BUF + 1 - w + L, sl]
        cnt = jnp.minimum(w, pos + 1).astype(jnp.float32)
        means.append(s / cnt[None, :, None])
    diff = (jnp.concatenate(means, axis=-1) - xf).reshape(Bn, L, D_GROUPS, D_GC)
    y = jnp.einsum('blgc,gcd->blgd', diff, w_pool.astype(jnp.float32)).reshape(Bn, L, D_WIDTH)
    y = y * scale.astype(jnp.float32)
    return y, xp[:, xp.shape[1] - POOL_BUF:]


def peer(x, wq, keys, u_tab, v_tab):
    Bn, L, D = x.shape
    T = Bn * L
    xt = jnp.pad(x.reshape(T, D), ((0, (-T) % PEER_BLOCK), (0, 0)))
    blocks = xt.reshape(-1, PEER_BLOCK, D)

    def one(xb):
        q = (xb @ wq).astype(jnp.float32).reshape(PEER_BLOCK, PEER_HEADS, 2, PEER_HALF)
        s = jnp.einsum('thpd,hpnd->thpn', q, keys.astype(jnp.float32))
        sv, si = lax.top_k(s, PEER_TOPK)
        cand = sv[:, :, 0, :, None] + sv[:, :, 1, None, :]
        cv, ci = lax.top_k(cand.reshape(PEER_BLOCK, PEER_HEADS, PEER_TOPK * PEER_TOPK), PEER_TOPK)
        i1 = jnp.take_along_axis(si[:, :, 0], ci // PEER_TOPK, axis=-1)
        i2 = jnp.take_along_axis(si[:, :, 1], ci % PEER_TOPK, axis=-1)
        e = i1 * PEER_NKEYS + i2
        gate = jax.nn.softmax(cv, axis=-1)
        act = jax.nn.gelu(jnp.einsum('td,thkd->thk', xb, u_tab[e]).astype(jnp.float32), approximate=False)
        return jnp.einsum('thk,thkd->td', (gate * act).astype(xb.dtype), v_tab[e])

    y = lax.map(one, blocks).reshape(-1, D)[:T]
    return y.reshape(Bn, L, D).astype(x.dtype)


def run_trunk(x, p, pos0, sA, sB, cB, cC, cD, w, lb):
    h = x
    nA, nB, nbB, nbC, nbD = [], [], [], [], []
    for i in range(DEPTH):
        a = rmsnorm(h, w['norm_mix'][i])
        zaq, zaf, zai, zag, zbqkv, zbg, zbb, zba, zc, zd = split_in(a @ w['w_in'][i])
        oA, sa = mixer_hgrn2(zaq, zaf, zai, zag, lb[i], w['hgrn_norm'][i], sA[i])
        oB, sb, bb = mixer_gdn(zbqkv, zbg, zbb, zba, w['gdn_conv_w'][i], w['gdn_a_log'][i], w['gdn_dt_bias'][i],
                               w['gdn_norm'][i], sB[i], cB[i])
        oC, bc = mixer_conformer(zc, w['conf_dw_w'][i], w['conf_dw_b'][i], w['conf_ln_g'][i], w['conf_ln_b'][i], cC[i])
        oD, bd = mixer_pool(zd, w['pool_w'][i], w['pool_scale'][i], cD[i], pos0)
        mix = jnp.concatenate([oA, oB, oC, oD], axis=-1).astype(h.dtype)
        h = h + mix @ w['w_out'][i]
        h = h + peer(rmsnorm(h, w['norm_ffn'][i]), w['peer_wq'][i], w['peer_keys'][i], w['peer_u'][i], w['peer_v'][i])
        gate = jax.nn.sigmoid((rmsnorm(h, w['norm_ple'][i]) @ w['ple_gate_w'][i]).astype(jnp.float32))
        h = h + (gate * (p[i] @ w['ple_proj_w'][i]).astype(jnp.float32)).astype(h.dtype)
        nA.append(sa); nB.append(sb); nbB.append(bb); nbC.append(bc); nbD.append(bd)
    y = rmsnorm(h, w['norm_final'])
    st = lambda lst: jnp.stack(lst, axis=0).astype(x.dtype)
    return y, st(nA), st(nB), st(nbB), st(nbC), st(nbD)


def setup_inputs(seed: int = 0) -> dict:
    key = jax.random.key(seed)
    ks = jax.random.split(key, 33)
    f32 = jnp.float32
    nrm = lambda k, shape, sc: jax.random.normal(k, shape, f32) * sc
    gain = lambda k, shape: 1.0 + 0.02 * jax.random.normal(k, shape, f32)
    dt = jnp.exp(jax.random.uniform(ks[15], (DEPTH, B_HEADS), f32, float(np.log(1e-3)), float(np.log(1e-1))))
    return {
        'x_prompt': nrm(ks[0], (BATCH, SEQ, D_MODEL), 1.0),
        'x_sample': nrm(ks[1], (DEC_BATCH, DEC_SEQ, D_MODEL), 1.0),
        'state_hgrn': nrm(ks[2], (DEPTH, DEC_BATCH, A_HEADS, A_DK, A_DV), 0.5),
        'state_gdn': nrm(ks[3], (DEPTH, DEC_BATCH, B_HEADS, B_DK, B_DV), 0.3),
        'state_gdn_conv': nrm(ks[4], (DEPTH, DEC_BATCH, B_CONV - 1, B_QKV), 1.0),
        'state_conf_conv': nrm(ks[5], (DEPTH, DEC_BATCH, C_CONV - 1, C_WIDTH), 1.0),
        'state_pool': nrm(ks[6], (DEPTH, DEC_BATCH, POOL_BUF, D_WIDTH), 1.0),
        'p_prompt': nrm(ks[7], (DEPTH, BATCH, SEQ, D_PLE), 1.0),
        'p_sample': nrm(ks[8], (DEPTH, DEC_BATCH, DEC_SEQ, D_PLE), 1.0),
        'norm_mix': gain(ks[9], (DEPTH, D_MODEL)),
        'w_in': nrm(ks[10], (DEPTH, D_MODEL, IN_COLS), D_MODEL ** -0.5),
        'hgrn_lb_logits': nrm(ks[11], (DEPTH, A_HEADS * A_DK), 1.0),
        'hgrn_norm': gain(ks[12], (DEPTH, A_WIDTH)),
        'gdn_conv_w': nrm(ks[13], (DEPTH, B_CONV, B_QKV), B_CONV ** -0.5),
        'gdn_a_log': jnp.log(jax.random.uniform(ks[14], (DEPTH, B_HEADS), f32, 1.0, 16.0)),
        'gdn_dt_bias': dt + jnp.log(-jnp.expm1(-dt)),
        'gdn_norm': gain(ks[16], (DEPTH, B_DV)),
        'conf_dw_w': nrm(ks[17], (DEPTH, C_CONV, C_WIDTH), C_CONV ** -0.5),
        'conf_dw_b': nrm(ks[18], (DEPTH, C_WIDTH), 0.02),
        'conf_ln_g': gain(ks[19], (DEPTH, C_WIDTH)),
        'conf_ln_b': nrm(ks[20], (DEPTH, C_WIDTH), 0.02),
        'pool_w': nrm(ks[21], (DEPTH, D_GROUPS, D_GC, D_GC), D_GC ** -0.5),
        'pool_scale': gain(ks[22], (DEPTH, D_WIDTH)),
        'w_out': nrm(ks[23], (DEPTH, MIX_WIDTH, D_MODEL), MIX_WIDTH ** -0.5),
        'norm_ffn': gain(ks[24], (DEPTH, D_MODEL)),
        'peer_wq': nrm(ks[25], (DEPTH, D_MODEL, PEER_HEADS * PEER_DKEY), D_MODEL ** -0.5),
        'peer_keys': nrm(ks[26], (DEPTH, PEER_HEADS, 2, PEER_NKEYS, PEER_HALF), PEER_HALF ** -0.5),
        'peer_u': nrm(ks[27], (DEPTH, PEER_N, D_MODEL), D_MODEL ** -0.5),
        'peer_v': nrm(ks[28], (DEPTH, PEER_N, D_MODEL), 0.3),
        'norm_ple': gain(ks[29], (DEPTH, D_MODEL)),
        'ple_gate_w': nrm(ks[30], (DEPTH, D_MODEL, D_MODEL), D_MODEL ** -0.5),
        'ple_proj_w': nrm(ks[31], (DEPTH, D_PLE, D_MODEL), D_PLE ** -0.5),
        'norm_final': gain(ks[32], (D_MODEL,)),
    }


def reference(x_prompt, x_sample, state_hgrn, state_gdn, state_gdn_conv, state_conf_conv, state_pool,
              p_prompt, p_sample, norm_mix, w_in, hgrn_lb_logits, hgrn_norm, gdn_conv_w, gdn_a_log, gdn_dt_bias,
              gdn_norm, conf_dw_w, conf_dw_b, conf_ln_g, conf_ln_b, pool_w, pool_scale, w_out, norm_ffn,
              peer_wq, peer_keys, peer_u, peer_v, norm_ple, ple_gate_w, ple_proj_w, norm_final):
    w = dict(norm_mix=norm_mix, w_in=w_in, hgrn_norm=hgrn_norm, gdn_conv_w=gdn_conv_w, gdn_a_log=gdn_a_log,
             gdn_dt_bias=gdn_dt_bias, gdn_norm=gdn_norm, conf_dw_w=conf_dw_w, conf_dw_b=conf_dw_b,
             conf_ln_g=conf_ln_g, conf_ln_b=conf_ln_b, pool_w=pool_w, pool_scale=pool_scale, w_out=w_out,
             norm_ffn=norm_ffn, peer_wq=peer_wq, peer_keys=peer_keys, peer_u=peer_u, peer_v=peer_v,
             norm_ple=norm_ple, ple_gate_w=ple_gate_w, ple_proj_w=ple_proj_w, norm_final=norm_final)
    sm = jax.nn.softmax(hgrn_lb_logits.astype(jnp.float32), axis=0)
    lb = jnp.cumsum(sm, axis=0) - sm[0:1]
    dt = x_prompt.dtype
    zA = jnp.zeros((DEPTH, BATCH, A_HEADS, A_DK, A_DV), dt)
    zB = jnp.zeros((DEPTH, BATCH, B_HEADS, B_DK, B_DV), dt)
    zcB = jnp.zeros((DEPTH, BATCH, B_CONV - 1, B_QKV), dt)
    zcC = jnp.zeros((DEPTH, BATCH, C_CONV - 1, C_WIDTH), dt)
    zcD = jnp.zeros((DEPTH, BATCH, POOL_BUF, D_WIDTH), dt)
    y_prompt, hp, gp, gcp, ccp, pp = run_trunk(x_prompt, p_prompt, 0, zA, zB, zcB, zcC, zcD, w, lb)
    y_sample, hs, gs, gcs, ccs, ps = run_trunk(x_sample, p_sample, PAST_LEN, state_hgrn, state_gdn, state_gdn_conv,
                                               state_conf_conv, state_pool, w, lb)
    return (y_prompt, y_sample, hp, gp, gcp, ccp, pp, hs, gs, gcs, ccs, ps)
```

```python
import functools

import numpy as np
import jax
import jax.numpy as jnp
from jax import lax
from jax.experimental import pallas as pl
from jax.experimental.pallas import tpu as pltpu

F32 = jnp.float32
BF16 = jnp.bfloat16

D_MODEL = 1024
DEPTH = 4
PAST_LEN = 16384
EPS = 1e-6
NEG_BIG = -1e30
D_PLE = 256
HEADS = 4
DH = 64
GW = 256
B_CONV = 4
C_CONV = 31
POOL_WINDOWS = (2, 4, 8, 16)
POOL_BUF = 15
PEER_HEADS = 8
PEER_NKEYS = 128
PEER_HALF = 64
PEER_TOPK = 16

ZB_AQ, ZB_AF, ZB_AI, ZB_AG, ZB_BQ, ZB_BK, ZB_BV, ZB_BG, ZB_CA, ZB_CB, ZB_D, ZB_BB, ZB_BA = range(13)
ZW = 13 * GW

LANES = 128
VMEM_LIMIT = 56 * 1024 * 1024


def _cparams(sem):
    return pltpu.CompilerParams(dimension_semantics=sem, vmem_limit_bytes=VMEM_LIMIT)


def _bdot(a, b):
    return jnp.dot(a.astype(BF16), b.astype(BF16), preferred_element_type=F32)


def _bdot_nt(a, b):
    return lax.dot_general(a.astype(BF16), b.astype(BF16), (((1,), (1,)), ((), ())),
                           preferred_element_type=F32)


def _split3(a):
    a1 = a.astype(BF16)
    r1 = a - a1.astype(F32)
    a2 = r1.astype(BF16)
    a3 = (r1 - a2.astype(F32)).astype(BF16)
    return a1, a2, a3


def _xdot(a, m01):
    a1, a2, a3 = _split3(a)
    m = m01.astype(BF16)
    return (jnp.dot(a1, m, preferred_element_type=F32) + jnp.dot(a2, m, preferred_element_type=F32)
            + jnp.dot(a3, m, preferred_element_type=F32))


def _xdot_left(m01, a):
    a1, a2, a3 = _split3(a)
    m = m01.astype(BF16)
    return (jnp.dot(m, a1, preferred_element_type=F32) + jnp.dot(m, a2, preferred_element_type=F32)
            + jnp.dot(m, a3, preferred_element_type=F32))


def _xtranspose(a, eye):
    a1, a2, a3 = _split3(a)
    e = eye.astype(BF16)
    dn = (((1,), (1,)), ((), ()))
    return (lax.dot_general(e, a1, dn, preferred_element_type=F32)
            + lax.dot_general(e, a2, dn, preferred_element_type=F32)
            + lax.dot_general(e, a3, dn, preferred_element_type=F32))


def _iota(shape, axis):
    return lax.broadcasted_iota(jnp.int32, shape, axis)


def _tri_incl(c):
    return jnp.where(_iota((c, c), 1) <= _iota((c, c), 0), 1.0, 0.0).astype(F32)


def _eye(n):
    return jnp.where(_iota((n, n), 0) == _iota((n, n), 1), 1.0, 0.0).astype(F32)


def _block_ones(n, blk):
    return jnp.where(_iota((n, n), 0) // blk == _iota((n, n), 1) // blk, 1.0, 0.0).astype(F32)


def _rms(x, g):
    return x * lax.rsqrt(jnp.mean(x * x, axis=-1, keepdims=True) + EPS) * g


def _sigmoid(x):
    return jax.nn.sigmoid(x)


def _silu(x):
    return x * jax.nn.sigmoid(x)


def _inproj_kernel(x_ref, g_ref, w_ref, o_ref):
    xn = _rms(x_ref[...], g_ref[...])
    o_ref[...] = jnp.dot(xn.astype(BF16), w_ref[...], preferred_element_type=F32)


def _inproj(h, g, w_bf16, tm):
    t = h.shape[0]
    return pl.pallas_call(
        _inproj_kernel,
        out_shape=jax.ShapeDtypeStruct((t, ZW), F32),
        grid=(t // tm,),
        in_specs=[pl.BlockSpec((tm, D_MODEL), lambda i: (i, 0)),
                  pl.BlockSpec((1, D_MODEL), lambda i: (0, 0)),
                  pl.BlockSpec((D_MODEL, ZW), lambda i: (0, 0))],
        out_specs=pl.BlockSpec((tm, ZW), lambda i: (i, 0)),
        compiler_params=_cparams(("parallel",)),
        name="inproj",
    )(h, g, w_bf16)


def _outproj_kernel(h_ref, a_ref, b_ref, c_ref, d_ref, w_ref, o_ref):
    acc = h_ref[...]
    for i, r in enumerate((a_ref, b_ref, c_ref, d_ref)):
        acc = acc + jnp.dot(r[...].astype(BF16), w_ref[i * GW:(i + 1) * GW, :], preferred_element_type=F32)
    o_ref[...] = acc


def _outproj(h, oa, ob, oc, od, w_bf16, tm):
    t = h.shape[0]
    tok = lambda w: pl.BlockSpec((tm, w), lambda i: (i, 0))
    return pl.pallas_call(
        _outproj_kernel,
        out_shape=jax.ShapeDtypeStruct((t, D_MODEL), F32),
        grid=(t // tm,),
        in_specs=[tok(D_MODEL), tok(GW), tok(GW), tok(GW), tok(GW),
                  pl.BlockSpec((D_MODEL, D_MODEL), lambda i: (0, 0))],
        out_specs=tok(D_MODEL),
        compiler_params=_cparams(("parallel",)),
        name="outproj",
    )(h, oa, ob, oc, od, w_bf16)


def _ple_kernel(h_ref, p_ref, g_ref, wg_ref, wp_ref, gf_ref, o_ref, *, final):
    h = h_ref[...]
    xn = _rms(h, g_ref[...])
    gate = _sigmoid(jnp.dot(xn.astype(BF16), wg_ref[...], preferred_element_type=F32))
    proj = jnp.dot(p_ref[...].astype(BF16), wp_ref[...], preferred_element_type=F32)
    out = h + gate * proj
    if final:
        out = _rms(out, gf_ref[...])
    o_ref[...] = out


def _ple(h, p, g, wg_bf16, wp_bf16, gfinal, tm, final):
    t = h.shape[0]
    tok = lambda w: pl.BlockSpec((tm, w), lambda i: (i, 0))
    full = lambda r, c: pl.BlockSpec((r, c), lambda i: (0, 0))
    return pl.pallas_call(
        functools.partial(_ple_kernel, final=final),
        out_shape=jax.ShapeDtypeStruct((t, D_MODEL), F32),
        grid=(t // tm,),
        in_specs=[tok(D_MODEL), tok(D_PLE), full(1, D_MODEL), full(D_MODEL, D_MODEL), full(D_PLE, D_MODEL),
                  full(1, D_MODEL)],
        out_specs=tok(D_MODEL),
        compiler_params=_cparams(("parallel",)),
        name="ple_final" if final else "ple",
    )(h, p, g, wg_bf16, wp_bf16, gfinal)


def _hgrn_kernel(z_ref, lb_ref, nw_ref, s0_ref, o_ref, sf_ref, st_scr, la_scr, k_scr, v_scr, p_scr, *, nb, c):
    ci = pl.program_id(1)

    @pl.when(ci == 0)
    def _():
        st_scr[...] = s0_ref[...]

    lb = lb_ref[...]
    bones = _block_ones(GW, DH)
    tri = _tri_incl(c)
    rows = _iota((c, GW), 0)
    for b in range(nb):
        zq = z_ref[b, :, 0 * GW:1 * GW]
        zf = z_ref[b, :, 1 * GW:2 * GW]
        v = z_ref[b, :, 2 * GW:3 * GW]
        zg = z_ref[b, :, 3 * GW:4 * GW]
        q = _silu(zq)
        f = lb + (1.0 - lb) * _sigmoid(zf)
        k = (1.0 - lb) * _sigmoid(-zf)
        la = _xdot_left(tri, jnp.log(f))
        la_scr[...] = la
        k_scr[...] = k
        v_scr[...] = v

        def build(s, carry):
            la_s = la_scr[pl.ds(s, 1), :]
            k_s = k_scr[pl.ds(s, 1), :]
            dec = jnp.exp(jnp.minimum(la - la_s, 0.0))
            p_scr[pl.ds(pl.multiple_of(s * c, c), c), :] = jnp.where(rows >= s, q * k_s * dec, 0.0)
            return carry

        lax.fori_loop(0, c, build, 0)
        p_scr[...] = jnp.dot(p_scr[...].astype(BF16), bones.astype(BF16), preferred_element_type=F32)

        def consume(s, acc):
            return acc + p_scr[pl.ds(pl.multiple_of(s * c, c), c), :] * v_scr[pl.ds(s, 1), :]

        o = lax.fori_loop(0, c, consume, jnp.zeros((c, GW), F32))
        st = st_scr[b]
        o = o + _bdot_nt(q * jnp.exp(la), st)
        la_last = la[c - 1:c, :]
        k_dec = k * jnp.exp(la_last - la)
        upd = lax.dot_general(v.astype(BF16), k_dec.astype(BF16), (((0,), (0,)), ((), ())),
                              preferred_element_type=F32)
        st_scr[b] = st * jnp.exp(la_last) + upd * bones
        ms = _xdot(o * o, bones) * (1.0 / DH)
        o_ref[b] = o * lax.rsqrt(ms + EPS) * nw_ref[...] * _sigmoid(zg)

    @pl.when(ci == pl.num_programs(1) - 1)
    def _():
        sf_ref[...] = st_scr[...]


def _hgrn(z3, lb, nw, s0t, nb, c):
    bsz, l, _ = z3.shape
    return pl.pallas_call(
        functools.partial(_hgrn_kernel, nb=nb, c=c),
        out_shape=(jax.ShapeDtypeStruct((bsz, l, GW), F32), jax.ShapeDtypeStruct((bsz, GW, GW), F32)),
        grid=(bsz // nb, l // c),
        in_specs=[pl.BlockSpec((nb, c, 4 * GW), lambda b, i: (b, i, 0)),
                  pl.BlockSpec((1, GW), lambda b, i: (0, 0)),
                  pl.BlockSpec((1, GW), lambda b, i: (0, 0)),
                  pl.BlockSpec((nb, GW, GW), lambda b, i: (b, 0, 0))],
        out_specs=(pl.BlockSpec((nb, c, GW), lambda b, i: (b, i, 0)),
                   pl.BlockSpec((nb, GW, GW), lambda b, i: (b, 0, 0))),
        scratch_shapes=[pltpu.VMEM((nb, GW, GW), F32), pltpu.VMEM((c, GW), F32), pltpu.VMEM((c, GW), F32),
                        pltpu.VMEM((c, GW), F32), pltpu.VMEM((c * c, GW), F32)],
        compiler_params=_cparams(("parallel", "arbitrary")),
        name="hgrn",
    )(z3, lb, nw, s0t)


def _gdn_kernel(zq_ref, zk_ref, zv_ref, zg_ref, zb_ref, za_ref, cw_ref, alog_ref, dtb_ref, nw_ref, buf_ref, s0_ref,
                o_ref, sf_ref, s_scr, xp_scr, *, nb, c):
    ci = pl.program_id(1)

    @pl.when(ci == 0)
    def _():
        s_scr[...] = s0_ref[...]
        xp_scr[:, 0:8, :] = buf_ref[...]

    bones = _block_ones(GW, DH)
    tri = _tri_incl(c)
    eye_c = _eye(c)
    eye_h = _eye(DH)
    tt = _iota((c, c), 0)
    ss = _iota((c, c), 1)
    causal = tt >= ss
    strict = tt > ss
    for b in range(nb):
        for j, r in enumerate((zq_ref, zk_ref, zv_ref)):
            xp_scr[b, 8:8 + c, j * GW:(j + 1) * GW] = r[b]
        conv = cw_ref[3:4, :] * xp_scr[b, 8:8 + c, :]
        for j in range(B_CONV - 1):
            conv = conv + cw_ref[j:j + 1, :] * xp_scr[b, 5 + j:5 + j + c, :]
        xp_scr[b, 0:8, :] = xp_scr[b, c:c + 8, :]
        qkv = _silu(conv)
        q = qkv[:, 0:GW]
        k = qkv[:, GW:2 * GW]
        v = qkv[:, 2 * GW:3 * GW]
        q = q * lax.rsqrt(_xdot(q * q, bones) + EPS) * (DH ** -0.5)
        k = k * lax.rsqrt(_xdot(k * k, bones) + EPS)
        beta = _sigmoid(zb_ref[b])
        loga = -jnp.exp(alog_ref[...]) * jax.nn.softplus(za_ref[b] + dtb_ref[...])
        g = _xdot_left(tri, loga)
        zg = zg_ref[b]
        for h in range(HEADS):
            hs = slice(h * DH, (h + 1) * DH)
            qh, kh, vh, bh, gh = q[:, hs], k[:, hs], v[:, hs], beta[:, hs], g[:, hs]
            gcol = gh[:, 0:c]
            grow = _xtranspose(gcol, eye_c)
            dec = jnp.exp(jnp.where(causal, gcol - grow, NEG_BIG))
            a = jnp.where(strict, bh[:, 0:c] * _bdot_nt(kh, kh) * dec, 0.0)
            r = jnp.concatenate([bh * vh, bh * jnp.exp(gh) * kh], axis=1)
            for s in range(c - 1):
                r = r - a[:, s:s + 1] * r[s:s + 1, :]
            w1 = r[:, 0:DH]
            w2 = r[:, DH:2 * DH]
            sh = s_scr[b, h]
            u = w1 - _bdot(w2, sh)
            qkm = _bdot_nt(qh, kh) * dec
            o = _bdot(qh * jnp.exp(gh), sh) + _bdot(qkm, u)
            g_last = gh[c - 1:c, :]
            kd = kh * jnp.exp(g_last - gh)
            kd_t = lax.dot_general(eye_h.astype(BF16), kd.astype(BF16), (((1,), (1,)), ((), ())),
                                   preferred_element_type=F32)
            s_scr[b, h] = jnp.exp(g_last) * sh + _bdot(kd_t, u)
            o = o * lax.rsqrt(jnp.mean(o * o, axis=-1, keepdims=True) + EPS) * nw_ref[...] * _silu(zg[:, hs])
            o_ref[b, :, hs] = o

    @pl.when(ci == pl.num_programs(1) - 1)
    def _():
        sf_ref[...] = s_scr[...]


def _gdn(z3, cw, alog_x, dtb_x, nw, buf8, s0, nb, c):
    bsz, l, _ = z3.shape
    zblk = lambda j: pl.BlockSpec((nb, c, GW), lambda b, i, j=j: (b, i, j))
    full = lambda r, w: pl.BlockSpec((r, w), lambda b, i: (0, 0))
    return pl.pallas_call(
        functools.partial(_gdn_kernel, nb=nb, c=c),
        out_shape=(jax.ShapeDtypeStruct((bsz, l, GW), F32), jax.ShapeDtypeStruct((bsz, HEADS, DH, DH), F32)),
        grid=(bsz // nb, l // c),
        in_specs=[zblk(ZB_BQ), zblk(ZB_BK), zblk(ZB_BV), zblk(ZB_BG), zblk(ZB_BB), zblk(ZB_BA),
                  full(B_CONV, 3 * GW), full(1, GW), full(1, GW), full(1, DH),
                  pl.BlockSpec((nb, 8, 3 * GW), lambda b, i: (b, 0, 0)),
                  pl.BlockSpec((nb, HEADS, DH, DH), lambda b, i: (b, 0, 0, 0))],
        out_specs=(pl.BlockSpec((nb, c, GW), lambda b, i: (b, i, 0)),
                   pl.BlockSpec((nb, HEADS, DH, DH), lambda b, i: (b, 0, 0, 0))),
        scratch_shapes=[pltpu.VMEM((nb, HEADS, DH, DH), F32), pltpu.VMEM((nb, 8 + c, 3 * GW), F32)],
        compiler_params=_cparams(("parallel", "arbitrary")),
        name="gdn",
    )(z3, z3, z3, z3, z3, z3, cw, alog_x, dtb_x, nw, buf8, s0)


CPAD = 32
DPAD = 16


def _convpool_kernel(za_ref, zb_ref, zd_ref, dw_ref, db_ref, lg_ref, lb_ref, pw_ref, ps_ref, cbuf_ref, dbuf_ref,
                     oc_ref, u_ref, od_ref, xc_scr, xd_scr, *, nb, c, pos0):
    ci = pl.program_id(1)

    @pl.when(ci == 0)
    def _():
        xc_scr[:, 0:CPAD, :] = cbuf_ref[...]
        xd_scr[:, 0:DPAD, :] = dbuf_ref[...]

    lane = _iota((c, GW), 1)
    wl = jnp.where(lane < 64, 2.0, jnp.where(lane < 128, 4.0, jnp.where(lane < 192, 8.0, 16.0)))
    pos = (_iota((c, GW), 0) + (ci * c + pos0 + 1)).astype(F32)
    cnt = jnp.minimum(wl, pos)
    for b in range(nb):
        u = za_ref[b] * _sigmoid(zb_ref[b])
        u_ref[b] = u
        xc_scr[b, CPAD:CPAD + c, :] = u
        y = dw_ref[C_CONV - 1:C_CONV, :] * u
        for j in range(C_CONV - 1):
            y = y + dw_ref[j:j + 1, :] * xc_scr[b, 2 + j:2 + j + c, :]
        xc_scr[b, 0:CPAD, :] = xc_scr[b, c:c + CPAD, :]
        y = y + db_ref[...]
        mu = jnp.mean(y, axis=-1, keepdims=True)
        yc = y - mu
        var = jnp.mean(yc * yc, axis=-1, keepdims=True)
        oc_ref[b] = _silu(yc * lax.rsqrt(var + EPS) * lg_ref[...] + lb_ref[...])
        x = zd_ref[b]
        xd_scr[b, DPAD:DPAD + c, :] = x
        acc = x
        sums = {}
        for i in range(1, 16):
            acc = acc + xd_scr[b, DPAD - i:DPAD - i + c, :]
            if i + 1 in POOL_WINDOWS:
                sums[i + 1] = acc
        xd_scr[b, 0:DPAD, :] = xd_scr[b, c:c + DPAD, :]
        ssel = jnp.where(lane < 64, sums[2], jnp.where(lane < 128, sums[4], jnp.where(lane < 192, sums[8], sums[16])))
        diff = ssel / cnt - x
        od_ref[b] = _bdot(diff, pw_ref[...]) * ps_ref[...]


def _convpool(z3, dw, db, lg, lb, pwbd, ps, cbuf, dbuf, nb, c, pos0):
    bsz, l, _ = z3.shape
    zblk = lambda j: pl.BlockSpec((nb, c, GW), lambda b, i, j=j: (b, i, j))
    full = lambda r, w: pl.BlockSpec((r, w), lambda b, i: (0, 0))
    oblk = pl.BlockSpec((nb, c, GW), lambda b, i: (b, i, 0))
    osh = jax.ShapeDtypeStruct((bsz, l, GW), F32)
    return pl.pallas_call(
        functools.partial(_convpool_kernel, nb=nb, c=c, pos0=pos0),
        out_shape=(osh, osh, osh),
        grid=(bsz // nb, l // c),
        in_specs=[zblk(ZB_CA), zblk(ZB_CB), zblk(ZB_D), full(C_CONV, GW), full(1, GW), full(1, GW), full(1, GW),
                  full(GW, GW), full(1, GW),
                  pl.BlockSpec((nb, CPAD, GW), lambda b, i: (b, 0, 0)),
                  pl.BlockSpec((nb, DPAD, GW), lambda b, i: (b, 0, 0))],
        out_specs=(oblk, oblk, oblk),
        scratch_shapes=[pltpu.VMEM((nb, CPAD + c, GW), F32), pltpu.VMEM((nb, DPAD + c, GW), F32)],
        compiler_params=_cparams(("parallel", "arbitrary")),
        name="convpool",
    )(z3, z3, z3, dw, db, lg, lb, pwbd, ps, cbuf, dbuf)


NKH = 2 * PEER_HEADS * PEER_NKEYS
_CANDS = [(a, b) for a in range(PEER_TOPK) for b in range(PEER_TOPK) if (a + 1) * (b + 1) <= PEER_TOPK]


def _peer1_kernel(h_ref, g_ref, wq_ref, kbd_ref, xn_ref, e1_ref, c_ref, e2_ref, r2_ref, s_scr, sv_scr, d_scr, zi_scr,
                  *, nlt):
    xn = _rms(h_ref[...], g_ref[...]).astype(BF16)
    xn_ref[...] = xn
    q = jnp.dot(xn, wq_ref[...], preferred_element_type=F32).astype(BF16)
    for lt in range(nlt):
        s_scr[lt] = lax.dot_general(kbd_ref[...], q[lt * LANES:(lt + 1) * LANES, :], (((1,), (1,)), ((), ())),
                                    preferred_element_type=F32)

    def lane_tile(lt, carry):
        for h in range(PEER_HEADS):
            for p in range(2):
                s = s_scr[lt, (2 * h + p) * PEER_NKEYS:(2 * h + p + 1) * PEER_NKEYS, :]
                prev = jnp.full((1, LANES), jnp.inf, F32)
                svs = []
                for a in range(PEER_TOPK):
                    m = jnp.max(jnp.where(s < prev, s, NEG_BIG), axis=0, keepdims=True)
                    sv_scr[p, a, h:h + 1, :] = m
                    svs.append(m)
                    prev = m
                if p == 1:
                    rank = jnp.zeros(s.shape, F32)
                    for a in range(PEER_TOPK):
                        rank = rank + jnp.where(s < svs[a], 1.0, 0.0)
                    r2_ref[lt, h] = rank
                    e2_ref[lt, h] = jnp.exp(s - svs[0])
        sv1 = [sv_scr[0, a] for a in range(PEER_TOPK)]
        sv2 = [sv_scr[1, a] for a in range(PEER_TOPK)]
        vals = [sv1[a] + sv2[b] for a, b in _CANDS]
        n = len(_CANDS)
        before = [jnp.zeros((PEER_HEADS, LANES), F32) for _ in range(n)]
        for i in range(n):
            ai, bi = _CANDS[i]
            for j in range(i + 1, n):
                aj, bj = _CANDS[j]
                if ai <= aj and bi <= bj:
                    before[j] = before[j] + 1.0
                else:
                    t = jnp.where(vals[i] >= vals[j], 1.0, 0.0)
                    before[j] = before[j] + t
                    before[i] = before[i] + (1.0 - t)
        ex1 = [jnp.exp(sv1[a] - sv1[0]) for a in range(PEER_TOPK)]
        ex2 = [jnp.exp(sv2[b] - sv2[0]) for b in range(PEER_TOPK)]
        cnt = [jnp.zeros((PEER_HEADS, LANES), F32) for _ in range(PEER_TOPK)]
        zsum = jnp.zeros((PEER_HEADS, LANES), F32)
        for i, (a, b) in enumerate(_CANDS):
            sel = jnp.where(before[i] < float(PEER_TOPK), 1.0, 0.0)
            cnt[a] = cnt[a] + sel
            zsum = zsum + sel * ex1[a] * ex2[b]
        for a in range(PEER_TOPK):
            d_scr[a] = cnt[a] - (cnt[a + 1] if a + 1 < PEER_TOPK else 0.0)
        zi_scr[...] = 1.0 / zsum
        for h in range(PEER_HEADS):
            s = s_scr[lt, 2 * h * PEER_NKEYS:(2 * h + 1) * PEER_NKEYS, :]
            cc = jnp.zeros(s.shape, F32)
            for a in range(PEER_TOPK):
                cc = cc + jnp.where(s >= sv_scr[0, a, h:h + 1, :], d_scr[a, h:h + 1, :], 0.0)
            c_ref[lt, h] = cc
            e1_ref[lt, h] = jnp.exp(s - sv_scr[0, 0, h:h + 1, :]) * zi_scr[h:h + 1, :]
        return carry

    lax.fori_loop(0, nlt, lane_tile, 0)


def _peer1(h, g, wq_bf16, kbd_bf16, tt):
    t = h.shape[0]
    nlt = tt // LANES
    gsh = jax.ShapeDtypeStruct((t // LANES, PEER_HEADS, PEER_NKEYS, LANES), F32)
    gblk = pl.BlockSpec((nlt, PEER_HEADS, PEER_NKEYS, LANES), lambda i: (i, 0, 0, 0))
    return pl.pallas_call(
        functools.partial(_peer1_kernel, nlt=nlt),
        out_shape=(jax.ShapeDtypeStruct((t, D_MODEL), BF16), gsh, gsh, gsh, gsh),
        grid=(t // tt,),
        in_specs=[pl.BlockSpec((tt, D_MODEL), lambda i: (i, 0)),
                  pl.BlockSpec((1, D_MODEL), lambda i: (0, 0)),
                  pl.BlockSpec((D_MODEL, D_MODEL), lambda i: (0, 0)),
                  pl.BlockSpec((NKH, D_MODEL), lambda i: (0, 0))],
        out_specs=(pl.BlockSpec((tt, D_MODEL), lambda i: (i, 0)), gblk, gblk, gblk, gblk),
        scratch_shapes=[pltpu.VMEM((nlt, NKH, LANES), F32),
                        pltpu.VMEM((2, PEER_TOPK, PEER_HEADS, LANES), F32),
                        pltpu.VMEM((PEER_TOPK, PEER_HEADS, LANES), F32),
                        pltpu.VMEM((PEER_HEADS, LANES), F32)],
        compiler_params=_cparams(("parallel",)),
        name="peer1",
    )(h, g, wq_bf16, kbd_bf16)


NE1 = 8
ETILE = NE1 * PEER_NKEYS


def _peer2_kernel(h_ref, xn_ref, u_ref, vt_ref, e1_ref, c_ref, e2_ref, r2_ref, o_ref, acc_scr, w_scr, *, nlt):
    e = pl.program_id(1)

    @pl.when(e == 0)
    def _():
        acc_scr[...] = jnp.zeros(acc_scr.shape, F32)

    act = lax.dot_general(u_ref[...], xn_ref[...], (((1,), (1,)), ((), ())), preferred_element_type=F32)
    act = 0.5 * act * (1.0 + lax.erf(act * 0.7071067811865476))
    for lt in range(nlt):
        ls = slice(lt * LANES, (lt + 1) * LANES)
        for i1 in range(NE1):
            gate = jnp.zeros((PEER_NKEYS, LANES), F32)
            for h in range(PEER_HEADS):
                cnt = c_ref[lt, h, i1:i1 + 1, :]
                wgt = e1_ref[lt, h, i1:i1 + 1, :]
                gate = gate + jnp.where(r2_ref[lt, h] < cnt, e2_ref[lt, h], 0.0) * wgt
            rs = slice(i1 * PEER_NKEYS, (i1 + 1) * PEER_NKEYS)
            w_scr[rs, ls] = (gate * act[rs, ls]).astype(BF16)
    acc_scr[...] += jnp.dot(vt_ref[...], w_scr[...], preferred_element_type=F32)

    @pl.when(e == pl.num_programs(1) - 1)
    def _():
        o_ref[...] = h_ref[...] + acc_scr[...].T


def _peer2(h, xn, u_bf16, vt_bf16, e1, cc, e2, r2, tt):
    t = h.shape[0]
    nlt = tt // LANES
    ne = PEER_NKEYS // NE1
    gfull = pl.BlockSpec((nlt, PEER_HEADS, PEER_NKEYS, LANES), lambda i, e: (i, 0, 0, 0))
    gtile = pl.BlockSpec((nlt, PEER_HEADS, NE1, LANES), lambda i, e: (i, 0, e, 0))
    return pl.pallas_call(
        functools.partial(_peer2_kernel, nlt=nlt),
        out_shape=jax.ShapeDtypeStruct((t, D_MODEL), F32),
        grid=(t // tt, ne),
        in_specs=[pl.BlockSpec((tt, D_MODEL), lambda i, e: (i, 0)),
                  pl.BlockSpec((tt, D_MODEL), lambda i, e: (i, 0)),
                  pl.BlockSpec((ETILE, D_MODEL), lambda i, e: (e, 0)),
                  pl.BlockSpec((D_MODEL, ETILE), lambda i, e: (0, e)),
                  gtile, gtile, gfull, gfull],
        out_specs=pl.BlockSpec((tt, D_MODEL), lambda i, e: (i, 0)),
        scratch_shapes=[pltpu.VMEM((D_MODEL, tt), F32), pltpu.VMEM((ETILE, tt), BF16)],
        compiler_params=_cparams(("parallel", "arbitrary")),
        name="peer2",
    )(h, xn, u_bf16, vt_bf16, e1, cc, e2, r2)


def _prep_layer(i, w_in, lbs, hgrn_norm, gdn_conv_w, gdn_a_log, gdn_dt_bias, gdn_norm, conf_dw_w, conf_dw_b,
                conf_ln_g, conf_ln_b, pool_w, pool_scale, w_out, peer_wq, peer_keys, peer_u, peer_v, ple_gate_w,
                ple_proj_w):
    wi = w_in[i]
    hk = HEADS * DH
    c0 = 4 * hk
    qkv = wi[:, c0:c0 + 3 * hk]
    zbg = wi[:, c0 + 3 * hk:c0 + 4 * hk]
    zbb = wi[:, c0 + 4 * hk:c0 + 4 * hk + HEADS]
    zba = wi[:, c0 + 4 * hk + HEADS:c0 + 4 * hk + 2 * HEADS]
    rest = wi[:, c0 + 4 * hk + 2 * HEADS:]
    w_perm = jnp.concatenate([wi[:, :c0], qkv, zbg, rest, jnp.repeat(zbb, DH, axis=1), jnp.repeat(zba, DH, axis=1)],
                             axis=1).astype(BF16)
    keys = peer_keys[i].reshape(2 * PEER_HEADS, PEER_NKEYS, PEER_HALF)
    eye = jnp.eye(2 * PEER_HEADS, dtype=F32)
    kbd = (keys[:, :, None, :] * eye[:, None, :, None]).reshape(NKH, D_MODEL).astype(BF16)
    pw = pool_w[i]
    eye4 = jnp.eye(4, dtype=F32)
    pwbd = (pw[:, :, None, :] * eye4[:, None, :, None]).reshape(GW, GW)
    return dict(
        w_in=w_perm, lb=lbs[i][None, :], hgrn_norm=hgrn_norm[i][None, :],
        gdn_conv_w=gdn_conv_w[i], gdn_a_log=jnp.repeat(gdn_a_log[i], DH)[None, :],
        gdn_dt_bias=jnp.repeat(gdn_dt_bias[i], DH)[None, :], gdn_norm=gdn_norm[i][None, :],
        conf_dw_w=conf_dw_w[i], conf_dw_b=conf_dw_b[i][None, :], conf_ln_g=conf_ln_g[i][None, :],
        conf_ln_b=conf_ln_b[i][None, :], pool_w=pwbd, pool_scale=pool_scale[i][None, :],
        w_out=w_out[i].astype(BF16), peer_wq=peer_wq[i].astype(BF16), kbd=kbd,
        peer_u=peer_u[i].astype(BF16), peer_vt=peer_v[i].T.astype(BF16),
        ple_gate_w=ple_gate_w[i].astype(BF16), ple_proj_w=ple_proj_w[i].astype(BF16))


def _tail(buf, x, n):
    l = x.shape[1]
    if l >= n:
        return x[:, l - n:]
    return jnp.concatenate([buf[:, l:], x], axis=1)


def _trunk(x, p, pos0, s_a, s_b, c_b, c_c, c_d, layers, norms, norm_final):
    bsz, l, _ = x.shape
    t = bsz * l
    prompt = l >= 64
    c = 64 if prompt else l
    nb = 1 if prompt else 8
    cc = min(256, l)
    tm = min(512, t)
    tt = min(512, t)
    h = x.reshape(t, D_MODEL)
    n_a, n_b, nb_b, nb_c, nb_d = [], [], [], [], []
    for i, lw in enumerate(layers):
        z = _inproj(h, norms['mix'][i][None, :], lw['w_in'], tm)
        z3 = z.reshape(bsz, l, ZW)
        eye4 = jnp.eye(HEADS, dtype=F32)
        s0t = (jnp.swapaxes(s_a[i], 2, 3)[:, :, :, None, :] * eye4[None, :, None, :, None]).reshape(bsz, GW, GW)
        o_a, st = _hgrn(z3, lw['lb'], lw['hgrn_norm'], s0t, nb, c)
        st = st.reshape(bsz, HEADS, DH, HEADS, DH)
        n_a.append(jnp.stack([jnp.swapaxes(st[:, j, :, j, :], 1, 2) for j in range(HEADS)], axis=1))
        buf8 = jnp.pad(c_b[i], ((0, 0), (8 - (B_CONV - 1), 0), (0, 0)))
        o_b, sb = _gdn(z3, lw['gdn_conv_w'], lw['gdn_a_log'], lw['gdn_dt_bias'], lw['gdn_norm'], buf8, s_b[i], nb, c)
        n_b.append(sb)
        nb_b.append(_tail(c_b[i], z3[:, :, ZB_BQ * GW:(ZB_BV + 1) * GW], B_CONV - 1))
        cbuf = jnp.pad(c_c[i], ((0, 0), (CPAD - (C_CONV - 1), 0), (0, 0)))
        dbuf = jnp.pad(c_d[i], ((0, 0), (DPAD - POOL_BUF, 0), (0, 0)))
        o_c, u_c, o_d = _convpool(z3, lw['conf_dw_w'], lw['conf_dw_b'], lw['conf_ln_g'], lw['conf_ln_b'],
                                  lw['pool_w'], lw['pool_scale'], cbuf, dbuf, nb, cc, pos0)
        nb_c.append(_tail(c_c[i], u_c, C_CONV - 1))
        nb_d.append(_tail(c_d[i], z3[:, :, ZB_D * GW:(ZB_D + 1) * GW], POOL_BUF))
        h = _outproj(h, o_a.reshape(t, GW), o_b.reshape(t, GW), o_c.reshape(t, GW), o_d.reshape(t, GW),
                     lw['w_out'], tm)
        xn, e1, cnt, e2, r2 = _peer1(h, norms['ffn'][i][None, :], lw['peer_wq'], lw['kbd'], tt)
        h = _peer2(h, xn, lw['peer_u'], lw['peer_vt'], e1, cnt, e2, r2, tt)
        h = _ple(h, p[i].reshape(t, D_PLE), norms['ple'][i][None, :], lw['ple_gate_w'], lw['ple_proj_w'],
                 norm_final[None, :], tm, final=(i == len(layers) - 1))
    st = lambda lst: jnp.stack(lst, axis=0)
    return h.reshape(bsz, l, D_MODEL), st(n_a), st(n_b), st(nb_b), st(nb_c), st(nb_d)


def kernel(x_prompt, x_sample, state_hgrn, state_gdn, state_gdn_conv, state_conf_conv, state_pool, p_prompt, p_sample, norm_mix, w_in, hgrn_lb_logits, hgrn_norm, gdn_conv_w, gdn_a_log, gdn_dt_bias, gdn_norm, conf_dw_w, conf_dw_b, conf_ln_g, conf_ln_b, pool_w, pool_scale, w_out, norm_ffn, peer_wq, peer_keys, peer_u, peer_v, norm_ple, ple_gate_w, ple_proj_w, norm_final):
    sm = jax.nn.softmax(hgrn_lb_logits.astype(F32), axis=0)
    lbs = jnp.cumsum(sm, axis=0) - sm[0:1]
    layers = [_prep_layer(i, w_in, lbs, hgrn_norm, gdn_conv_w, gdn_a_log, gdn_dt_bias, gdn_norm, conf_dw_w,
                          conf_dw_b, conf_ln_g, conf_ln_b, pool_w, pool_scale, w_out, peer_wq, peer_keys, peer_u,
                          peer_v, ple_gate_w, ple_proj_w) for i in range(DEPTH)]
    norms = dict(mix=norm_mix, ffn=norm_ffn, ple=norm_ple)
    bp = x_prompt.shape[0]
    z = lambda *s: jnp.zeros((DEPTH, bp) + s, F32)
    yp, hp, gp, gcp, ccp, pp = _trunk(x_prompt, p_prompt, 0, z(HEADS, DH, DH), z(HEADS, DH, DH),
                                      z(B_CONV - 1, 3 * GW), z(C_CONV - 1, GW), z(POOL_BUF, GW),
                                      layers, norms, norm_final)
    ys, hs, gs, gcs, ccs, ps = _trunk(x_sample, p_sample, PAST_LEN, state_hgrn, state_gdn, state_gdn_conv,
                                      state_conf_conv, state_pool, layers, norms, norm_final)
    return (yp, ys, hp, gp, gcp, ccp, pp, hs, gs, gcs, ccs, ps)
```

```python
import functools

import numpy as np
import jax
import jax.numpy as jnp
from jax import lax
from jax.experimental import pallas as pl
from jax.experimental.pallas import tpu as pltpu

F32 = jnp.float32
BF16 = jnp.bfloat16

D_MODEL = 1024
DEPTH = 4
PAST_LEN = 16384
EPS = 1e-6
NEG_BIG = -1e30
D_PLE = 256
HEADS = 4
DH = 64
GW = 256
B_CONV = 4
C_CONV = 31
POOL_WINDOWS = (2, 4, 8, 16)
POOL_BUF = 15
PEER_HEADS = 8
PEER_NKEYS = 128
PEER_HALF = 64
PEER_TOPK = 16

ZB_AQ, ZB_AF, ZB_AI, ZB_AG, ZB_BQ, ZB_BK, ZB_BV, ZB_BG, ZB_CA, ZB_CB, ZB_D, ZB_BB, ZB_BA = range(13)
ZW = 13 * GW

LANES = 128
SOLVE_BLOCK = 16
VMEM_LIMIT = 56 * 1024 * 1024


def _cparams(sem, flags=None):
    return pltpu.CompilerParams(dimension_semantics=sem, vmem_limit_bytes=VMEM_LIMIT, flags=flags)


def _bdot(a, b):
    return jnp.dot(a.astype(BF16), b.astype(BF16), preferred_element_type=F32)


def _bdot_nt(a, b):
    return lax.dot_general(a.astype(BF16), b.astype(BF16), (((1,), (1,)), ((), ())),
                           preferred_element_type=F32)


def _split3(a):
    a1 = a.astype(BF16)
    r1 = a - a1.astype(F32)
    a2 = r1.astype(BF16)
    a3 = (r1 - a2.astype(F32)).astype(BF16)
    return a1, a2, a3


def _xdot(a, m01):
    a1, a2, a3 = _split3(a)
    m = m01.astype(BF16)
    return (jnp.dot(a1, m, preferred_element_type=F32) + jnp.dot(a2, m, preferred_element_type=F32)
            + jnp.dot(a3, m, preferred_element_type=F32))


def _xdot_left(m01, a):
    a1, a2, a3 = _split3(a)
    m = m01.astype(BF16)
    return (jnp.dot(m, a1, preferred_element_type=F32) + jnp.dot(m, a2, preferred_element_type=F32)
            + jnp.dot(m, a3, preferred_element_type=F32))


def _dot2(a, b):
    ah = a.astype(BF16)
    al = (a - ah.astype(F32)).astype(BF16)
    bh = b.astype(BF16)
    bl = (b - bh.astype(F32)).astype(BF16)
    return (jnp.dot(ah, bh, preferred_element_type=F32) + jnp.dot(ah, bl, preferred_element_type=F32)
            + jnp.dot(al, bh, preferred_element_type=F32))


def _xtranspose(a, eye):
    a1, a2, a3 = _split3(a)
    e = eye.astype(BF16)
    dn = (((1,), (1,)), ((), ()))
    return (lax.dot_general(e, a1, dn, preferred_element_type=F32)
            + lax.dot_general(e, a2, dn, preferred_element_type=F32)
            + lax.dot_general(e, a3, dn, preferred_element_type=F32))


def _iota(shape, axis):
    return lax.broadcasted_iota(jnp.int32, shape, axis)


def _tri_incl(c):
    return jnp.where(_iota((c, c), 1) <= _iota((c, c), 0), 1.0, 0.0).astype(F32)


def _eye(n):
    return jnp.where(_iota((n, n), 0) == _iota((n, n), 1), 1.0, 0.0).astype(F32)


def _block_ones(n, blk):
    return jnp.where(_iota((n, n), 0) // blk == _iota((n, n), 1) // blk, 1.0, 0.0).astype(F32)


def _rms(x, g):
    return x * lax.rsqrt(jnp.mean(x * x, axis=-1, keepdims=True) + EPS) * g


def _sigmoid(x):
    return jax.nn.sigmoid(x)


def _silu(x):
    return x * jax.nn.sigmoid(x)


def _inproj_kernel(x_ref, g_ref, w_ref, o_ref):
    xn = _rms(x_ref[...], g_ref[...])
    o_ref[...] = jnp.dot(xn.astype(BF16), w_ref[...], preferred_element_type=F32)


def _inproj(h, g, w_bf16, tm):
    t = h.shape[0]
    return pl.pallas_call(
        _inproj_kernel,
        out_shape=jax.ShapeDtypeStruct((t, ZW), F32),
        grid=(t // tm,),
        in_specs=[pl.BlockSpec((tm, D_MODEL), lambda i: (i, 0)),
                  pl.BlockSpec((1, D_MODEL), lambda i: (0, 0)),
                  pl.BlockSpec((D_MODEL, ZW), lambda i: (0, 0))],
        out_specs=pl.BlockSpec((tm, ZW), lambda i: (i, 0)),
        compiler_params=_cparams(("parallel",)),
        name="inproj",
    )(h, g, w_bf16)


def _outproj_kernel(h_ref, a_ref, b_ref, c_ref, d_ref, w_ref, o_ref):
    acc = h_ref[...]
    for i, r in enumerate((a_ref, b_ref, c_ref, d_ref)):
        acc = acc + jnp.dot(r[...].astype(BF16), w_ref[i * GW:(i + 1) * GW, :], preferred_element_type=F32)
    o_ref[...] = acc


def _outproj(h, oa, ob, oc, od, w_bf16, tm):
    t = h.shape[0]
    tok = lambda w: pl.BlockSpec((tm, w), lambda i: (i, 0))
    return pl.pallas_call(
        _outproj_kernel,
        out_shape=jax.ShapeDtypeStruct((t, D_MODEL), F32),
        grid=(t // tm,),
        in_specs=[tok(D_MODEL), tok(GW), tok(GW), tok(GW), tok(GW),
                  pl.BlockSpec((D_MODEL, D_MODEL), lambda i: (0, 0))],
        out_specs=tok(D_MODEL),
        compiler_params=_cparams(("parallel",)),
        name="outproj",
    )(h, oa, ob, oc, od, w_bf16)


def _ple_kernel(h_ref, p_ref, g_ref, wg_ref, wp_ref, gf_ref, o_ref, *, final):
    h = h_ref[...]
    xn = _rms(h, g_ref[...])
    gate = _sigmoid(jnp.dot(xn.astype(BF16), wg_ref[...], preferred_element_type=F32))
    proj = jnp.dot(p_ref[...].astype(BF16), wp_ref[...], preferred_element_type=F32)
    out = h + gate * proj
    if final:
        out = _rms(out, gf_ref[...])
    o_ref[...] = out


def _ple(h, p, g, wg_bf16, wp_bf16, gfinal, tm, final):
    t = h.shape[0]
    tok = lambda w: pl.BlockSpec((tm, w), lambda i: (i, 0))
    full = lambda r, c: pl.BlockSpec((r, c), lambda i: (0, 0))
    return pl.pallas_call(
        functools.partial(_ple_kernel, final=final),
        out_shape=jax.ShapeDtypeStruct((t, D_MODEL), F32),
        grid=(t // tm,),
        in_specs=[tok(D_MODEL), tok(D_PLE), full(1, D_MODEL), full(D_MODEL, D_MODEL), full(D_PLE, D_MODEL),
                  full(1, D_MODEL)],
        out_specs=tok(D_MODEL),
        compiler_params=_cparams(("parallel",)),
        name="ple_final" if final else "ple",
    )(h, p, g, wg_bf16, wp_bf16, gfinal)


def _hgrn_kernel(z_ref, lb_ref, nw_ref, s0_ref, o_ref, sf_ref, st_scr, la_scr, k_scr, v_scr, p_scr, *, nb, c):
    ci = pl.program_id(1)

    @pl.when(ci == 0)
    def _():
        st_scr[...] = s0_ref[...]

    lb = lb_ref[...]
    bones = _block_ones(GW, DH)
    tri = _tri_incl(c)
    rows = _iota((c, GW), 0)
    for b in range(nb):
        zq = z_ref[b, :, 0 * GW:1 * GW]
        zf = z_ref[b, :, 1 * GW:2 * GW]
        v = z_ref[b, :, 2 * GW:3 * GW]
        zg = z_ref[b, :, 3 * GW:4 * GW]
        q = _silu(zq)
        f = lb + (1.0 - lb) * _sigmoid(zf)
        k = (1.0 - lb) * _sigmoid(-zf)
        la = _xdot_left(tri, jnp.log(f))
        la_scr[...] = la
        k_scr[...] = k
        v_scr[...] = v

        def build(s, carry):
            la_s = la_scr[pl.ds(s, 1), :]
            k_s = k_scr[pl.ds(s, 1), :]
            dec = jnp.exp(jnp.minimum(la - la_s, 0.0))
            p_scr[pl.ds(pl.multiple_of(s * c, c), c), :] = jnp.where(rows >= s, q * k_s * dec, 0.0)
            return carry

        lax.fori_loop(0, c, build, 0, unroll=min(4, c))
        p_scr[...] = jnp.dot(p_scr[...].astype(BF16), bones.astype(BF16), preferred_element_type=F32)

        def consume(s, acc):
            return acc + p_scr[pl.ds(pl.multiple_of(s * c, c), c), :] * v_scr[pl.ds(s, 1), :]

        o = lax.fori_loop(0, c, consume, jnp.zeros((c, GW), F32), unroll=min(4, c))
        st = st_scr[b]
        o = o + _bdot_nt(q * jnp.exp(la), st)
        la_last = la[c - 1:c, :]
        k_dec = k * jnp.exp(la_last - la)
        upd = lax.dot_general(v.astype(BF16), k_dec.astype(BF16), (((0,), (0,)), ((), ())),
                              preferred_element_type=F32)
        st_scr[b] = st * jnp.exp(la_last) + upd * bones
        ms = _xdot(o * o, bones) * (1.0 / DH)
        o_ref[b] = o * lax.rsqrt(ms + EPS) * nw_ref[...] * _sigmoid(zg)

    @pl.when(ci == pl.num_programs(1) - 1)
    def _():
        sf_ref[...] = st_scr[...]


def _hgrn(z3, lb, nw, s0t, nb, c):
    bsz, l, _ = z3.shape
    return pl.pallas_call(
        functools.partial(_hgrn_kernel, nb=nb, c=c),
        out_shape=(jax.ShapeDtypeStruct((bsz, l, GW), F32), jax.ShapeDtypeStruct((bsz, GW, GW), F32)),
        grid=(bsz // nb, l // c),
        in_specs=[pl.BlockSpec((nb, c, 4 * GW), lambda b, i: (b, i, 0)),
                  pl.BlockSpec((1, GW), lambda b, i: (0, 0)),
                  pl.BlockSpec((1, GW), lambda b, i: (0, 0)),
                  pl.BlockSpec((nb, GW, GW), lambda b, i: (b, 0, 0))],
        out_specs=(pl.BlockSpec((nb, c, GW), lambda b, i: (b, i, 0)),
                   pl.BlockSpec((nb, GW, GW), lambda b, i: (b, 0, 0))),
        scratch_shapes=[pltpu.VMEM((nb, GW, GW), F32), pltpu.VMEM((c, GW), F32), pltpu.VMEM((c, GW), F32),
                        pltpu.VMEM((c, GW), F32), pltpu.VMEM((c * c, GW), F32)],
        compiler_params=_cparams(("parallel", "arbitrary")),
        name="hgrn",
    )(z3, lb, nw, s0t)


def _gdn_kernel(zq_ref, zk_ref, zv_ref, zg_ref, zb_ref, za_ref, cw_ref, alog_ref, dtb_ref, nw_ref, buf_ref, s0_ref,
                o_ref, sf_ref, s_scr, xp_scr, *, nb, c):
    ci = pl.program_id(1)

    @pl.when(ci == 0)
    def _():
        s_scr[...] = s0_ref[...]
        xp_scr[:, 0:8, :] = buf_ref[...]

    bones = _block_ones(GW, DH)
    tri = _tri_incl(c)
    eye_c = _eye(c)
    eye_h = _eye(DH)
    tt = _iota((c, c), 0)
    ss = _iota((c, c), 1)
    causal = tt >= ss
    strict = tt > ss
    items = []
    for b in range(nb):
        for j, r in enumerate((zq_ref, zk_ref, zv_ref)):
            xp_scr[b, 8:8 + c, j * GW:(j + 1) * GW] = r[b]
        conv = cw_ref[3:4, :] * xp_scr[b, 8:8 + c, :]
        for j in range(B_CONV - 1):
            conv = conv + cw_ref[j:j + 1, :] * xp_scr[b, 5 + j:5 + j + c, :]
        xp_scr[b, 0:8, :] = xp_scr[b, c:c + 8, :]
        qkv = _silu(conv)
        q = qkv[:, 0:GW]
        k = qkv[:, GW:2 * GW]
        v = qkv[:, 2 * GW:3 * GW]
        q = q * lax.rsqrt(_xdot(q * q, bones) + EPS) * (DH ** -0.5)
        k = k * lax.rsqrt(_xdot(k * k, bones) + EPS)
        beta = _sigmoid(zb_ref[b])
        loga = -jnp.exp(alog_ref[...]) * jax.nn.softplus(za_ref[b] + dtb_ref[...])
        g = _xdot_left(tri, loga)
        for h in range(HEADS):
            hs = slice(h * DH, (h + 1) * DH)
            items.append(dict(b=b, h=h, hs=hs, q=q[:, hs], k=k[:, hs], v=v[:, hs], beta=beta[:, hs], g=g[:, hs]))
    for it in items:
        it['gcol'] = it['g'][:, 0:c]
    for it in items:
        it['grow'] = _xtranspose(it['gcol'], eye_c)
    for it in items:
        it['kk'] = _bdot_nt(it['k'], it['k'])
        it['qk'] = _bdot_nt(it['q'], it['k'])
    for it in items:
        it['dec'] = jnp.exp(jnp.where(causal, it['gcol'] - it['grow'], NEG_BIG))
        it['a'] = jnp.where(strict, it['beta'][:, 0:c] * it['kk'] * it['dec'], 0.0)
        it['r'] = jnp.concatenate([it['beta'] * it['v'], it['beta'] * jnp.exp(it['g']) * it['k']], axis=1)
    blk = min(SOLVE_BLOCK, c)
    for it in items:
        it['solved'] = []
    for j0 in range(0, c, blk):
        rbs = [it['r'][j0:j0 + blk, :] for it in items]
        abs_ = [it['a'][j0:j0 + blk, j0:j0 + blk] for it in items]
        for s in range(blk - 1):
            rbs = [rb - ab[:, s:s + 1] * rb[s:s + 1, :] for rb, ab in zip(rbs, abs_)]
        for it, rb in zip(items, rbs):
            it['solved'].append(rb)
        if j0 + blk < c:
            belows = [_dot2(it['a'][j0 + blk:, j0:j0 + blk], rb) for it, rb in zip(items, rbs)]
            for it, below in zip(items, belows):
                it['r'] = jnp.concatenate([it['r'][:j0 + blk, :], it['r'][j0 + blk:, :] - below], axis=0)
    for it in items:
        w = it['solved'][0] if len(it['solved']) == 1 else jnp.concatenate(it['solved'], axis=0)
        it['w1'] = w[:, 0:DH]
        it['w2'] = w[:, DH:2 * DH]
        it['s'] = s_scr[it['b'], it['h']]
    for it in items:
        it['u'] = it['w1'] - _bdot(it['w2'], it['s'])
    for it in items:
        it['o'] = _bdot(it['q'] * jnp.exp(it['g']), it['s']) + _bdot(it['qk'] * it['dec'], it['u'])
    for it in items:
        g_last = it['g'][c - 1:c, :]
        kd = it['k'] * jnp.exp(g_last - it['g'])
        kd_t = lax.dot_general(eye_h.astype(BF16), kd.astype(BF16), (((1,), (1,)), ((), ())),
                               preferred_element_type=F32)
        it['snew'] = jnp.exp(g_last) * it['s'] + _bdot(kd_t, it['u'])
    for it in items:
        s_scr[it['b'], it['h']] = it['snew']
        o = it['o']
        zg = zg_ref[it['b'], :, it['hs']]
        o = o * lax.rsqrt(jnp.mean(o * o, axis=-1, keepdims=True) + EPS) * nw_ref[...] * _silu(zg)
        o_ref[it['b'], :, it['hs']] = o

    @pl.when(ci == pl.num_programs(1) - 1)
    def _():
        sf_ref[...] = s_scr[...]


def _gdn(z3, cw, alog_x, dtb_x, nw, buf8, s0, nb, c):
    bsz, l, _ = z3.shape
    zblk = lambda j: pl.BlockSpec((nb, c, GW), lambda b, i, j=j: (b, i, j))
    full = lambda r, w: pl.BlockSpec((r, w), lambda b, i: (0, 0))
    return pl.pallas_call(
        functools.partial(_gdn_kernel, nb=nb, c=c),
        out_shape=(jax.ShapeDtypeStruct((bsz, l, GW), F32), jax.ShapeDtypeStruct((bsz, HEADS, DH, DH), F32)),
        grid=(bsz // nb, l // c),
        in_specs=[zblk(ZB_BQ), zblk(ZB_BK), zblk(ZB_BV), zblk(ZB_BG), zblk(ZB_BB), zblk(ZB_BA),
                  full(B_CONV, 3 * GW), full(1, GW), full(1, GW), full(1, DH),
                  pl.BlockSpec((nb, 8, 3 * GW), lambda b, i: (b, 0, 0)),
                  pl.BlockSpec((nb, HEADS, DH, DH), lambda b, i: (b, 0, 0, 0))],
        out_specs=(pl.BlockSpec((nb, c, GW), lambda b, i: (b, i, 0)),
                   pl.BlockSpec((nb, HEADS, DH, DH), lambda b, i: (b, 0, 0, 0))),
        scratch_shapes=[pltpu.VMEM((nb, HEADS, DH, DH), F32), pltpu.VMEM((nb, 8 + c, 3 * GW), F32)],
        compiler_params=_cparams(("parallel", "arbitrary")),
        name="gdn",
    )(z3, z3, z3, z3, z3, z3, cw, alog_x, dtb_x, nw, buf8, s0)


CPAD = 32
DPAD = 16


def _convpool_kernel(za_ref, zb_ref, zd_ref, dw_ref, db_ref, lg_ref, lb_ref, pw_ref, ps_ref, cbuf_ref, dbuf_ref,
                     oc_ref, u_ref, od_ref, xc_scr, xd_scr, *, nb, c, pos0):
    ci = pl.program_id(1)

    @pl.when(ci == 0)
    def _():
        xc_scr[:, 0:CPAD, :] = cbuf_ref[...]
        xd_scr[:, 0:DPAD, :] = dbuf_ref[...]

    lane = _iota((c, GW), 1)
    wl = jnp.where(lane < 64, 2.0, jnp.where(lane < 128, 4.0, jnp.where(lane < 192, 8.0, 16.0)))
    pos = (_iota((c, GW), 0) + (ci * c + pos0 + 1)).astype(F32)
    cnt = jnp.minimum(wl, pos)
    for b in range(nb):
        u = za_ref[b] * _sigmoid(zb_ref[b])
        u_ref[b] = u
        xc_scr[b, CPAD:CPAD + c, :] = u
        y = dw_ref[C_CONV - 1:C_CONV, :] * u
        for j in range(C_CONV - 1):
            y = y + dw_ref[j:j + 1, :] * xc_scr[b, 2 + j:2 + j + c, :]
        xc_scr[b, 0:CPAD, :] = xc_scr[b, c:c + CPAD, :]
        y = y + db_ref[...]
        mu = jnp.mean(y, axis=-1, keepdims=True)
        yc = y - mu
        var = jnp.mean(yc * yc, axis=-1, keepdims=True)
        oc_ref[b] = _silu(yc * lax.rsqrt(var + EPS) * lg_ref[...] + lb_ref[...])
        x = zd_ref[b]
        xd_scr[b, DPAD:DPAD + c, :] = x
        acc = x
        sums = {}
        for i in range(1, 16):
            acc = acc + xd_scr[b, DPAD - i:DPAD - i + c, :]
            if i + 1 in POOL_WINDOWS:
                sums[i + 1] = acc
        xd_scr[b, 0:DPAD, :] = xd_scr[b, c:c + DPAD, :]
        ssel = jnp.where(lane < 64, sums[2], jnp.where(lane < 128, sums[4], jnp.where(lane < 192, sums[8], sums[16])))
        diff = ssel / cnt - x
        od_ref[b] = _bdot(diff, pw_ref[...]) * ps_ref[...]


def _convpool(z3, dw, db, lg, lb, pwbd, ps, cbuf, dbuf, nb, c, pos0):
    bsz, l, _ = z3.shape
    zblk = lambda j: pl.BlockSpec((nb, c, GW), lambda b, i, j=j: (b, i, j))
    full = lambda r, w: pl.BlockSpec((r, w), lambda b, i: (0, 0))
    oblk = pl.BlockSpec((nb, c, GW), lambda b, i: (b, i, 0))
    osh = jax.ShapeDtypeStruct((bsz, l, GW), F32)
    return pl.pallas_call(
        functools.partial(_convpool_kernel, nb=nb, c=c, pos0=pos0),
        out_shape=(osh, osh, osh),
        grid=(bsz // nb, l // c),
        in_specs=[zblk(ZB_CA), zblk(ZB_CB), zblk(ZB_D), full(C_CONV, GW), full(1, GW), full(1, GW), full(1, GW),
                  full(GW, GW), full(1, GW),
                  pl.BlockSpec((nb, CPAD, GW), lambda b, i: (b, 0, 0)),
                  pl.BlockSpec((nb, DPAD, GW), lambda b, i: (b, 0, 0))],
        out_specs=(oblk, oblk, oblk),
        scratch_shapes=[pltpu.VMEM((nb, CPAD + c, GW), F32), pltpu.VMEM((nb, DPAD + c, GW), F32)],
        compiler_params=_cparams(("parallel", "arbitrary")),
        name="convpool",
    )(z3, z3, z3, dw, db, lg, lb, pwbd, ps, cbuf, dbuf)


NKH = 2 * PEER_HEADS * PEER_NKEYS
_CANDS = [(a, b) for a in range(PEER_TOPK) for b in range(PEER_TOPK) if (a + 1) * (b + 1) <= PEER_TOPK]


def _peer1_kernel(h_ref, g_ref, wq_ref, kbd_ref, xn_ref, e1_ref, c_ref, e2_ref, r2_ref, s_scr, sv_scr, d_scr, zi_scr,
                  *, nlt):
    xn = _rms(h_ref[...], g_ref[...]).T.astype(BF16)
    xn_ref[...] = xn
    qt = jnp.dot(wq_ref[...], xn, preferred_element_type=F32).astype(BF16)
    for hp in range(2 * PEER_HEADS):
        sc = jnp.dot(kbd_ref[hp], qt[hp * PEER_HALF:(hp + 1) * PEER_HALF, :], preferred_element_type=F32)
        for lt in range(nlt):
            s_scr[lt, hp * PEER_NKEYS:(hp + 1) * PEER_NKEYS, :] = sc[:, lt * LANES:(lt + 1) * LANES]

    def lane_tile(lt, carry):
        for h in range(PEER_HEADS):
            for p in range(2):
                s = s_scr[lt, (2 * h + p) * PEER_NKEYS:(2 * h + p + 1) * PEER_NKEYS, :]
                prev = jnp.full((1, LANES), jnp.inf, F32)
                svs = []
                for a in range(PEER_TOPK):
                    m = jnp.max(jnp.where(s < prev, s, NEG_BIG), axis=0, keepdims=True)
                    sv_scr[p, a, h:h + 1, :] = m
                    svs.append(m)
                    prev = m
                if p == 1:
                    rank = jnp.zeros(s.shape, F32)
                    for a in range(PEER_TOPK):
                        rank = rank + jnp.where(s < svs[a], 1.0, 0.0)
                    r2_ref[lt, h] = pltpu.bitcast(rank.astype(BF16), jnp.uint32)
                    e2_ref[lt, h] = pltpu.bitcast(jnp.exp(s - svs[0]).astype(BF16), jnp.uint32)
        sv1 = [sv_scr[0, a] for a in range(PEER_TOPK)]
        sv2 = [sv_scr[1, a] for a in range(PEER_TOPK)]
        vals = [sv1[a] + sv2[b] for a, b in _CANDS]
        n = len(_CANDS)
        before = [jnp.zeros((PEER_HEADS, LANES), F32) for _ in range(n)]
        for i in range(n):
            ai, bi = _CANDS[i]
            for j in range(i + 1, n):
                aj, bj = _CANDS[j]
                if ai <= aj and bi <= bj:
                    before[j] = before[j] + 1.0
                else:
                    t = jnp.where(vals[i] >= vals[j], 1.0, 0.0)
                    before[j] = before[j] + t
                    before[i] = before[i] + (1.0 - t)
        ex1 = [jnp.exp(sv1[a] - sv1[0]) for a in range(PEER_TOPK)]
        ex2 = [jnp.exp(sv2[b] - sv2[0]) for b in range(PEER_TOPK)]
        cnt = [jnp.zeros((PEER_HEADS, LANES), F32) for _ in range(PEER_TOPK)]
        zsum = jnp.zeros((PEER_HEADS, LANES), F32)
        for i, (a, b) in enumerate(_CANDS):
            sel = jnp.where(before[i] < float(PEER_TOPK), 1.0, 0.0)
            cnt[a] = cnt[a] + sel
            zsum = zsum + sel * ex1[a] * ex2[b]
        for a in range(PEER_TOPK):
            d_scr[a] = cnt[a] - (cnt[a + 1] if a + 1 < PEER_TOPK else 0.0)
        zi_scr[...] = 1.0 / zsum
        for h in range(PEER_HEADS):
            s = s_scr[lt, 2 * h * PEER_NKEYS:(2 * h + 1) * PEER_NKEYS, :]
            cc = jnp.zeros(s.shape, F32)
            for a in range(PEER_TOPK):
                cc = cc + jnp.where(s >= sv_scr[0, a, h:h + 1, :], d_scr[a, h:h + 1, :], 0.0)
            c_ref[lt, h] = cc
            e1_ref[lt, h] = jnp.exp(s - sv_scr[0, 0, h:h + 1, :]) * zi_scr[h:h + 1, :]
        return carry

    lax.fori_loop(0, nlt, lane_tile, 0)


def _peer1(h, g, wq_bf16, kbd_bf16, tt):
    t = h.shape[0]
    nlt = tt // LANES
    gsh = jax.ShapeDtypeStruct((t // LANES, PEER_HEADS, PEER_NKEYS, LANES), F32)
    gblk = pl.BlockSpec((nlt, PEER_HEADS, PEER_NKEYS, LANES), lambda i: (i, 0, 0, 0))
    psh = jax.ShapeDtypeStruct((t // LANES, PEER_HEADS, PEER_NKEYS // 2, LANES), jnp.uint32)
    pblk = pl.BlockSpec((nlt, PEER_HEADS, PEER_NKEYS // 2, LANES), lambda i: (i, 0, 0, 0))
    return pl.pallas_call(
        functools.partial(_peer1_kernel, nlt=nlt),
        out_shape=(jax.ShapeDtypeStruct((D_MODEL, t), BF16), gsh, gsh, psh, psh),
        grid=(t // tt,),
        in_specs=[pl.BlockSpec((tt, D_MODEL), lambda i: (i, 0)),
                  pl.BlockSpec((1, D_MODEL), lambda i: (0, 0)),
                  pl.BlockSpec((D_MODEL, D_MODEL), lambda i: (0, 0)),
                  pl.BlockSpec((2 * PEER_HEADS, PEER_NKEYS, PEER_HALF), lambda i: (0, 0, 0))],
        out_specs=(pl.BlockSpec((D_MODEL, tt), lambda i: (0, i)), gblk, gblk, pblk, pblk),
        scratch_shapes=[pltpu.VMEM((nlt, NKH, LANES), F32),
                        pltpu.VMEM((2, PEER_TOPK, PEER_HEADS, LANES), F32),
                        pltpu.VMEM((PEER_TOPK, PEER_HEADS, LANES), F32),
                        pltpu.VMEM((PEER_HEADS, LANES), F32)],
        compiler_params=_cparams(("parallel",)),
        name="peer1",
    )(h, g, wq_bf16, kbd_bf16)


NE1 = 8
ETILE = NE1 * PEER_NKEYS


def _peer2_step(u_ref, xn_ref, vt_ref, e1_ref, c_ref, e2_ref, r2_ref, acc_scr, act_w, act_r, w_w, w_r, nlt):
    tile = (PEER_NKEYS, LANES)
    zero = jnp.zeros(tile, BF16)
    grp = 2
    mblk = D_MODEL // 4

    def value_rows(m):
        rs = slice(m * mblk, (m + 1) * mblk)
        acc_scr[rs, :] += jnp.dot(vt_ref[rs, :], pltpu.bitcast(w_r[...], BF16), preferred_element_type=F32)

    def act_rows(m):
        a = jnp.dot(u_ref[m * mblk:(m + 1) * mblk, :], xn_ref[...], preferred_element_type=F32)
        a = 0.5 * a * (1.0 + lax.erf(a * 0.7071067811865476))
        act_w[m * mblk // 2:(m + 1) * mblk // 2, :] = pltpu.bitcast(a.astype(BF16), jnp.uint32)

    def gate_group(lt, g0):
        ls = slice(lt * LANES, (lt + 1) * LANES)
        gates = [zero] * grp
        for h in range(PEER_HEADS):
            rank = pltpu.bitcast(r2_ref[lt, h], BF16)
            wkey = pltpu.bitcast(e2_ref[lt, h], BF16)
            for j in range(grp):
                i1 = g0 + j
                cnt = jnp.broadcast_to(c_ref[lt, h, i1:i1 + 1, :], tile).astype(BF16)
                wgt = jnp.broadcast_to(e1_ref[lt, h, i1:i1 + 1, :], tile).astype(BF16)
                gates[j] = gates[j] + jnp.where(rank < cnt, wkey, zero) * wgt
        for j in range(grp):
            ps = slice((g0 + j) * PEER_NKEYS // 2, (g0 + j + 1) * PEER_NKEYS // 2)
            w_w[ps, ls] = pltpu.bitcast(gates[j] * pltpu.bitcast(act_r[ps, ls], BF16), jnp.uint32)

    groups = [(lt, g0) for lt in range(nlt) for g0 in range(0, NE1, grp)]
    mxu_work = [f for m in range(4) for f in (functools.partial(value_rows, m), functools.partial(act_rows, m))]
    per = max(1, len(groups) // len(mxu_work))
    gi = 0
    for k, mm in enumerate(mxu_work):
        mm()
        take = len(groups) - gi if k == len(mxu_work) - 1 else per
        for _ in range(take):
            if gi < len(groups):
                gate_group(*groups[gi])
                gi += 1


def _peer2_kernel(h_ref, xn_ref, u_ref, vt_ref, e1_ref, c_ref, e2_ref, r2_ref, o_ref, acc_scr, act0, act1, w0, w1, *,
                  nlt):
    e = pl.program_id(1)

    @pl.when(e == 0)
    def _():
        acc_scr[...] = jnp.zeros(acc_scr.shape, F32)
        for r in (act0, act1, w0, w1):
            r[...] = jnp.zeros(r.shape, jnp.uint32)

    args = (u_ref, xn_ref, vt_ref, e1_ref, c_ref, e2_ref, r2_ref, acc_scr)

    @pl.when(e % 2 == 0)
    def _():
        _peer2_step(*args, act0, act1, w1, w0, nlt)

    @pl.when(e % 2 == 1)
    def _():
        _peer2_step(*args, act1, act0, w0, w1, nlt)

    @pl.when(e == pl.num_programs(1) - 1)
    def _():
        o_ref[...] = h_ref[...] + acc_scr[...].T


def _peer2(h, xn, u_bf16, vt_bf16, e1, cc, e2, r2, tt):
    t = h.shape[0]
    nlt = tt // LANES
    ne = PEER_NKEYS // NE1
    last = ne - 1
    pfull = pl.BlockSpec((nlt, PEER_HEADS, PEER_NKEYS // 2, LANES), lambda i, e: (i, 0, 0, 0))
    gtile = pl.BlockSpec((nlt, PEER_HEADS, NE1, LANES), lambda i, e: (i, 0, jnp.clip(e - 1, 0, last), 0))
    slot = pltpu.VMEM((ETILE // 2, tt), jnp.uint32)
    return pl.pallas_call(
        functools.partial(_peer2_kernel, nlt=nlt),
        out_shape=jax.ShapeDtypeStruct((t, D_MODEL), F32),
        grid=(t // tt, ne + 2),
        in_specs=[pl.BlockSpec((tt, D_MODEL), lambda i, e: (i, 0)),
                  pl.BlockSpec((D_MODEL, tt), lambda i, e: (0, i)),
                  pl.BlockSpec((ETILE, D_MODEL), lambda i, e: (jnp.minimum(e, last), 0)),
                  pl.BlockSpec((D_MODEL, ETILE), lambda i, e: (0, jnp.clip(e - 2, 0, last))),
                  gtile, gtile, pfull, pfull],
        out_specs=pl.BlockSpec((tt, D_MODEL), lambda i, e: (i, 0)),
        scratch_shapes=[pltpu.VMEM((D_MODEL, tt), F32), slot, slot, slot, slot],
        compiler_params=_cparams(("parallel", "arbitrary")),
        name="peer2",
    )(h, xn, u_bf16, vt_bf16, e1, cc, e2, r2)


def _prep_layer(i, w_in, lbs, hgrn_norm, gdn_conv_w, gdn_a_log, gdn_dt_bias, gdn_norm, conf_dw_w, conf_dw_b,
                conf_ln_g, conf_ln_b, pool_w, pool_scale, w_out, peer_wq, peer_keys, peer_u, peer_v, ple_gate_w,
                ple_proj_w):
    wi = w_in[i]
    hk = HEADS * DH
    c0 = 4 * hk
    qkv = wi[:, c0:c0 + 3 * hk]
    zbg = wi[:, c0 + 3 * hk:c0 + 4 * hk]
    zbb = wi[:, c0 + 4 * hk:c0 + 4 * hk + HEADS]
    zba = wi[:, c0 + 4 * hk + HEADS:c0 + 4 * hk + 2 * HEADS]
    rest = wi[:, c0 + 4 * hk + 2 * HEADS:]
    w_perm = jnp.concatenate([wi[:, :c0], qkv, zbg, rest, jnp.repeat(zbb, DH, axis=1), jnp.repeat(zba, DH, axis=1)],
                             axis=1).astype(BF16)
    kbd = peer_keys[i].reshape(2 * PEER_HEADS, PEER_NKEYS, PEER_HALF).astype(BF16)
    pw = pool_w[i]
    eye4 = jnp.eye(4, dtype=F32)
    pwbd = (pw[:, :, None, :] * eye4[:, None, :, None]).reshape(GW, GW)
    return dict(
        w_in=w_perm, lb=lbs[i][None, :], hgrn_norm=hgrn_norm[i][None, :],
        gdn_conv_w=gdn_conv_w[i], gdn_a_log=jnp.repeat(gdn_a_log[i], DH)[None, :],
        gdn_dt_bias=jnp.repeat(gdn_dt_bias[i], DH)[None, :], gdn_norm=gdn_norm[i][None, :],
        conf_dw_w=conf_dw_w[i], conf_dw_b=conf_dw_b[i][None, :], conf_ln_g=conf_ln_g[i][None, :],
        conf_ln_b=conf_ln_b[i][None, :], pool_w=pwbd, pool_scale=pool_scale[i][None, :],
        w_out=w_out[i].astype(BF16), peer_wq=peer_wq[i].T.astype(BF16), kbd=kbd,
        peer_u=peer_u[i].astype(BF16), peer_vt=peer_v[i].T.astype(BF16),
        ple_gate_w=ple_gate_w[i].astype(BF16), ple_proj_w=ple_proj_w[i].astype(BF16))


def _tail(buf, x, n):
    l = x.shape[1]
    if l >= n:
        return x[:, l - n:]
    return jnp.concatenate([buf[:, l:], x], axis=1)


def _trunk(x, p, pos0, s_a, s_b, c_b, c_c, c_d, layers, norms, norm_final):
    bsz, l, _ = x.shape
    t = bsz * l
    prompt = l >= 64
    c = 64 if prompt else l
    nb = 1 if prompt else 8
    cc = min(256, l)
    tm = min(512, t)
    tt = min(512, t)
    h = x.reshape(t, D_MODEL)
    n_a, n_b, nb_b, nb_c, nb_d = [], [], [], [], []
    for i, lw in enumerate(layers):
        z = _inproj(h, norms['mix'][i][None, :], lw['w_in'], tm)
        z3 = z.reshape(bsz, l, ZW)
        eye4 = jnp.eye(HEADS, dtype=F32)
        s0t = (jnp.swapaxes(s_a[i], 2, 3)[:, :, :, None, :] * eye4[None, :, None, :, None]).reshape(bsz, GW, GW)
        o_a, st = _hgrn(z3, lw['lb'], lw['hgrn_norm'], s0t, nb, c)
        st = st.reshape(bsz, HEADS, DH, HEADS, DH)
        n_a.append(jnp.stack([jnp.swapaxes(st[:, j, :, j, :], 1, 2) for j in range(HEADS)], axis=1))
        buf8 = jnp.pad(c_b[i], ((0, 0), (8 - (B_CONV - 1), 0), (0, 0)))
        o_b, sb = _gdn(z3, lw['gdn_conv_w'], lw['gdn_a_log'], lw['gdn_dt_bias'], lw['gdn_norm'], buf8, s_b[i],
                       min(2, bsz) if prompt else nb, c)
        n_b.append(sb)
        nb_b.append(_tail(c_b[i], z3[:, :, ZB_BQ * GW:(ZB_BV + 1) * GW], B_CONV - 1))
        cbuf = jnp.pad(c_c[i], ((0, 0), (CPAD - (C_CONV - 1), 0), (0, 0)))
        dbuf = jnp.pad(c_d[i], ((0, 0), (DPAD - POOL_BUF, 0), (0, 0)))
        o_c, u_c, o_d = _convpool(z3, lw['conf_dw_w'], lw['conf_dw_b'], lw['conf_ln_g'], lw['conf_ln_b'],
                                  lw['pool_w'], lw['pool_scale'], cbuf, dbuf, nb, cc, pos0)
        nb_c.append(_tail(c_c[i], u_c, C_CONV - 1))
        nb_d.append(_tail(c_d[i], z3[:, :, ZB_D * GW:(ZB_D + 1) * GW], POOL_BUF))
        h = _outproj(h, o_a.reshape(t, GW), o_b.reshape(t, GW), o_c.reshape(t, GW), o_d.reshape(t, GW),
                     lw['w_out'], tm)
        xn, e1, cnt, e2, r2 = _peer1(h, norms['ffn'][i][None, :], lw['peer_wq'], lw['kbd'], tt)
        h = _peer2(h, xn, lw['peer_u'], lw['peer_vt'], e1, cnt, e2, r2, tt)
        h = _ple(h, p[i].reshape(t, D_PLE), norms['ple'][i][None, :], lw['ple_gate_w'], lw['ple_proj_w'],
                 norm_final[None, :], tm, final=(i == len(layers) - 1))
    st = lambda lst: jnp.stack(lst, axis=0)
    return h.reshape(bsz, l, D_MODEL), st(n_a), st(n_b), st(nb_b), st(nb_c), st(nb_d)


def kernel(x_prompt, x_sample, state_hgrn, state_gdn, state_gdn_conv, state_conf_conv, state_pool, p_prompt, p_sample, norm_mix, w_in, hgrn_lb_logits, hgrn_norm, gdn_conv_w, gdn_a_log, gdn_dt_bias, gdn_norm, conf_dw_w, conf_dw_b, conf_ln_g, conf_ln_b, pool_w, pool_scale, w_out, norm_ffn, peer_wq, peer_keys, peer_u, peer_v, norm_ple, ple_gate_w, ple_proj_w, norm_final):
    sm = jax.nn.softmax(hgrn_lb_logits.astype(F32), axis=0)
    lbs = jnp.cumsum(sm, axis=0) - sm[0:1]
    layers = [_prep_layer(i, w_in, lbs, hgrn_norm, gdn_conv_w, gdn_a_log, gdn_dt_bias, gdn_norm, conf_dw_w,
                          conf_dw_b, conf_ln_g, conf_ln_b, pool_w, pool_scale, w_out, peer_wq, peer_keys, peer_u,
                          peer_v, ple_gate_w, ple_proj_w) for i in range(DEPTH)]
    norms = dict(mix=norm_mix, ffn=norm_ffn, ple=norm_ple)
    bp = x_prompt.shape[0]
    z = lambda *s: jnp.zeros((DEPTH, bp) + s, F32)
    yp, hp, gp, gcp, ccp, pp = _trunk(x_prompt, p_prompt, 0, z(HEADS, DH, DH), z(HEADS, DH, DH),
                                      z(B_CONV - 1, 3 * GW), z(C_CONV - 1, GW), z(POOL_BUF, GW),
                                      layers, norms, norm_final)
    ys, hs, gs, gcs, ccs, ps = _trunk(x_sample, p_sample, PAST_LEN, state_hgrn, state_gdn, state_gdn_conv,
                                      state_conf_conv, state_pool, layers, norms, norm_final)
    return (yp, ys, hp, gp, gcp, ccp, pp, hs, gs, gcs, ccs, ps)
```

```python
import functools

import numpy as np
import jax
import jax.numpy as jnp
from jax import lax
from jax.experimental import pallas as pl
from jax.experimental.pallas import tpu as pltpu

F32 = jnp.float32
BF16 = jnp.bfloat16

D_MODEL = 1024
DEPTH = 4
PAST_LEN = 16384
EPS = 1e-6
NEG_BIG = -1e30
D_PLE = 256
HEADS = 4
DH = 64
GW = 256
B_CONV = 4
C_CONV = 31
POOL_WINDOWS = (2, 4, 8, 16)
POOL_BUF = 15
PEER_HEADS = 8
PEER_NKEYS = 128
PEER_HALF = 64
PEER_TOPK = 16

ZB_AQ, ZB_AF, ZB_AI, ZB_AG, ZB_BQ, ZB_BK, ZB_BV, ZB_BG, ZB_CA, ZB_CB, ZB_D, ZB_BB, ZB_BA = range(13)
ZW = 13 * GW

LANES = 128
SOLVE_BLOCK = 16
VMEM_LIMIT = 56 * 1024 * 1024


def _cparams(sem, flags=None):
    return pltpu.CompilerParams(dimension_semantics=sem, vmem_limit_bytes=VMEM_LIMIT, flags=flags)


def _bdot(a, b):
    return jnp.dot(a.astype(BF16), b.astype(BF16), preferred_element_type=F32)


def _bdot_nt(a, b):
    return lax.dot_general(a.astype(BF16), b.astype(BF16), (((1,), (1,)), ((), ())),
                           preferred_element_type=F32)


def _split3(a):
    a1 = a.astype(BF16)
    r1 = a - a1.astype(F32)
    a2 = r1.astype(BF16)
    a3 = (r1 - a2.astype(F32)).astype(BF16)
    return a1, a2, a3


def _xdot(a, m01):
    a1, a2, a3 = _split3(a)
    m = m01.astype(BF16)
    return (jnp.dot(a1, m, preferred_element_type=F32) + jnp.dot(a2, m, preferred_element_type=F32)
            + jnp.dot(a3, m, preferred_element_type=F32))


def _xdot_left(m01, a):
    a1, a2, a3 = _split3(a)
    m = m01.astype(BF16)
    return (jnp.dot(m, a1, preferred_element_type=F32) + jnp.dot(m, a2, preferred_element_type=F32)
            + jnp.dot(m, a3, preferred_element_type=F32))


def _dot2(a, b):
    ah = a.astype(BF16)
    al = (a - ah.astype(F32)).astype(BF16)
    bh = b.astype(BF16)
    bl = (b - bh.astype(F32)).astype(BF16)
    return (jnp.dot(ah, bh, preferred_element_type=F32) + jnp.dot(ah, bl, preferred_element_type=F32)
            + jnp.dot(al, bh, preferred_element_type=F32))


def _xtranspose(a, eye):
    a1, a2, a3 = _split3(a)
    e = eye.astype(BF16)
    dn = (((1,), (1,)), ((), ()))
    return (lax.dot_general(e, a1, dn, preferred_element_type=F32)
            + lax.dot_general(e, a2, dn, preferred_element_type=F32)
            + lax.dot_general(e, a3, dn, preferred_element_type=F32))


def _iota(shape, axis):
    return lax.broadcasted_iota(jnp.int32, shape, axis)


def _tri_incl(c):
    return jnp.where(_iota((c, c), 1) <= _iota((c, c), 0), 1.0, 0.0).astype(F32)


def _eye(n):
    return jnp.where(_iota((n, n), 0) == _iota((n, n), 1), 1.0, 0.0).astype(F32)


def _block_ones(n, blk):
    return jnp.where(_iota((n, n), 0) // blk == _iota((n, n), 1) // blk, 1.0, 0.0).astype(F32)


def _rms(x, g):
    return x * lax.rsqrt(jnp.mean(x * x, axis=-1, keepdims=True) + EPS) * g


def _sigmoid(x):
    return jax.nn.sigmoid(x)


def _silu(x):
    return x * jax.nn.sigmoid(x)


def _inproj_kernel(x_ref, g_ref, w_ref, o_ref):
    xn = _rms(x_ref[...], g_ref[...])
    o_ref[...] = jnp.dot(xn.astype(BF16), w_ref[...], preferred_element_type=F32)


def _inproj(h, g, w_bf16, tm):
    t = h.shape[0]
    return pl.pallas_call(
        _inproj_kernel,
        out_shape=jax.ShapeDtypeStruct((t, ZW), F32),
        grid=(t // tm,),
        in_specs=[pl.BlockSpec((tm, D_MODEL), lambda i: (i, 0)),
                  pl.BlockSpec((1, D_MODEL), lambda i: (0, 0)),
                  pl.BlockSpec((D_MODEL, ZW), lambda i: (0, 0))],
        out_specs=pl.BlockSpec((tm, ZW), lambda i: (i, 0)),
        compiler_params=_cparams(("parallel",)),
        name="inproj",
    )(h, g, w_bf16)


def _outproj_kernel(h_ref, a_ref, b_ref, c_ref, d_ref, w_ref, o_ref):
    acc = h_ref[...]
    for i, r in enumerate((a_ref, b_ref, c_ref, d_ref)):
        acc = acc + jnp.dot(r[...].astype(BF16), w_ref[i * GW:(i + 1) * GW, :], preferred_element_type=F32)
    o_ref[...] = acc


def _outproj(h, oa, ob, oc, od, w_bf16, tm):
    t = h.shape[0]
    tok = lambda w: pl.BlockSpec((tm, w), lambda i: (i, 0))
    return pl.pallas_call(
        _outproj_kernel,
        out_shape=jax.ShapeDtypeStruct((t, D_MODEL), F32),
        grid=(t // tm,),
        in_specs=[tok(D_MODEL), tok(GW), tok(GW), tok(GW), tok(GW),
                  pl.BlockSpec((D_MODEL, D_MODEL), lambda i: (0, 0))],
        out_specs=tok(D_MODEL),
        compiler_params=_cparams(("parallel",)),
        name="outproj",
    )(h, oa, ob, oc, od, w_bf16)


def _ple_kernel(h_ref, p_ref, g_ref, wg_ref, wp_ref, gf_ref, o_ref, *, final):
    h = h_ref[...]
    xn = _rms(h, g_ref[...])
    gate = _sigmoid(jnp.dot(xn.astype(BF16), wg_ref[...], preferred_element_type=F32))
    proj = jnp.dot(p_ref[...].astype(BF16), wp_ref[...], preferred_element_type=F32)
    out = h + gate * proj
    if final:
        out = _rms(out, gf_ref[...])
    o_ref[...] = out


def _ple(h, p, g, wg_bf16, wp_bf16, gfinal, tm, final):
    t = h.shape[0]
    tok = lambda w: pl.BlockSpec((tm, w), lambda i: (i, 0))
    full = lambda r, c: pl.BlockSpec((r, c), lambda i: (0, 0))
    return pl.pallas_call(
        functools.partial(_ple_kernel, final=final),
        out_shape=jax.ShapeDtypeStruct((t, D_MODEL), F32),
        grid=(t // tm,),
        in_specs=[tok(D_MODEL), tok(D_PLE), full(1, D_MODEL), full(D_MODEL, D_MODEL), full(D_PLE, D_MODEL),
                  full(1, D_MODEL)],
        out_specs=tok(D_MODEL),
        compiler_params=_cparams(("parallel",)),
        name="ple_final" if final else "ple",
    )(h, p, g, wg_bf16, wp_bf16, gfinal)


def _hgrn_kernel(z_ref, lb_ref, nw_ref, s0_ref, o_ref, sf_ref, st_scr, la_scr, q_scr, k_scr, v_scr, p_scr, *, nb, c):
    ci = pl.program_id(1)

    @pl.when(ci == 0)
    def _():
        st_scr[...] = s0_ref[...]

    lb = lb_ref[...]
    bones = _block_ones(GW, DH)
    tri = _tri_incl(c)
    rows = _iota((c, GW), 0)
    seqs = range(nb)
    for b in seqs:
        zf = z_ref[b, :, 1 * GW:2 * GW]
        q_scr[b] = _silu(z_ref[b, :, 0 * GW:1 * GW])
        k_scr[b] = (1.0 - lb) * _sigmoid(-zf)
        v_scr[b] = z_ref[b, :, 2 * GW:3 * GW]
        f = lb + (1.0 - lb) * _sigmoid(zf)
        la_scr[b] = _xdot_left(tri, jnp.log(f))

    def build(s, carry):
        for b in seqs:
            la_s = la_scr[b, pl.ds(s, 1), :]
            k_s = k_scr[b, pl.ds(s, 1), :]
            dec = jnp.exp(jnp.minimum(la_scr[b] - la_s, 0.0))
            p_scr[b, pl.ds(pl.multiple_of(s * c, c), c), :] = jnp.where(rows >= s, q_scr[b] * k_s * dec, 0.0)
        return carry

    lax.fori_loop(0, c, build, 0, unroll=min(4, c))
    for b in seqs:
        p_scr[b] = jnp.dot(p_scr[b].astype(BF16), bones.astype(BF16), preferred_element_type=F32)

    def consume(s, accs):
        return tuple(acc + p_scr[b, pl.ds(pl.multiple_of(s * c, c), c), :] * v_scr[b, pl.ds(s, 1), :]
                     for b, acc in zip(seqs, accs))

    os_ = lax.fori_loop(0, c, consume, tuple(jnp.zeros((c, GW), F32) for _ in seqs), unroll=min(4, c))
    sts = [st_scr[b] for b in seqs]
    las = [la_scr[b] for b in seqs]
    os_ = [o + _bdot_nt(q_scr[b] * jnp.exp(la), st) for b, o, la, st in zip(seqs, os_, las, sts)]
    upds = []
    for b, la in zip(seqs, las):
        k_dec = k_scr[b] * jnp.exp(la[c - 1:c, :] - la)
        upds.append(lax.dot_general(v_scr[b].astype(BF16), k_dec.astype(BF16), (((0,), (0,)), ((), ())),
                                    preferred_element_type=F32))
    for b, la, st, upd in zip(seqs, las, sts, upds):
        st_scr[b] = st * jnp.exp(la[c - 1:c, :]) + upd * bones
    mss = [_xdot(o * o, bones) * (1.0 / DH) for o in os_]
    for b, o, ms in zip(seqs, os_, mss):
        o_ref[b] = o * lax.rsqrt(ms + EPS) * nw_ref[...] * _sigmoid(z_ref[b, :, 3 * GW:4 * GW])

    @pl.when(ci == pl.num_programs(1) - 1)
    def _():
        sf_ref[...] = st_scr[...]


def _hgrn(z3, lb, nw, s0t, nb, c):
    bsz, l, _ = z3.shape
    return pl.pallas_call(
        functools.partial(_hgrn_kernel, nb=nb, c=c),
        out_shape=(jax.ShapeDtypeStruct((bsz, l, GW), F32), jax.ShapeDtypeStruct((bsz, GW, GW), F32)),
        grid=(bsz // nb, l // c),
        in_specs=[pl.BlockSpec((nb, c, 4 * GW), lambda b, i: (b, i, 0)),
                  pl.BlockSpec((1, GW), lambda b, i: (0, 0)),
                  pl.BlockSpec((1, GW), lambda b, i: (0, 0)),
                  pl.BlockSpec((nb, GW, GW), lambda b, i: (b, 0, 0))],
        out_specs=(pl.BlockSpec((nb, c, GW), lambda b, i: (b, i, 0)),
                   pl.BlockSpec((nb, GW, GW), lambda b, i: (b, 0, 0))),
        scratch_shapes=[pltpu.VMEM((nb, GW, GW), F32)] + [pltpu.VMEM((nb, c, GW), F32)] * 4
                       + [pltpu.VMEM((nb, c * c, GW), F32)],
        compiler_params=_cparams(("parallel", "arbitrary")),
        name="hgrn",
    )(z3, lb, nw, s0t)


def _gdn_kernel(zq_ref, zk_ref, zv_ref, zg_ref, zb_ref, za_ref, cw_ref, alog_ref, dtb_ref, nw_ref, buf_ref, s0_ref,
                o_ref, sf_ref, s_scr, xp_scr, *, nb, c):
    ci = pl.program_id(1)

    @pl.when(ci == 0)
    def _():
        s_scr[...] = s0_ref[...]
        xp_scr[:, 0:8, :] = buf_ref[...]

    bones = _block_ones(GW, DH)
    tri = _tri_incl(c)
    eye_c = _eye(c)
    eye_h = _eye(DH)
    tt = _iota((c, c), 0)
    ss = _iota((c, c), 1)
    causal = tt >= ss
    strict = tt > ss
    items = []
    for b in range(nb):
        for j, r in enumerate((zq_ref, zk_ref, zv_ref)):
            xp_scr[b, 8:8 + c, j * GW:(j + 1) * GW] = r[b]
        conv = cw_ref[3:4, :] * xp_scr[b, 8:8 + c, :]
        for j in range(B_CONV - 1):
            conv = conv + cw_ref[j:j + 1, :] * xp_scr[b, 5 + j:5 + j + c, :]
        xp_scr[b, 0:8, :] = xp_scr[b, c:c + 8, :]
        qkv = _silu(conv)
        q = qkv[:, 0:GW]
        k = qkv[:, GW:2 * GW]
        v = qkv[:, 2 * GW:3 * GW]
        q = q * lax.rsqrt(_xdot(q * q, bones) + EPS) * (DH ** -0.5)
        k = k * lax.rsqrt(_xdot(k * k, bones) + EPS)
        beta = _sigmoid(zb_ref[b])
        loga = -jnp.exp(alog_ref[...]) * jax.nn.softplus(za_ref[b] + dtb_ref[...])
        g = _xdot_left(tri, loga)
        for h in range(HEADS):
            hs = slice(h * DH, (h + 1) * DH)
            items.append(dict(b=b, h=h, hs=hs, q=q[:, hs], k=k[:, hs], v=v[:, hs], beta=beta[:, hs], g=g[:, hs]))
    for it in items:
        it['gcol'] = it['g'][:, 0:c]
    for it in items:
        it['grow'] = _xtranspose(it['gcol'], eye_c)
    for it in items:
        it['kk'] = _bdot_nt(it['k'], it['k'])
        it['qk'] = _bdot_nt(it['q'], it['k'])
    for it in items:
        it['dec'] = jnp.exp(jnp.where(causal, it['gcol'] - it['grow'], NEG_BIG))
        it['a'] = jnp.where(strict, it['beta'][:, 0:c] * it['kk'] * it['dec'], 0.0)
        it['r'] = jnp.concatenate([it['beta'] * it['v'], it['beta'] * jnp.exp(it['g']) * it['k']], axis=1)
    blk = min(SOLVE_BLOCK, c)
    for it in items:
        it['solved'] = []
    for j0 in range(0, c, blk):
        rbs = [it['r'][j0:j0 + blk, :] for it in items]
        abs_ = [it['a'][j0:j0 + blk, j0:j0 + blk] for it in items]
        for s in range(blk - 1):
            rbs = [rb - ab[:, s:s + 1] * rb[s:s + 1, :] for rb, ab in zip(rbs, abs_)]
        for it, rb in zip(items, rbs):
            it['solved'].append(rb)
        if j0 + blk < c:
            belows = [_dot2(it['a'][j0 + blk:, j0:j0 + blk], rb) for it, rb in zip(items, rbs)]
            for it, below in zip(items, belows):
                it['r'] = jnp.concatenate([it['r'][:j0 + blk, :], it['r'][j0 + blk:, :] - below], axis=0)
    for it in items:
        w = it['solved'][0] if len(it['solved']) == 1 else jnp.concatenate(it['solved'], axis=0)
        it['w1'] = w[:, 0:DH]
        it['w2'] = w[:, DH:2 * DH]
        it['s'] = s_scr[it['b'], it['h']]
    for it in items:
        it['u'] = it['w1'] - _bdot(it['w2'], it['s'])
    for it in items:
        it['o'] = _bdot(it['q'] * jnp.exp(it['g']), it['s']) + _bdot(it['qk'] * it['dec'], it['u'])
    for it in items:
        g_last = it['g'][c - 1:c, :]
        kd = it['k'] * jnp.exp(g_last - it['g'])
        kd_t = lax.dot_general(eye_h.astype(BF16), kd.astype(BF16), (((1,), (1,)), ((), ())),
                               preferred_element_type=F32)
        it['snew'] = jnp.exp(g_last) * it['s'] + _bdot(kd_t, it['u'])
    for it in items:
        s_scr[it['b'], it['h']] = it['snew']
        o = it['o']
        zg = zg_ref[it['b'], :, it['hs']]
        o = o * lax.rsqrt(jnp.mean(o * o, axis=-1, keepdims=True) + EPS) * nw_ref[...] * _silu(zg)
        o_ref[it['b'], :, it['hs']] = o

    @pl.when(ci == pl.num_programs(1) - 1)
    def _():
        sf_ref[...] = s_scr[...]


def _gdn(z3, cw, alog_x, dtb_x, nw, buf8, s0, nb, c):
    bsz, l, _ = z3.shape
    zblk = lambda j: pl.BlockSpec((nb, c, GW), lambda b, i, j=j: (b, i, j))
    full = lambda r, w: pl.BlockSpec((r, w), lambda b, i: (0, 0))
    return pl.pallas_call(
        functools.partial(_gdn_kernel, nb=nb, c=c),
        out_shape=(jax.ShapeDtypeStruct((bsz, l, GW), F32), jax.ShapeDtypeStruct((bsz, HEADS, DH, DH), F32)),
        grid=(bsz // nb, l // c),
        in_specs=[zblk(ZB_BQ), zblk(ZB_BK), zblk(ZB_BV), zblk(ZB_BG), zblk(ZB_BB), zblk(ZB_BA),
                  full(B_CONV, 3 * GW), full(1, GW), full(1, GW), full(1, DH),
                  pl.BlockSpec((nb, 8, 3 * GW), lambda b, i: (b, 0, 0)),
                  pl.BlockSpec((nb, HEADS, DH, DH), lambda b, i: (b, 0, 0, 0))],
        out_specs=(pl.BlockSpec((nb, c, GW), lambda b, i: (b, i, 0)),
                   pl.BlockSpec((nb, HEADS, DH, DH), lambda b, i: (b, 0, 0, 0))),
        scratch_shapes=[pltpu.VMEM((nb, HEADS, DH, DH), F32), pltpu.VMEM((nb, 8 + c, 3 * GW), F32)],
        compiler_params=_cparams(("parallel", "arbitrary")),
        name="gdn",
    )(z3, z3, z3, z3, z3, z3, cw, alog_x, dtb_x, nw, buf8, s0)


CPAD = 32
DPAD = 16


def _convpool_kernel(za_ref, zb_ref, zd_ref, dw_ref, db_ref, lg_ref, lb_ref, pw_ref, ps_ref, cbuf_ref, dbuf_ref,
                     oc_ref, u_ref, od_ref, xc_scr, xd_scr, *, nb, c, pos0):
    ci = pl.program_id(1)

    @pl.when(ci == 0)
    def _():
        xc_scr[:, 0:CPAD, :] = cbuf_ref[...]
        xd_scr[:, 0:DPAD, :] = dbuf_ref[...]

    lane = _iota((c, GW), 1)
    wl = jnp.where(lane < 64, 2.0, jnp.where(lane < 128, 4.0, jnp.where(lane < 192, 8.0, 16.0)))
    pos = (_iota((c, GW), 0) + (ci * c + pos0 + 1)).astype(F32)
    cnt = jnp.minimum(wl, pos)
    for b in range(nb):
        u = za_ref[b] * _sigmoid(zb_ref[b])
        u_ref[b] = u
        xc_scr[b, CPAD:CPAD + c, :] = u
        y = dw_ref[C_CONV - 1:C_CONV, :] * u
        for j in range(C_CONV - 1):
            y = y + dw_ref[j:j + 1, :] * xc_scr[b, 2 + j:2 + j + c, :]
        xc_scr[b, 0:CPAD, :] = xc_scr[b, c:c + CPAD, :]
        y = y + db_ref[...]
        mu = jnp.mean(y, axis=-1, keepdims=True)
        yc = y - mu
        var = jnp.mean(yc * yc, axis=-1, keepdims=True)
        oc_ref[b] = _silu(yc * lax.rsqrt(var + EPS) * lg_ref[...] + lb_ref[...])
        x = zd_ref[b]
        xd_scr[b, DPAD:DPAD + c, :] = x
        acc = x
        sums = {}
        for i in range(1, 16):
            acc = acc + xd_scr[b, DPAD - i:DPAD - i + c, :]
            if i + 1 in POOL_WINDOWS:
                sums[i + 1] = acc
        xd_scr[b, 0:DPAD, :] = xd_scr[b, c:c + DPAD, :]
        ssel = jnp.where(lane < 64, sums[2], jnp.where(lane < 128, sums[4], jnp.where(lane < 192, sums[8], sums[16])))
        diff = ssel / cnt - x
        od_ref[b] = _bdot(diff, pw_ref[...]) * ps_ref[...]


def _convpool(z3, dw, db, lg, lb, pwbd, ps, cbuf, dbuf, nb, c, pos0):
    bsz, l, _ = z3.shape
    zblk = lambda j: pl.BlockSpec((nb, c, GW), lambda b, i, j=j: (b, i, j))
    full = lambda r, w: pl.BlockSpec((r, w), lambda b, i: (0, 0))
    oblk = pl.BlockSpec((nb, c, GW), lambda b, i: (b, i, 0))
    osh = jax.ShapeDtypeStruct((bsz, l, GW), F32)
    return pl.pallas_call(
        functools.partial(_convpool_kernel, nb=nb, c=c, pos0=pos0),
        out_shape=(osh, osh, osh),
        grid=(bsz // nb, l // c),
        in_specs=[zblk(ZB_CA), zblk(ZB_CB), zblk(ZB_D), full(C_CONV, GW), full(1, GW), full(1, GW), full(1, GW),
                  full(GW, GW), full(1, GW),
                  pl.BlockSpec((nb, CPAD, GW), lambda b, i: (b, 0, 0)),
                  pl.BlockSpec((nb, DPAD, GW), lambda b, i: (b, 0, 0))],
        out_specs=(oblk, oblk, oblk),
        scratch_shapes=[pltpu.VMEM((nb, CPAD + c, GW), F32), pltpu.VMEM((nb, DPAD + c, GW), F32)],
        compiler_params=_cparams(("parallel", "arbitrary")),
        name="convpool",
    )(z3, z3, z3, dw, db, lg, lb, pwbd, ps, cbuf, dbuf)


NKH = 2 * PEER_HEADS * PEER_NKEYS
_CANDS = [(a, b) for a in range(PEER_TOPK) for b in range(PEER_TOPK) if (a + 1) * (b + 1) <= PEER_TOPK]


def _oddeven_merge_sort_pairs(n):
    pairs = []

    def merge(lo, hi, r):
        step = r * 2
        if step < hi - lo:
            merge(lo, hi, step)
            merge(lo + r, hi, step)
            pairs.extend((i, i + r) for i in range(lo + r, hi - r, step))
        else:
            pairs.append((lo, lo + r))

    def sort(lo, hi):
        if hi - lo >= 1:
            mid = lo + (hi - lo) // 2
            sort(lo, mid)
            sort(mid + 1, hi)
            merge(lo, hi, 1)

    sort(0, n - 1)
    return pairs


_SORT16 = _oddeven_merge_sort_pairs(PEER_NKEYS // 8)


def _peer1_kernel(h_ref, g_ref, wq_ref, kbd_ref, xn_ref, e1_ref, c_ref, e2_ref, r2_ref, s_scr, sv_scr, d_scr, zi_scr,
                  *, nlt):
    xn = _rms(h_ref[...], g_ref[...]).T.astype(BF16)
    xn_ref[...] = xn
    qt = jnp.dot(wq_ref[...], xn, preferred_element_type=F32).astype(BF16)
    for hp in range(2 * PEER_HEADS):
        sc = jnp.dot(kbd_ref[hp], qt[hp * PEER_HALF:(hp + 1) * PEER_HALF, :], preferred_element_type=F32)
        for lt in range(nlt):
            s_scr[lt, hp * PEER_NKEYS:(hp + 1) * PEER_NKEYS, :] = sc[:, lt * LANES:(lt + 1) * LANES]

    def lane_tile(lt, carry):
        for h in range(PEER_HEADS):
            for p in range(2):
                s = s_scr[lt, (2 * h + p) * PEER_NKEYS:(2 * h + p + 1) * PEER_NKEYS, :]
                lists = [s[8 * i:8 * (i + 1), :] for i in range(PEER_NKEYS // 8)]
                for i, j in _SORT16:
                    lists[i], lists[j] = jnp.maximum(lists[i], lists[j]), jnp.minimum(lists[i], lists[j])
                svs = []
                for a in range(PEER_TOPK):
                    m = jnp.max(lists[0], axis=0, keepdims=True)
                    sv_scr[p, a, h:h + 1, :] = m
                    svs.append(m)
                    popped = lists[0] == m
                    for i in range(PEER_TOPK - 1 - a):
                        lists[i] = jnp.where(popped, lists[i + 1], lists[i])
                if p == 1:
                    rank = jnp.full(s.shape, float(PEER_TOPK), F32)
                    for a in reversed(range(PEER_TOPK)):
                        rank = jnp.where(s >= svs[a], float(a), rank)
                    r2_ref[lt, h] = pltpu.bitcast(rank.astype(BF16), jnp.uint32)
                    e2_ref[lt, h] = pltpu.bitcast(jnp.exp(s - svs[0]).astype(BF16), jnp.uint32)
        sv1 = [sv_scr[0, a] for a in range(PEER_TOPK)]
        sv2 = [sv_scr[1, a] for a in range(PEER_TOPK)]
        vals = [sv1[a] + sv2[b] for a, b in _CANDS]
        n = len(_CANDS)
        before = [jnp.zeros((PEER_HEADS, LANES), F32) for _ in range(n)]
        for i in range(n):
            ai, bi = _CANDS[i]
            for j in range(i + 1, n):
                aj, bj = _CANDS[j]
                if ai <= aj and bi <= bj:
                    before[j] = before[j] + 1.0
                else:
                    t = jnp.where(vals[i] >= vals[j], 1.0, 0.0)
                    before[j] = before[j] + t
                    before[i] = before[i] + (1.0 - t)
        ex1 = [jnp.exp(sv1[a] - sv1[0]) for a in range(PEER_TOPK)]
        ex2 = [jnp.exp(sv2[b] - sv2[0]) for b in range(PEER_TOPK)]
        cnt = [jnp.zeros((PEER_HEADS, LANES), F32) for _ in range(PEER_TOPK)]
        zsum = jnp.zeros((PEER_HEADS, LANES), F32)
        for i, (a, b) in enumerate(_CANDS):
            sel = jnp.where(before[i] < float(PEER_TOPK), 1.0, 0.0)
            cnt[a] = cnt[a] + sel
            zsum = zsum + sel * ex1[a] * ex2[b]
        for a in range(PEER_TOPK):
            d_scr[a] = cnt[a]
        zi_scr[...] = 1.0 / zsum
        for h in range(PEER_HEADS):
            s = s_scr[lt, 2 * h * PEER_NKEYS:(2 * h + 1) * PEER_NKEYS, :]
            cc = jnp.zeros(s.shape, F32)
            for a in reversed(range(PEER_TOPK)):
                cc = jnp.where(s >= sv_scr[0, a, h:h + 1, :], d_scr[a, h:h + 1, :], cc)
            c_ref[lt, h] = cc
            e1_ref[lt, h] = jnp.exp(s - sv_scr[0, 0, h:h + 1, :]) * zi_scr[h:h + 1, :]
        return carry

    lax.fori_loop(0, nlt, lane_tile, 0)


def _peer1(h, g, wq_bf16, kbd_bf16, tt):
    t = h.shape[0]
    nlt = tt // LANES
    gsh = jax.ShapeDtypeStruct((t // LANES, PEER_HEADS, PEER_NKEYS, LANES), F32)
    gblk = pl.BlockSpec((nlt, PEER_HEADS, PEER_NKEYS, LANES), lambda i: (i, 0, 0, 0))
    psh = jax.ShapeDtypeStruct((t // LANES, PEER_HEADS, PEER_NKEYS // 2, LANES), jnp.uint32)
    pblk = pl.BlockSpec((nlt, PEER_HEADS, PEER_NKEYS // 2, LANES), lambda i: (i, 0, 0, 0))
    return pl.pallas_call(
        functools.partial(_peer1_kernel, nlt=nlt),
        out_shape=(jax.ShapeDtypeStruct((D_MODEL, t), BF16), gsh, gsh, psh, psh),
        grid=(t // tt,),
        in_specs=[pl.BlockSpec((tt, D_MODEL), lambda i: (i, 0)),
                  pl.BlockSpec((1, D_MODEL), lambda i: (0, 0)),
                  pl.BlockSpec((D_MODEL, D_MODEL), lambda i: (0, 0)),
                  pl.BlockSpec((2 * PEER_HEADS, PEER_NKEYS, PEER_HALF), lambda i: (0, 0, 0))],
        out_specs=(pl.BlockSpec((D_MODEL, tt), lambda i: (0, i)), gblk, gblk, pblk, pblk),
        scratch_shapes=[pltpu.VMEM((nlt, NKH, LANES), F32),
                        pltpu.VMEM((2, PEER_TOPK, PEER_HEADS, LANES), F32),
                        pltpu.VMEM((PEER_TOPK, PEER_HEADS, LANES), F32),
                        pltpu.VMEM((PEER_HEADS, LANES), F32)],
        compiler_params=_cparams(("parallel",)),
        name="peer1",
    )(h, g, wq_bf16, kbd_bf16)


NE1 = 8
ETILE = NE1 * PEER_NKEYS


def _peer2_step(u_ref, xn_ref, vt_ref, e1_ref, c_ref, e2_ref, r2_ref, acc_scr, act_w, act_r, w_w, w_r, nlt):
    tile = (PEER_NKEYS, LANES)
    zero = jnp.zeros(tile, BF16)
    grp = 2
    mblk = D_MODEL // 4

    def value_rows(m):
        rs = slice(m * mblk, (m + 1) * mblk)
        acc_scr[rs, :] += jnp.dot(vt_ref[rs, :], pltpu.bitcast(w_r[...], BF16), preferred_element_type=F32)

    def act_rows(m):
        a = jnp.dot(u_ref[m * mblk:(m + 1) * mblk, :], xn_ref[...], preferred_element_type=F32)
        a = 0.5 * a * (1.0 + lax.erf(a * 0.7071067811865476))
        act_w[m * mblk // 2:(m + 1) * mblk // 2, :] = pltpu.bitcast(a.astype(BF16), jnp.uint32)

    def gate_group(lt, g0):
        ls = slice(lt * LANES, (lt + 1) * LANES)
        gates = [zero] * grp
        for h in range(PEER_HEADS):
            rank = pltpu.bitcast(r2_ref[lt, h], BF16)
            wkey = pltpu.bitcast(e2_ref[lt, h], BF16)
            for j in range(grp):
                i1 = g0 + j
                cnt = jnp.broadcast_to(c_ref[lt, h, i1:i1 + 1, :], tile).astype(BF16)
                wgt = jnp.broadcast_to(e1_ref[lt, h, i1:i1 + 1, :], tile).astype(BF16)
                gates[j] = gates[j] + jnp.where(rank < cnt, wkey, zero) * wgt
        for j in range(grp):
            ps = slice((g0 + j) * PEER_NKEYS // 2, (g0 + j + 1) * PEER_NKEYS // 2)
            w_w[ps, ls] = pltpu.bitcast(gates[j] * pltpu.bitcast(act_r[ps, ls], BF16), jnp.uint32)

    groups = [(lt, g0) for lt in range(nlt) for g0 in range(0, NE1, grp)]
    mxu_work = [f for m in range(4) for f in (functools.partial(value_rows, m), functools.partial(act_rows, m))]
    per = max(1, len(groups) // len(mxu_work))
    gi = 0
    for k, mm in enumerate(mxu_work):
        mm()
        take = len(groups) - gi if k == len(mxu_work) - 1 else per
        for _ in range(take):
            if gi < len(groups):
                gate_group(*groups[gi])
                gi += 1


def _peer2_kernel(h_ref, xn_ref, u_ref, vt_ref, e1_ref, c_ref, e2_ref, r2_ref, o_ref, acc_scr, act0, act1, w0, w1, *,
                  nlt):
    e = pl.program_id(1)

    @pl.when(e == 0)
    def _():
        acc_scr[...] = jnp.zeros(acc_scr.shape, F32)
        for r in (act0, act1, w0, w1):
            r[...] = jnp.zeros(r.shape, jnp.uint32)

    args = (u_ref, xn_ref, vt_ref, e1_ref, c_ref, e2_ref, r2_ref, acc_scr)

    @pl.when(e % 2 == 0)
    def _():
        _peer2_step(*args, act0, act1, w1, w0, nlt)

    @pl.when(e % 2 == 1)
    def _():
        _peer2_step(*args, act1, act0, w0, w1, nlt)

    @pl.when(e == pl.num_programs(1) - 1)
    def _():
        o_ref[...] = h_ref[...] + acc_scr[...].T


def _peer2(h, xn, u_bf16, vt_bf16, e1, cc, e2, r2, tt):
    t = h.shape[0]
    nlt = tt // LANES
    ne = PEER_NKEYS // NE1
    last = ne - 1
    pfull = pl.BlockSpec((nlt, PEER_HEADS, PEER_NKEYS // 2, LANES), lambda i, e: (i, 0, 0, 0))
    gtile = pl.BlockSpec((nlt, PEER_HEADS, NE1, LANES), lambda i, e: (i, 0, jnp.clip(e - 1, 0, last), 0))
    slot = pltpu.VMEM((ETILE // 2, tt), jnp.uint32)
    return pl.pallas_call(
        functools.partial(_peer2_kernel, nlt=nlt),
        out_shape=jax.ShapeDtypeStruct((t, D_MODEL), F32),
        grid=(t // tt, ne + 2),
        in_specs=[pl.BlockSpec((tt, D_MODEL), lambda i, e: (i, 0)),
                  pl.BlockSpec((D_MODEL, tt), lambda i, e: (0, i)),
                  pl.BlockSpec((ETILE, D_MODEL), lambda i, e: (jnp.minimum(e, last), 0)),
                  pl.BlockSpec((D_MODEL, ETILE), lambda i, e: (0, jnp.clip(e - 2, 0, last))),
                  gtile, gtile, pfull, pfull],
        out_specs=pl.BlockSpec((tt, D_MODEL), lambda i, e: (i, 0)),
        scratch_shapes=[pltpu.VMEM((D_MODEL, tt), F32), slot, slot, slot, slot],
        compiler_params=_cparams(("parallel", "arbitrary")),
        name="peer2",
    )(h, xn, u_bf16, vt_bf16, e1, cc, e2, r2)


def _prep_layer(i, w_in, lbs, hgrn_norm, gdn_conv_w, gdn_a_log, gdn_dt_bias, gdn_norm, conf_dw_w, conf_dw_b,
                conf_ln_g, conf_ln_b, pool_w, pool_scale, w_out, peer_wq, peer_keys, peer_u, peer_v, ple_gate_w,
                ple_proj_w):
    wi = w_in[i]
    hk = HEADS * DH
    c0 = 4 * hk
    qkv = wi[:, c0:c0 + 3 * hk]
    zbg = wi[:, c0 + 3 * hk:c0 + 4 * hk]
    zbb = wi[:, c0 + 4 * hk:c0 + 4 * hk + HEADS]
    zba = wi[:, c0 + 4 * hk + HEADS:c0 + 4 * hk + 2 * HEADS]
    rest = wi[:, c0 + 4 * hk + 2 * HEADS:]
    w_perm = jnp.concatenate([wi[:, :c0], qkv, zbg, rest, jnp.repeat(zbb, DH, axis=1), jnp.repeat(zba, DH, axis=1)],
                             axis=1).astype(BF16)
    kbd = peer_keys[i].reshape(2 * PEER_HEADS, PEER_NKEYS, PEER_HALF).astype(BF16)
    pw = pool_w[i]
    eye4 = jnp.eye(4, dtype=F32)
    pwbd = (pw[:, :, None, :] * eye4[:, None, :, None]).reshape(GW, GW)
    return dict(
        w_in=w_perm, lb=lbs[i][None, :], hgrn_norm=hgrn_norm[i][None, :],
        gdn_conv_w=gdn_conv_w[i], gdn_a_log=jnp.repeat(gdn_a_log[i], DH)[None, :],
        gdn_dt_bias=jnp.repeat(gdn_dt_bias[i], DH)[None, :], gdn_norm=gdn_norm[i][None, :],
        conf_dw_w=conf_dw_w[i], conf_dw_b=conf_dw_b[i][None, :], conf_ln_g=conf_ln_g[i][None, :],
        conf_ln_b=conf_ln_b[i][None, :], pool_w=pwbd, pool_scale=pool_scale[i][None, :],
        w_out=w_out[i].astype(BF16), peer_wq=peer_wq[i].T.astype(BF16), kbd=kbd,
        peer_u=peer_u[i].astype(BF16), peer_vt=peer_v[i].T.astype(BF16),
        ple_gate_w=ple_gate_w[i].astype(BF16), ple_proj_w=ple_proj_w[i].astype(BF16))


def _tail(buf, x, n):
    l = x.shape[1]
    if l >= n:
        return x[:, l - n:]
    return jnp.concatenate([buf[:, l:], x], axis=1)


def _trunk(x, p, pos0, s_a, s_b, c_b, c_c, c_d, layers, norms, norm_final):
    bsz, l, _ = x.shape
    t = bsz * l
    prompt = l >= 64
    c = 64 if prompt else l
    nb = 1 if prompt else 8
    cc = min(256, l)
    tm = min(512, t)
    tt = min(512, t)
    h = x.reshape(t, D_MODEL)
    n_a, n_b, nb_b, nb_c, nb_d = [], [], [], [], []
    for i, lw in enumerate(layers):
        z = _inproj(h, norms['mix'][i][None, :], lw['w_in'], tm)
        z3 = z.reshape(bsz, l, ZW)
        eye4 = jnp.eye(HEADS, dtype=F32)
        s0t = (jnp.swapaxes(s_a[i], 2, 3)[:, :, :, None, :] * eye4[None, :, None, :, None]).reshape(bsz, GW, GW)
        o_a, st = _hgrn(z3, lw['lb'], lw['hgrn_norm'], s0t, min(4, bsz) if prompt else nb, min(c, 32))
        st = st.reshape(bsz, HEADS, DH, HEADS, DH)
        n_a.append(jnp.stack([jnp.swapaxes(st[:, j, :, j, :], 1, 2) for j in range(HEADS)], axis=1))
        buf8 = jnp.pad(c_b[i], ((0, 0), (8 - (B_CONV - 1), 0), (0, 0)))
        o_b, sb = _gdn(z3, lw['gdn_conv_w'], lw['gdn_a_log'], lw['gdn_dt_bias'], lw['gdn_norm'], buf8, s_b[i],
                       min(2, bsz) if prompt else nb, c)
        n_b.append(sb)
        nb_b.append(_tail(c_b[i], z3[:, :, ZB_BQ * GW:(ZB_BV + 1) * GW], B_CONV - 1))
        cbuf = jnp.pad(c_c[i], ((0, 0), (CPAD - (C_CONV - 1), 0), (0, 0)))
        dbuf = jnp.pad(c_d[i], ((0, 0), (DPAD - POOL_BUF, 0), (0, 0)))
        o_c, u_c, o_d = _convpool(z3, lw['conf_dw_w'], lw['conf_dw_b'], lw['conf_ln_g'], lw['conf_ln_b'],
                                  lw['pool_w'], lw['pool_scale'], cbuf, dbuf, nb, cc, pos0)
        nb_c.append(_tail(c_c[i], u_c, C_CONV - 1))
        nb_d.append(_tail(c_d[i], z3[:, :, ZB_D * GW:(ZB_D + 1) * GW], POOL_BUF))
        h = _outproj(h, o_a.reshape(t, GW), o_b.reshape(t, GW), o_c.reshape(t, GW), o_d.reshape(t, GW),
                     lw['w_out'], tm)
        xn, e1, cnt, e2, r2 = _peer1(h, norms['ffn'][i][None, :], lw['peer_wq'], lw['kbd'], tt)
        h = _peer2(h, xn, lw['peer_u'], lw['peer_vt'], e1, cnt, e2, r2, min(1024, t))
        h = _ple(h, p[i].reshape(t, D_PLE), norms['ple'][i][None, :], lw['ple_gate_w'], lw['ple_proj_w'],
                 norm_final[None, :], tm, final=(i == len(layers) - 1))
    st = lambda lst: jnp.stack(lst, axis=0)
    return h.reshape(bsz, l, D_MODEL), st(n_a), st(n_b), st(nb_b), st(nb_c), st(nb_d)


def kernel(x_prompt, x_sample, state_hgrn, state_gdn, state_gdn_conv, state_conf_conv, state_pool, p_prompt, p_sample, norm_mix, w_in, hgrn_lb_logits, hgrn_norm, gdn_conv_w, gdn_a_log, gdn_dt_bias, gdn_norm, conf_dw_w, conf_dw_b, conf_ln_g, conf_ln_b, pool_w, pool_scale, w_out, norm_ffn, peer_wq, peer_keys, peer_u, peer_v, norm_ple, ple_gate_w, ple_proj_w, norm_final):
    sm = jax.nn.softmax(hgrn_lb_logits.astype(F32), axis=0)
    lbs = jnp.cumsum(sm, axis=0) - sm[0:1]
    layers = [_prep_layer(i, w_in, lbs, hgrn_norm, gdn_conv_w, gdn_a_log, gdn_dt_bias, gdn_norm, conf_dw_w,
                          conf_dw_b, conf_ln_g, conf_ln_b, pool_w, pool_scale, w_out, peer_wq, peer_keys, peer_u,
                          peer_v, ple_gate_w, ple_proj_w) for i in range(DEPTH)]
    norms = dict(mix=norm_mix, ffn=norm_ffn, ple=norm_ple)
    bp = x_prompt.shape[0]
    z = lambda *s: jnp.zeros((DEPTH, bp) + s, F32)
    yp, hp, gp, gcp, ccp, pp = _trunk(x_prompt, p_prompt, 0, z(HEADS, DH, DH), z(HEADS, DH, DH),
                                      z(B_CONV - 1, 3 * GW), z(C_CONV - 1, GW), z(POOL_BUF, GW),
                                      layers, norms, norm_final)
    ys, hs, gs, gcs, ccs, ps = _trunk(x_sample, p_sample, PAST_LEN, state_hgrn, state_gdn, state_gdn_conv,
                                      state_conf_conv, state_pool, layers, norms, norm_final)
    return (yp, ys, hp, gp, gcp, ccp, pp, hs, gs, gcs, ccs, ps)
```

```python
import functools

import numpy as np
import jax
import jax.numpy as jnp
from jax import lax
from jax.experimental import pallas as pl
from jax.experimental.pallas import tpu as pltpu

F32 = jnp.float32
BF16 = jnp.bfloat16

D_MODEL = 1024
DEPTH = 4
PAST_LEN = 16384
EPS = 1e-6
NEG_BIG = -1e30
D_PLE = 256
HEADS = 4
DH = 64
GW = 256
B_CONV = 4
C_CONV = 31
POOL_WINDOWS = (2, 4, 8, 16)
POOL_BUF = 15
PEER_HEADS = 8
PEER_NKEYS = 128
PEER_HALF = 64
PEER_TOPK = 16

ZB_AQ, ZB_AF, ZB_AI, ZB_AG, ZB_BQ, ZB_BK, ZB_BV, ZB_BG, ZB_CA, ZB_CB, ZB_D, ZB_BB, ZB_BA = range(13)
ZW = 13 * GW

LANES = 128
SOLVE_BLOCK = 16
VMEM_LIMIT = 56 * 1024 * 1024


def _cparams(sem, flags=None):
    return pltpu.CompilerParams(dimension_semantics=sem, vmem_limit_bytes=VMEM_LIMIT, flags=flags)


def _bdot(a, b):
    return jnp.dot(a.astype(BF16), b.astype(BF16), preferred_element_type=F32)


def _bdot_nt(a, b):
    return lax.dot_general(a.astype(BF16), b.astype(BF16), (((1,), (1,)), ((), ())),
                           preferred_element_type=F32)


def _split3(a):
    a1 = a.astype(BF16)
    r1 = a - a1.astype(F32)
    a2 = r1.astype(BF16)
    a3 = (r1 - a2.astype(F32)).astype(BF16)
    return a1, a2, a3


def _xdot(a, m01):
    a1, a2, a3 = _split3(a)
    m = m01.astype(BF16)
    return (jnp.dot(a1, m, preferred_element_type=F32) + jnp.dot(a2, m, preferred_element_type=F32)
            + jnp.dot(a3, m, preferred_element_type=F32))


def _xdot_left(m01, a):
    a1, a2, a3 = _split3(a)
    m = m01.astype(BF16)
    return (jnp.dot(m, a1, preferred_element_type=F32) + jnp.dot(m, a2, preferred_element_type=F32)
            + jnp.dot(m, a3, preferred_element_type=F32))


def _dot2(a, b):
    ah = a.astype(BF16)
    al = (a - ah.astype(F32)).astype(BF16)
    bh = b.astype(BF16)
    bl = (b - bh.astype(F32)).astype(BF16)
    return (jnp.dot(ah, bh, preferred_element_type=F32) + jnp.dot(ah, bl, preferred_element_type=F32)
            + jnp.dot(al, bh, preferred_element_type=F32))


def _xtranspose(a, eye):
    a1, a2, a3 = _split3(a)
    e = eye.astype(BF16)
    dn = (((1,), (1,)), ((), ()))
    return (lax.dot_general(e, a1, dn, preferred_element_type=F32)
            + lax.dot_general(e, a2, dn, preferred_element_type=F32)
            + lax.dot_general(e, a3, dn, preferred_element_type=F32))


def _pack_rows(x):
    r2, c = x.shape
    return lax.bitcast_convert_type(jnp.swapaxes(x.astype(BF16).reshape(r2 // 2, 2, c), -1, -2), jnp.uint32)


def _rows_bf16(words):
    return pltpu.bitcast(words, BF16)


def _iota(shape, axis):
    return lax.broadcasted_iota(jnp.int32, shape, axis)


def _tri_incl(c):
    return jnp.where(_iota((c, c), 1) <= _iota((c, c), 0), 1.0, 0.0).astype(F32)


def _eye(n):
    return jnp.where(_iota((n, n), 0) == _iota((n, n), 1), 1.0, 0.0).astype(F32)


def _block_ones(n, blk):
    return jnp.where(_iota((n, n), 0) // blk == _iota((n, n), 1) // blk, 1.0, 0.0).astype(F32)


def _rms(x, g):
    return x * lax.rsqrt(jnp.mean(x * x, axis=-1, keepdims=True) + EPS) * g


def _sigmoid(x):
    return jax.nn.sigmoid(x)


def _silu(x):
    return x * jax.nn.sigmoid(x)


def _inproj_kernel(x_ref, g_ref, w_ref, o_ref):
    xn = _rms(x_ref[...], g_ref[...])
    o_ref[...] = jnp.dot(xn.astype(BF16), _rows_bf16(w_ref[...]), preferred_element_type=F32)


def _inproj(h, g, w_bf16, tm):
    t = h.shape[0]
    return pl.pallas_call(
        _inproj_kernel,
        out_shape=jax.ShapeDtypeStruct((t, ZW), F32),
        grid=(t // tm,),
        in_specs=[pl.BlockSpec((tm, D_MODEL), lambda i: (i, 0)),
                  pl.BlockSpec((1, D_MODEL), lambda i: (0, 0)),
                  pl.BlockSpec((D_MODEL // 2, ZW), lambda i: (0, 0))],
        out_specs=pl.BlockSpec((tm, ZW), lambda i: (i, 0)),
        compiler_params=_cparams(("parallel",)),
        name="inproj",
    )(h, g, w_bf16)


def _outproj_kernel(h_ref, a_ref, b_ref, c_ref, d_ref, w_ref, o_ref):
    acc = h_ref[...]
    for i, r in enumerate((a_ref, b_ref, c_ref, d_ref)):
        acc = acc + jnp.dot(r[...].astype(BF16), _rows_bf16(w_ref[i * GW // 2:(i + 1) * GW // 2, :]),
                            preferred_element_type=F32)
    o_ref[...] = acc


def _outproj(h, oa, ob, oc, od, w_bf16, tm):
    t = h.shape[0]
    tok = lambda w: pl.BlockSpec((tm, w), lambda i: (i, 0))
    return pl.pallas_call(
        _outproj_kernel,
        out_shape=jax.ShapeDtypeStruct((t, D_MODEL), F32),
        grid=(t // tm,),
        in_specs=[tok(D_MODEL), tok(GW), tok(GW), tok(GW), tok(GW),
                  pl.BlockSpec((D_MODEL // 2, D_MODEL), lambda i: (0, 0))],
        out_specs=tok(D_MODEL),
        compiler_params=_cparams(("parallel",)),
        name="outproj",
    )(h, oa, ob, oc, od, w_bf16)


def _ple_kernel(h_ref, p_ref, g_ref, wg_ref, wp_ref, gf_ref, o_ref, *, final):
    h = h_ref[...]
    xn = _rms(h, g_ref[...])
    gate = _sigmoid(jnp.dot(xn.astype(BF16), _rows_bf16(wg_ref[...]), preferred_element_type=F32))
    proj = jnp.dot(p_ref[...].astype(BF16), _rows_bf16(wp_ref[...]), preferred_element_type=F32)
    out = h + gate * proj
    if final:
        out = _rms(out, gf_ref[...])
    o_ref[...] = out


def _ple(h, p, g, wg_bf16, wp_bf16, gfinal, tm, final):
    t = h.shape[0]
    tok = lambda w: pl.BlockSpec((tm, w), lambda i: (i, 0))
    full = lambda r, c: pl.BlockSpec((r, c), lambda i: (0, 0))
    return pl.pallas_call(
        functools.partial(_ple_kernel, final=final),
        out_shape=jax.ShapeDtypeStruct((t, D_MODEL), F32),
        grid=(t // tm,),
        in_specs=[tok(D_MODEL), tok(D_PLE), full(1, D_MODEL), full(D_MODEL // 2, D_MODEL), full(D_PLE // 2, D_MODEL),
                  full(1, D_MODEL)],
        out_specs=tok(D_MODEL),
        compiler_params=_cparams(("parallel",)),
        name="ple_final" if final else "ple",
    )(h, p, g, wg_bf16, wp_bf16, gfinal)


def _hgrn_kernel(z_ref, lb_ref, nw_ref, s0_ref, o_ref, sf_ref, st_scr, la_scr, q_scr, k_scr, v_scr, p_scr, *, nb, c):
    ci = pl.program_id(1)

    @pl.when(ci == 0)
    def _():
        st_scr[...] = s0_ref[...]

    lb = lb_ref[...]
    bones = _block_ones(GW, DH)
    tri = _tri_incl(c)
    rows = _iota((c, GW), 0)
    seqs = range(nb)
    for b in seqs:
        zf = z_ref[b, :, 1 * GW:2 * GW]
        q_scr[b] = _silu(z_ref[b, :, 0 * GW:1 * GW])
        k_scr[b] = (1.0 - lb) * _sigmoid(-zf)
        v_scr[b] = z_ref[b, :, 2 * GW:3 * GW]
        f = lb + (1.0 - lb) * _sigmoid(zf)
        la_scr[b] = _xdot_left(tri, jnp.log(f))

    def build(s, carry):
        for b in seqs:
            la_s = la_scr[b, pl.ds(s, 1), :]
            k_s = k_scr[b, pl.ds(s, 1), :]
            dec = jnp.exp(jnp.minimum(la_scr[b] - la_s, 0.0))
            p_scr[b, pl.ds(pl.multiple_of(s * c, c), c), :] = jnp.where(rows >= s, q_scr[b] * k_s * dec, 0.0)
        return carry

    lax.fori_loop(0, c, build, 0, unroll=min(4, c))
    for b in seqs:
        p_scr[b] = jnp.dot(p_scr[b].astype(BF16), bones.astype(BF16), preferred_element_type=F32)

    def consume(s, accs):
        return tuple(acc + p_scr[b, pl.ds(pl.multiple_of(s * c, c), c), :] * v_scr[b, pl.ds(s, 1), :]
                     for b, acc in zip(seqs, accs))

    os_ = lax.fori_loop(0, c, consume, tuple(jnp.zeros((c, GW), F32) for _ in seqs), unroll=min(4, c))
    sts = [st_scr[b] for b in seqs]
    las = [la_scr[b] for b in seqs]
    os_ = [o + _bdot_nt(q_scr[b] * jnp.exp(la), st) for b, o, la, st in zip(seqs, os_, las, sts)]
    upds = []
    for b, la in zip(seqs, las):
        k_dec = k_scr[b] * jnp.exp(la[c - 1:c, :] - la)
        upds.append(lax.dot_general(v_scr[b].astype(BF16), k_dec.astype(BF16), (((0,), (0,)), ((), ())),
                                    preferred_element_type=F32))
    for b, la, st, upd in zip(seqs, las, sts, upds):
        st_scr[b] = st * jnp.exp(la[c - 1:c, :]) + upd * bones
    mss = [_xdot(o * o, bones) * (1.0 / DH) for o in os_]
    for b, o, ms in zip(seqs, os_, mss):
        o_ref[b] = o * lax.rsqrt(ms + EPS) * nw_ref[...] * _sigmoid(z_ref[b, :, 3 * GW:4 * GW])

    @pl.when(ci == pl.num_programs(1) - 1)
    def _():
        sf_ref[...] = st_scr[...]


def _hgrn(z3, lb, nw, s0t, nb, c):
    bsz, l, _ = z3.shape
    return pl.pallas_call(
        functools.partial(_hgrn_kernel, nb=nb, c=c),
        out_shape=(jax.ShapeDtypeStruct((bsz, l, GW), F32), jax.ShapeDtypeStruct((bsz, GW, GW), F32)),
        grid=(bsz // nb, l // c),
        in_specs=[pl.BlockSpec((nb, c, 4 * GW), lambda b, i: (b, i, 0)),
                  pl.BlockSpec((1, GW), lambda b, i: (0, 0)),
                  pl.BlockSpec((1, GW), lambda b, i: (0, 0)),
                  pl.BlockSpec((nb, GW, GW), lambda b, i: (b, 0, 0))],
        out_specs=(pl.BlockSpec((nb, c, GW), lambda b, i: (b, i, 0)),
                   pl.BlockSpec((nb, GW, GW), lambda b, i: (b, 0, 0))),
        scratch_shapes=[pltpu.VMEM((nb, GW, GW), F32)] + [pltpu.VMEM((nb, c, GW), F32)] * 4
                       + [pltpu.VMEM((nb, c * c, GW), F32)],
        compiler_params=_cparams(("parallel", "arbitrary")),
        name="hgrn",
    )(z3, lb, nw, s0t)


def _gdn_kernel(zq_ref, zk_ref, zv_ref, zg_ref, zb_ref, za_ref, cw_ref, alog_ref, dtb_ref, nw_ref, buf_ref, s0_ref,
                o_ref, sf_ref, s_scr, xp_scr, *, nb, c):
    ci = pl.program_id(1)

    @pl.when(ci == 0)
    def _():
        s_scr[...] = s0_ref[...]
        xp_scr[:, 0:8, :] = buf_ref[...]

    bones = _block_ones(GW, DH)
    tri = _tri_incl(c)
    eye_c = _eye(c)
    eye_h = _eye(DH)
    tt = _iota((c, c), 0)
    ss = _iota((c, c), 1)
    causal = tt >= ss
    strict = tt > ss
    items = []
    for b in range(nb):
        for j, r in enumerate((zq_ref, zk_ref, zv_ref)):
            xp_scr[b, 8:8 + c, j * GW:(j + 1) * GW] = r[b]
        conv = cw_ref[3:4, :] * xp_scr[b, 8:8 + c, :]
        for j in range(B_CONV - 1):
            conv = conv + cw_ref[j:j + 1, :] * xp_scr[b, 5 + j:5 + j + c, :]
        xp_scr[b, 0:8, :] = xp_scr[b, c:c + 8, :]
        qkv = _silu(conv)
        q = qkv[:, 0:GW]
        k = qkv[:, GW:2 * GW]
        v = qkv[:, 2 * GW:3 * GW]
        q = q * lax.rsqrt(_xdot(q * q, bones) + EPS) * (DH ** -0.5)
        k = k * lax.rsqrt(_xdot(k * k, bones) + EPS)
        beta = _sigmoid(zb_ref[b])
        loga = -jnp.exp(alog_ref[...]) * jax.nn.softplus(za_ref[b] + dtb_ref[...])
        g = _xdot_left(tri, loga)
        for h in range(HEADS):
            hs = slice(h * DH, (h + 1) * DH)
            items.append(dict(b=b, h=h, hs=hs, q=q[:, hs], k=k[:, hs], v=v[:, hs], beta=beta[:, hs], g=g[:, hs]))
    for it in items:
        it['gcol'] = it['g'][:, 0:c]
    for it in items:
        it['grow'] = _xtranspose(it['gcol'], eye_c)
    for it in items:
        it['kk'] = _bdot_nt(it['k'], it['k'])
        it['qk'] = _bdot_nt(it['q'], it['k'])
    for it in items:
        it['dec'] = jnp.exp(jnp.where(causal, it['gcol'] - it['grow'], NEG_BIG))
        it['a'] = jnp.where(strict, it['beta'][:, 0:c] * it['kk'] * it['dec'], 0.0)
        it['r'] = jnp.concatenate([it['beta'] * it['v'], it['beta'] * jnp.exp(it['g']) * it['k']], axis=1)
    blk = min(SOLVE_BLOCK, c)
    for it in items:
        it['solved'] = []
    for j0 in range(0, c, blk):
        rbs = [it['r'][j0:j0 + blk, :] for it in items]
        abs_ = [it['a'][j0:j0 + blk, j0:j0 + blk] for it in items]
        for s in range(blk - 1):
            rbs = [rb - ab[:, s:s + 1] * rb[s:s + 1, :] for rb, ab in zip(rbs, abs_)]
        for it, rb in zip(items, rbs):
            it['solved'].append(rb)
        if j0 + blk < c:
            belows = [_dot2(it['a'][j0 + blk:, j0:j0 + blk], rb) for it, rb in zip(items, rbs)]
            for it, below in zip(items, belows):
                it['r'] = jnp.concatenate([it['r'][:j0 + blk, :], it['r'][j0 + blk:, :] - below], axis=0)
    for it in items:
        w = it['solved'][0] if len(it['solved']) == 1 else jnp.concatenate(it['solved'], axis=0)
        it['w1'] = w[:, 0:DH]
        it['w2'] = w[:, DH:2 * DH]
        it['s'] = s_scr[it['b'], it['h']]
    for it in items:
        it['u'] = it['w1'] - _bdot(it['w2'], it['s'])
    for it in items:
        it['o'] = _bdot(it['q'] * jnp.exp(it['g']), it['s']) + _bdot(it['qk'] * it['dec'], it['u'])
    for it in items:
        g_last = it['g'][c - 1:c, :]
        kd = it['k'] * jnp.exp(g_last - it['g'])
        kd_t = lax.dot_general(eye_h.astype(BF16), kd.astype(BF16), (((1,), (1,)), ((), ())),
                               preferred_element_type=F32)
        it['snew'] = jnp.exp(g_last) * it['s'] + _bdot(kd_t, it['u'])
    for it in items:
        s_scr[it['b'], it['h']] = it['snew']
        o = it['o']
        zg = zg_ref[it['b'], :, it['hs']]
        o = o * lax.rsqrt(jnp.mean(o * o, axis=-1, keepdims=True) + EPS) * nw_ref[...] * _silu(zg)
        o_ref[it['b'], :, it['hs']] = o

    @pl.when(ci == pl.num_programs(1) - 1)
    def _():
        sf_ref[...] = s_scr[...]


def _gdn(z3, cw, alog_x, dtb_x, nw, buf8, s0, nb, c):
    bsz, l, _ = z3.shape
    zblk = lambda j: pl.BlockSpec((nb, c, GW), lambda b, i, j=j: (b, i, j))
    full = lambda r, w: pl.BlockSpec((r, w), lambda b, i: (0, 0))
    return pl.pallas_call(
        functools.partial(_gdn_kernel, nb=nb, c=c),
        out_shape=(jax.ShapeDtypeStruct((bsz, l, GW), F32), jax.ShapeDtypeStruct((bsz, HEADS, DH, DH), F32)),
        grid=(bsz // nb, l // c),
        in_specs=[zblk(ZB_BQ), zblk(ZB_BK), zblk(ZB_BV), zblk(ZB_BG), zblk(ZB_BB), zblk(ZB_BA),
                  full(B_CONV, 3 * GW), full(1, GW), full(1, GW), full(1, DH),
                  pl.BlockSpec((nb, 8, 3 * GW), lambda b, i: (b, 0, 0)),
                  pl.BlockSpec((nb, HEADS, DH, DH), lambda b, i: (b, 0, 0, 0))],
        out_specs=(pl.BlockSpec((nb, c, GW), lambda b, i: (b, i, 0)),
                   pl.BlockSpec((nb, HEADS, DH, DH), lambda b, i: (b, 0, 0, 0))),
        scratch_shapes=[pltpu.VMEM((nb, HEADS, DH, DH), F32), pltpu.VMEM((nb, 8 + c, 3 * GW), F32)],
        compiler_params=_cparams(("parallel", "arbitrary")),
        name="gdn",
    )(z3, z3, z3, z3, z3, z3, cw, alog_x, dtb_x, nw, buf8, s0)


CPAD = 32
DPAD = 16


def _convpool_kernel(za_ref, zb_ref, zd_ref, dw_ref, db_ref, lg_ref, lb_ref, pw_ref, ps_ref, cbuf_ref, dbuf_ref,
                     oc_ref, u_ref, od_ref, xc_scr, xd_scr, *, nb, c, pos0):
    ci = pl.program_id(1)

    @pl.when(ci == 0)
    def _():
        xc_scr[:, 0:CPAD, :] = cbuf_ref[...]
        xd_scr[:, 0:DPAD, :] = dbuf_ref[...]

    lane = _iota((c, GW), 1)
    wl = jnp.where(lane < 64, 2.0, jnp.where(lane < 128, 4.0, jnp.where(lane < 192, 8.0, 16.0)))
    pos = (_iota((c, GW), 0) + (ci * c + pos0 + 1)).astype(F32)
    cnt = jnp.minimum(wl, pos)
    for b in range(nb):
        u = za_ref[b] * _sigmoid(zb_ref[b])
        u_ref[b] = u
        xc_scr[b, CPAD:CPAD + c, :] = u
        y = dw_ref[C_CONV - 1:C_CONV, :] * u
        for j in range(C_CONV - 1):
            y = y + dw_ref[j:j + 1, :] * xc_scr[b, 2 + j:2 + j + c, :]
        xc_scr[b, 0:CPAD, :] = xc_scr[b, c:c + CPAD, :]
        y = y + db_ref[...]
        mu = jnp.mean(y, axis=-1, keepdims=True)
        yc = y - mu
        var = jnp.mean(yc * yc, axis=-1, keepdims=True)
        oc_ref[b] = _silu(yc * lax.rsqrt(var + EPS) * lg_ref[...] + lb_ref[...])
        x = zd_ref[b]
        xd_scr[b, DPAD:DPAD + c, :] = x
        acc = x
        sums = {}
        for i in range(1, 16):
            acc = acc + xd_scr[b, DPAD - i:DPAD - i + c, :]
            if i + 1 in POOL_WINDOWS:
                sums[i + 1] = acc
        xd_scr[b, 0:DPAD, :] = xd_scr[b, c:c + DPAD, :]
        ssel = jnp.where(lane < 64, sums[2], jnp.where(lane < 128, sums[4], jnp.where(lane < 192, sums[8], sums[16])))
        diff = ssel / cnt - x
        od_ref[b] = _bdot(diff, pw_ref[...]) * ps_ref[...]


def _convpool(z3, dw, db, lg, lb, pwbd, ps, cbuf, dbuf, nb, c, pos0):
    bsz, l, _ = z3.shape
    zblk = lambda j: pl.BlockSpec((nb, c, GW), lambda b, i, j=j: (b, i, j))
    full = lambda r, w: pl.BlockSpec((r, w), lambda b, i: (0, 0))
    oblk = pl.BlockSpec((nb, c, GW), lambda b, i: (b, i, 0))
    osh = jax.ShapeDtypeStruct((bsz, l, GW), F32)
    return pl.pallas_call(
        functools.partial(_convpool_kernel, nb=nb, c=c, pos0=pos0),
        out_shape=(osh, osh, osh),
        grid=(bsz // nb, l // c),
        in_specs=[zblk(ZB_CA), zblk(ZB_CB), zblk(ZB_D), full(C_CONV, GW), full(1, GW), full(1, GW), full(1, GW),
                  full(GW, GW), full(1, GW),
                  pl.BlockSpec((nb, CPAD, GW), lambda b, i: (b, 0, 0)),
                  pl.BlockSpec((nb, DPAD, GW), lambda b, i: (b, 0, 0))],
        out_specs=(oblk, oblk, oblk),
        scratch_shapes=[pltpu.VMEM((nb, CPAD + c, GW), F32), pltpu.VMEM((nb, DPAD + c, GW), F32)],
        compiler_params=_cparams(("parallel", "arbitrary")),
        name="convpool",
    )(z3, z3, z3, dw, db, lg, lb, pwbd, ps, cbuf, dbuf)


NKH = 2 * PEER_HEADS * PEER_NKEYS
_CANDS = [(a, b) for a in range(PEER_TOPK) for b in range(PEER_TOPK) if (a + 1) * (b + 1) <= PEER_TOPK]


def _oddeven_merge_sort_pairs(n):
    pairs = []

    def merge(lo, hi, r):
        step = r * 2
        if step < hi - lo:
            merge(lo, hi, step)
            merge(lo + r, hi, step)
            pairs.extend((i, i + r) for i in range(lo + r, hi - r, step))
        else:
            pairs.append((lo, lo + r))

    def sort(lo, hi):
        if hi - lo >= 1:
            mid = lo + (hi - lo) // 2
            sort(lo, mid)
            sort(mid + 1, hi)
            merge(lo, hi, 1)

    sort(0, n - 1)
    return pairs


_SORT16 = _oddeven_merge_sort_pairs(PEER_NKEYS // 8)


def _peer1_kernel(h_ref, g_ref, wq_ref, kbd_ref, xn_ref, e1_ref, c_ref, e2_ref, r2_ref, s_scr, sv_scr, d_scr, zi_scr,
                  *, nlt):
    xn = _rms(h_ref[...], g_ref[...]).T.astype(BF16)
    xn_ref[...] = pltpu.bitcast(xn, jnp.uint32)
    qt = jnp.dot(_rows_bf16(wq_ref[...]), xn, preferred_element_type=F32).astype(BF16)
    for hp in range(2 * PEER_HEADS):
        sc = jnp.dot(kbd_ref[hp], qt[hp * PEER_HALF:(hp + 1) * PEER_HALF, :], preferred_element_type=F32)
        for lt in range(nlt):
            s_scr[lt, hp * PEER_NKEYS:(hp + 1) * PEER_NKEYS, :] = sc[:, lt * LANES:(lt + 1) * LANES]

    def lane_tile(lt, carry):
        for h in range(PEER_HEADS):
            for p in range(2):
                s = s_scr[lt, (2 * h + p) * PEER_NKEYS:(2 * h + p + 1) * PEER_NKEYS, :]
                lists = [s[8 * i:8 * (i + 1), :] for i in range(PEER_NKEYS // 8)]
                for i, j in _SORT16:
                    lists[i], lists[j] = jnp.maximum(lists[i], lists[j]), jnp.minimum(lists[i], lists[j])
                svs = []
                for a in range(PEER_TOPK):
                    m = jnp.max(lists[0], axis=0, keepdims=True)
                    sv_scr[p, a, h:h + 1, :] = m
                    svs.append(m)
                    popped = lists[0] == m
                    for i in range(PEER_TOPK - 1 - a):
                        lists[i] = jnp.where(popped, lists[i + 1], lists[i])
                if p == 1:
                    rank = jnp.full(s.shape, float(PEER_TOPK), F32)
                    for a in reversed(range(PEER_TOPK)):
                        rank = jnp.where(s >= svs[a], float(a), rank)
                    r2_ref[lt, h] = pltpu.bitcast(rank.astype(BF16), jnp.uint32)
                    e2_ref[lt, h] = pltpu.bitcast(jnp.exp(s - svs[0]).astype(BF16), jnp.uint32)
        sv1 = [sv_scr[0, a] for a in range(PEER_TOPK)]
        sv2 = [sv_scr[1, a] for a in range(PEER_TOPK)]
        vals = [sv1[a] + sv2[b] for a, b in _CANDS]
        n = len(_CANDS)
        before = [jnp.zeros((PEER_HEADS, LANES), F32) for _ in range(n)]
        for i in range(n):
            ai, bi = _CANDS[i]
            for j in range(i + 1, n):
                aj, bj = _CANDS[j]
                if ai <= aj and bi <= bj:
                    before[j] = before[j] + 1.0
                else:
                    t = jnp.where(vals[i] >= vals[j], 1.0, 0.0)
                    before[j] = before[j] + t
                    before[i] = before[i] + (1.0 - t)
        ex1 = [jnp.exp(sv1[a] - sv1[0]) for a in range(PEER_TOPK)]
        ex2 = [jnp.exp(sv2[b] - sv2[0]) for b in range(PEER_TOPK)]
        cnt = [jnp.zeros((PEER_HEADS, LANES), F32) for _ in range(PEER_TOPK)]
        zsum = jnp.zeros((PEER_HEADS, LANES), F32)
        for i, (a, b) in enumerate(_CANDS):
            sel = jnp.where(before[i] < float(PEER_TOPK), 1.0, 0.0)
            cnt[a] = cnt[a] + sel
            zsum = zsum + sel * ex1[a] * ex2[b]
        for a in range(PEER_TOPK):
            d_scr[a] = cnt[a]
        zi_scr[...] = 1.0 / zsum
        for h in range(PEER_HEADS):
            s = s_scr[lt, 2 * h * PEER_NKEYS:(2 * h + 1) * PEER_NKEYS, :]
            cc = jnp.zeros(s.shape, F32)
            for a in reversed(range(PEER_TOPK)):
                cc = jnp.where(s >= sv_scr[0, a, h:h + 1, :], d_scr[a, h:h + 1, :], cc)
            c_ref[lt, h] = cc
            e1_ref[lt, h] = jnp.exp(s - sv_scr[0, 0, h:h + 1, :]) * zi_scr[h:h + 1, :]
        return carry

    lax.fori_loop(0, nlt, lane_tile, 0)


def _peer1(h, g, wq_bf16, kbd_bf16, tt):
    t = h.shape[0]
    nlt = tt // LANES
    gsh = jax.ShapeDtypeStruct((t // LANES, PEER_HEADS, PEER_NKEYS, LANES), F32)
    gblk = pl.BlockSpec((nlt, PEER_HEADS, PEER_NKEYS, LANES), lambda i: (i, 0, 0, 0))
    psh = jax.ShapeDtypeStruct((t // LANES, PEER_HEADS, PEER_NKEYS // 2, LANES), jnp.uint32)
    pblk = pl.BlockSpec((nlt, PEER_HEADS, PEER_NKEYS // 2, LANES), lambda i: (i, 0, 0, 0))
    return pl.pallas_call(
        functools.partial(_peer1_kernel, nlt=nlt),
        out_shape=(jax.ShapeDtypeStruct((D_MODEL // 2, t), jnp.uint32), gsh, gsh, psh, psh),
        grid=(t // tt,),
        in_specs=[pl.BlockSpec((tt, D_MODEL), lambda i: (i, 0)),
                  pl.BlockSpec((1, D_MODEL), lambda i: (0, 0)),
                  pl.BlockSpec((D_MODEL // 2, D_MODEL), lambda i: (0, 0)),
                  pl.BlockSpec((2 * PEER_HEADS, PEER_NKEYS, PEER_HALF), lambda i: (0, 0, 0))],
        out_specs=(pl.BlockSpec((D_MODEL // 2, tt), lambda i: (0, i)), gblk, gblk, pblk, pblk),
        scratch_shapes=[pltpu.VMEM((nlt, NKH, LANES), F32),
                        pltpu.VMEM((2, PEER_TOPK, PEER_HEADS, LANES), F32),
                        pltpu.VMEM((PEER_TOPK, PEER_HEADS, LANES), F32),
                        pltpu.VMEM((PEER_HEADS, LANES), F32)],
        compiler_params=_cparams(("parallel",)),
        name="peer1",
    )(h, g, wq_bf16, kbd_bf16)


NE1 = 8
ETILE = NE1 * PEER_NKEYS


def _peer2_step(u_ref, xn_ref, vt_ref, e1_ref, c_ref, e2_ref, r2_ref, acc_scr, act_w, act_r, w_w, w_r, nlt,
                do_value=True, do_gate=True, do_act=True):
    tile = (PEER_NKEYS, LANES)
    zero = jnp.zeros(tile, BF16)
    grp = 2
    mblk = D_MODEL // 4

    def value_rows(m):
        rs = slice(m * mblk, (m + 1) * mblk)
        vrows = _rows_bf16(vt_ref[m * mblk // 2:(m + 1) * mblk // 2, :])
        acc_scr[rs, :] += jnp.dot(vrows, _rows_bf16(w_r[...]), preferred_element_type=F32)

    def act_rows(m):
        urows = _rows_bf16(u_ref[m * mblk // 2:(m + 1) * mblk // 2, :])
        a = jnp.dot(urows, _rows_bf16(xn_ref[...]), preferred_element_type=F32).astype(BF16)
        a = 0.5 * a * (1.0 + lax.erf(a * 0.7071067811865476))
        act_w[m * mblk // 2:(m + 1) * mblk // 2, :] = pltpu.bitcast(a, jnp.uint32)

    def gate_group(lt, g0):
        ls = slice(lt * LANES, (lt + 1) * LANES)
        gates = [zero] * grp
        for h in range(PEER_HEADS):
            rank = pltpu.bitcast(r2_ref[lt, h], BF16)
            wkey = pltpu.bitcast(e2_ref[lt, h], BF16)
            for j in range(grp):
                i1 = g0 + j
                cnt = jnp.broadcast_to(c_ref[lt, h, i1:i1 + 1, :], tile).astype(BF16)
                wgt = jnp.broadcast_to(e1_ref[lt, h, i1:i1 + 1, :], tile).astype(BF16)
                gates[j] = gates[j] + jnp.where(rank < cnt, wkey, zero) * wgt
        for j in range(grp):
            ps = slice((g0 + j) * PEER_NKEYS // 2, (g0 + j + 1) * PEER_NKEYS // 2)
            w_w[ps, ls] = pltpu.bitcast(gates[j] * pltpu.bitcast(act_r[ps, ls], BF16), jnp.uint32)

    groups = [(lt, g0) for lt in range(nlt) for g0 in range(0, NE1, grp)] if do_gate else []
    mxu_work = [f for m in range(4) for f, on in ((functools.partial(value_rows, m), do_value),
                                                  (functools.partial(act_rows, m), do_act)) if on]
    if not mxu_work:
        mxu_work = [lambda: None]
    per = max(1, len(groups) // len(mxu_work))
    gi = 0
    for k, mm in enumerate(mxu_work):
        mm()
        take = len(groups) - gi if k == len(mxu_work) - 1 else per
        for _ in range(take):
            if gi < len(groups):
                gate_group(*groups[gi])
                gi += 1


def _peer2_kernel(h_ref, xn_ref, u_ref, vt_ref, e1_ref, c_ref, e2_ref, r2_ref, o_ref, acc_scr, act0, act1, w0, w1, *,
                  nlt):
    e = pl.program_id(1)

    last = pl.num_programs(1) - 1
    args = (u_ref, xn_ref, vt_ref, e1_ref, c_ref, e2_ref, r2_ref, acc_scr)
    even = (act0, act1, w1, w0)
    odd = (act1, act0, w0, w1)
    steady = jnp.logical_and(e >= 2, e <= last - 2)

    @pl.when(e == 0)
    def _():
        acc_scr[...] = jnp.zeros(acc_scr.shape, F32)
        _peer2_step(*args, *even, nlt, do_value=False, do_gate=False)

    @pl.when(e == 1)
    def _():
        _peer2_step(*args, *odd, nlt, do_value=False)

    @pl.when(jnp.logical_and(steady, e % 2 == 0))
    def _():
        _peer2_step(*args, *even, nlt)

    @pl.when(jnp.logical_and(steady, e % 2 == 1))
    def _():
        _peer2_step(*args, *odd, nlt)

    @pl.when(e == last - 1)
    def _():
        _peer2_step(*args, *even, nlt, do_act=False)

    @pl.when(e == last)
    def _():
        _peer2_step(*args, *odd, nlt, do_act=False, do_gate=False)
        o_ref[...] = h_ref[...] + acc_scr[...].T


def _peer2(h, xn, u_bf16, vt_bf16, e1, cc, e2, r2, tt):
    t = h.shape[0]
    nlt = tt // LANES
    ne = PEER_NKEYS // NE1
    assert ne % 2 == 0
    last = ne - 1
    pfull = pl.BlockSpec((nlt, PEER_HEADS, PEER_NKEYS // 2, LANES), lambda i, e: (i, 0, 0, 0))
    gtile = pl.BlockSpec((nlt, PEER_HEADS, NE1, LANES), lambda i, e: (i, 0, jnp.clip(e - 1, 0, last), 0))
    slot = pltpu.VMEM((ETILE // 2, tt), jnp.uint32)
    return pl.pallas_call(
        functools.partial(_peer2_kernel, nlt=nlt),
        out_shape=jax.ShapeDtypeStruct((t, D_MODEL), F32),
        grid=(t // tt, ne + 2),
        in_specs=[pl.BlockSpec((tt, D_MODEL), lambda i, e: (i, 0)),
                  pl.BlockSpec((D_MODEL // 2, tt), lambda i, e: (0, i)),
                  pl.BlockSpec((ETILE // 2, D_MODEL), lambda i, e: (jnp.minimum(e, last), 0)),
                  pl.BlockSpec((D_MODEL // 2, ETILE), lambda i, e: (0, jnp.clip(e - 2, 0, last))),
                  gtile, gtile, pfull, pfull],
        out_specs=pl.BlockSpec((tt, D_MODEL), lambda i, e: (i, 0)),
        scratch_shapes=[pltpu.VMEM((D_MODEL, tt), F32), slot, slot, slot, slot],
        compiler_params=_cparams(("parallel", "arbitrary")),
        name="peer2",
    )(h, xn, u_bf16, vt_bf16, e1, cc, e2, r2)


def _prep_layer(i, w_in, lbs, hgrn_norm, gdn_conv_w, gdn_a_log, gdn_dt_bias, gdn_norm, conf_dw_w, conf_dw_b,
                conf_ln_g, conf_ln_b, pool_w, pool_scale, w_out, peer_wq, peer_keys, peer_u, peer_v, ple_gate_w,
                ple_proj_w):
    wi = w_in[i]
    hk = HEADS * DH
    c0 = 4 * hk
    qkv = wi[:, c0:c0 + 3 * hk]
    zbg = wi[:, c0 + 3 * hk:c0 + 4 * hk]
    zbb = wi[:, c0 + 4 * hk:c0 + 4 * hk + HEADS]
    zba = wi[:, c0 + 4 * hk + HEADS:c0 + 4 * hk + 2 * HEADS]
    rest = wi[:, c0 + 4 * hk + 2 * HEADS:]
    w_perm = jnp.concatenate([wi[:, :c0], qkv, zbg, rest, jnp.repeat(zbb, DH, axis=1), jnp.repeat(zba, DH, axis=1)],
                             axis=1)
    kbd = peer_keys[i].reshape(2 * PEER_HEADS, PEER_NKEYS, PEER_HALF).astype(BF16)
    pw = pool_w[i]
    eye4 = jnp.eye(4, dtype=F32)
    pwbd = (pw[:, :, None, :] * eye4[:, None, :, None]).reshape(GW, GW)
    return dict(
        w_in=_pack_rows(w_perm), lb=lbs[i][None, :], hgrn_norm=hgrn_norm[i][None, :],
        gdn_conv_w=gdn_conv_w[i], gdn_a_log=jnp.repeat(gdn_a_log[i], DH)[None, :],
        gdn_dt_bias=jnp.repeat(gdn_dt_bias[i], DH)[None, :], gdn_norm=gdn_norm[i][None, :],
        conf_dw_w=conf_dw_w[i], conf_dw_b=conf_dw_b[i][None, :], conf_ln_g=conf_ln_g[i][None, :],
        conf_ln_b=conf_ln_b[i][None, :], pool_w=pwbd, pool_scale=pool_scale[i][None, :],
        w_out=_pack_rows(w_out[i]), peer_wq=_pack_rows(peer_wq[i].T), kbd=kbd,
        peer_u=_pack_rows(peer_u[i]), peer_vt=_pack_rows(peer_v[i].T),
        ple_gate_w=_pack_rows(ple_gate_w[i]), ple_proj_w=_pack_rows(ple_proj_w[i]))


def _tail(buf, x, n):
    l = x.shape[1]
    if l >= n:
        return x[:, l - n:]
    return jnp.concatenate([buf[:, l:], x], axis=1)


def _trunk(x, p, pos0, s_a, s_b, c_b, c_c, c_d, layers, norms, norm_final):
    bsz, l, _ = x.shape
    t = bsz * l
    prompt = l >= 64
    c = 64 if prompt else l
    nb = 1 if prompt else 8
    cc = min(256, l)
    tm = min(512, t)
    tt = min(512, t)
    h = x.reshape(t, D_MODEL)
    n_a, n_b, nb_b, nb_c, nb_d = [], [], [], [], []
    for i, lw in enumerate(layers):
        z = _inproj(h, norms['mix'][i][None, :], lw['w_in'], tm)
        z3 = z.reshape(bsz, l, ZW)
        eye4 = jnp.eye(HEADS, dtype=F32)
        s0t = (jnp.swapaxes(s_a[i], 2, 3)[:, :, :, None, :] * eye4[None, :, None, :, None]).reshape(bsz, GW, GW)
        o_a, st = _hgrn(z3, lw['lb'], lw['hgrn_norm'], s0t, min(4, bsz) if prompt else nb, min(c, 32))
        st = st.reshape(bsz, HEADS, DH, HEADS, DH)
        n_a.append(jnp.stack([jnp.swapaxes(st[:, j, :, j, :], 1, 2) for j in range(HEADS)], axis=1))
        buf8 = jnp.pad(c_b[i], ((0, 0), (8 - (B_CONV - 1), 0), (0, 0)))
        o_b, sb = _gdn(z3, lw['gdn_conv_w'], lw['gdn_a_log'], lw['gdn_dt_bias'], lw['gdn_norm'], buf8, s_b[i],
                       min(4, bsz) if prompt else nb, c)
        n_b.append(sb)
        nb_b.append(_tail(c_b[i], z3[:, :, ZB_BQ * GW:(ZB_BV + 1) * GW], B_CONV - 1))
        cbuf = jnp.pad(c_c[i], ((0, 0), (CPAD - (C_CONV - 1), 0), (0, 0)))
        dbuf = jnp.pad(c_d[i], ((0, 0), (DPAD - POOL_BUF, 0), (0, 0)))
        o_c, u_c, o_d = _convpool(z3, lw['conf_dw_w'], lw['conf_dw_b'], lw['conf_ln_g'], lw['conf_ln_b'],
                                  lw['pool_w'], lw['pool_scale'], cbuf, dbuf, nb, cc, pos0)
        nb_c.append(_tail(c_c[i], u_c, C_CONV - 1))
        nb_d.append(_tail(c_d[i], z3[:, :, ZB_D * GW:(ZB_D + 1) * GW], POOL_BUF))
        h = _outproj(h, o_a.reshape(t, GW), o_b.reshape(t, GW), o_c.reshape(t, GW), o_d.reshape(t, GW),
                     lw['w_out'], tm)
        xn, e1, cnt, e2, r2 = _peer1(h, norms['ffn'][i][None, :], lw['peer_wq'], lw['kbd'], tt)
        h = _peer2(h, xn, lw['peer_u'], lw['peer_vt'], e1, cnt, e2, r2, min(1024, t))
        h = _ple(h, p[i].reshape(t, D_PLE), norms['ple'][i][None, :], lw['ple_gate_w'], lw['ple_proj_w'],
                 norm_final[None, :], tm, final=(i == len(layers) - 1))
    st = lambda lst: jnp.stack(lst, axis=0)
    return h.reshape(bsz, l, D_MODEL), st(n_a), st(n_b), st(nb_b), st(nb_c), st(nb_d)


def kernel(x_prompt, x_sample, state_hgrn, state_gdn, state_gdn_conv, state_conf_conv, state_pool, p_prompt, p_sample, norm_mix, w_in, hgrn_lb_logits, hgrn_norm, gdn_conv_w, gdn_a_log, gdn_dt_bias, gdn_norm, conf_dw_w, conf_dw_b, conf_ln_g, conf_ln_b, pool_w, pool_scale, w_out, norm_ffn, peer_wq, peer_keys, peer_u, peer_v, norm_ple, ple_gate_w, ple_proj_w, norm_final):
    sm = jax.nn.softmax(hgrn_lb_logits.astype(F32), axis=0)
    lbs = jnp.cumsum(sm, axis=0) - sm[0:1]
    layers = [_prep_layer(i, w_in, lbs, hgrn_norm, gdn_conv_w, gdn_a_log, gdn_dt_bias, gdn_norm, conf_dw_w,
                          conf_dw_b, conf_ln_g, conf_ln_b, pool_w, pool_scale, w_out, peer_wq, peer_keys, peer_u,
                          peer_v, ple_gate_w, ple_proj_w) for i in range(DEPTH)]
    norms = dict(mix=norm_mix, ffn=norm_ffn, ple=norm_ple)
    bp = x_prompt.shape[0]
    z = lambda *s: jnp.zeros((DEPTH, bp) + s, F32)
    yp, hp, gp, gcp, ccp, pp = _trunk(x_prompt, p_prompt, 0, z(HEADS, DH, DH), z(HEADS, DH, DH),
                                      z(B_CONV - 1, 3 * GW), z(C_CONV - 1, GW), z(POOL_BUF, GW),
                                      layers, norms, norm_final)
    ys, hs, gs, gcs, ccs, ps = _trunk(x_sample, p_sample, PAST_LEN, state_hgrn, state_gdn, state_gdn_conv,
                                      state_conf_conv, state_pool, layers, norms, norm_final)
    return (yp, ys, hp, gp, gcp, ccp, pp, hs, gs, gcs, ccs, ps)
```

```python
import functools

import numpy as np
import jax
import jax.numpy as jnp
from jax import lax
from jax.experimental import pallas as pl
from jax.experimental.pallas import tpu as pltpu

F32 = jnp.float32
BF16 = jnp.bfloat16

D_MODEL = 1024
DEPTH = 4
PAST_LEN = 16384
EPS = 1e-6
NEG_BIG = -1e30
D_PLE = 256
HEADS = 4
DH = 64
GW = 256
B_CONV = 4
C_CONV = 31
POOL_WINDOWS = (2, 4, 8, 16)
POOL_BUF = 15
PEER_HEADS = 8
PEER_NKEYS = 128
PEER_HALF = 64
PEER_TOPK = 16

ZB_AQ, ZB_AF, ZB_AI, ZB_AG, ZB_BQ, ZB_BK, ZB_BV, ZB_BG, ZB_CA, ZB_CB, ZB_D, ZB_BB, ZB_BA = range(13)
ZW = 13 * GW

LANES = 128
SOLVE_BLOCK = 16
VMEM_LIMIT = 56 * 1024 * 1024


def _cparams(sem, flags=None):
    return pltpu.CompilerParams(dimension_semantics=sem, vmem_limit_bytes=VMEM_LIMIT, flags=flags)


def _bdot(a, b):
    return jnp.dot(a.astype(BF16), b.astype(BF16), preferred_element_type=F32)


def _bdot_nt(a, b):
    return lax.dot_general(a.astype(BF16), b.astype(BF16), (((1,), (1,)), ((), ())),
                           preferred_element_type=F32)


def _split3(a):
    a1 = a.astype(BF16)
    r1 = a - a1.astype(F32)
    a2 = r1.astype(BF16)
    a3 = (r1 - a2.astype(F32)).astype(BF16)
    return a1, a2, a3


def _xdot(a, m01):
    a1, a2, a3 = _split3(a)
    m = m01.astype(BF16)
    return (jnp.dot(a1, m, preferred_element_type=F32) + jnp.dot(a2, m, preferred_element_type=F32)
            + jnp.dot(a3, m, preferred_element_type=F32))


def _xdot_left(m01, a):
    a1, a2, a3 = _split3(a)
    m = m01.astype(BF16)
    return (jnp.dot(m, a1, preferred_element_type=F32) + jnp.dot(m, a2, preferred_element_type=F32)
            + jnp.dot(m, a3, preferred_element_type=F32))


def _dot2(a, b):
    ah = a.astype(BF16)
    al = (a - ah.astype(F32)).astype(BF16)
    bh = b.astype(BF16)
    bl = (b - bh.astype(F32)).astype(BF16)
    return (jnp.dot(ah, bh, preferred_element_type=F32) + jnp.dot(ah, bl, preferred_element_type=F32)
            + jnp.dot(al, bh, preferred_element_type=F32))


def _xtranspose(a, eye):
    a1, a2, a3 = _split3(a)
    e = eye.astype(BF16)
    dn = (((1,), (1,)), ((), ()))
    return (lax.dot_general(e, a1, dn, preferred_element_type=F32)
            + lax.dot_general(e, a2, dn, preferred_element_type=F32)
            + lax.dot_general(e, a3, dn, preferred_element_type=F32))


def _pack_rows(x):
    bits = lax.bitcast_convert_type(x.astype(BF16), jnp.uint16)
    return bits[0::2].astype(jnp.uint32) | (bits[1::2].astype(jnp.uint32) << 16)


def _rows_bf16(words):
    return pltpu.bitcast(words, BF16)


def _iota(shape, axis):
    return lax.broadcasted_iota(jnp.int32, shape, axis)


def _tri_incl(c):
    return jnp.where(_iota((c, c), 1) <= _iota((c, c), 0), 1.0, 0.0).astype(F32)


def _eye(n):
    return jnp.where(_iota((n, n), 0) == _iota((n, n), 1), 1.0, 0.0).astype(F32)


def _block_ones(n, blk):
    return jnp.where(_iota((n, n), 0) // blk == _iota((n, n), 1) // blk, 1.0, 0.0).astype(F32)


def _rms(x, g):
    return x * lax.rsqrt(jnp.mean(x * x, axis=-1, keepdims=True) + EPS) * g


def _sigmoid(x):
    return jax.nn.sigmoid(x)


def _silu(x):
    return x * jax.nn.sigmoid(x)


def _inproj_kernel(x_ref, g_ref, w_ref, o_ref):
    xn = _rms(x_ref[...], g_ref[...])
    o_ref[...] = jnp.dot(xn.astype(BF16), _rows_bf16(w_ref[...]), preferred_element_type=F32)


def _inproj(h, g, w_bf16, tm):
    t = h.shape[0]
    return pl.pallas_call(
        _inproj_kernel,
        out_shape=jax.ShapeDtypeStruct((t, ZW), F32),
        grid=(t // tm,),
        in_specs=[pl.BlockSpec((tm, D_MODEL), lambda i: (i, 0)),
                  pl.BlockSpec((1, D_MODEL), lambda i: (0, 0)),
                  pl.BlockSpec((D_MODEL // 2, ZW), lambda i: (0, 0))],
        out_specs=pl.BlockSpec((tm, ZW), lambda i: (i, 0)),
        compiler_params=_cparams(("parallel",)),
        name="inproj",
    )(h, g, w_bf16)


def _outproj_kernel(h_ref, a_ref, b_ref, c_ref, d_ref, w_ref, o_ref):
    acc = h_ref[...]
    for i, r in enumerate((a_ref, b_ref, c_ref, d_ref)):
        acc = acc + jnp.dot(r[...].astype(BF16), _rows_bf16(w_ref[i * GW // 2:(i + 1) * GW // 2, :]),
                            preferred_element_type=F32)
    o_ref[...] = acc


def _outproj(h, oa, ob, oc, od, w_bf16, tm):
    t = h.shape[0]
    tok = lambda w: pl.BlockSpec((tm, w), lambda i: (i, 0))
    return pl.pallas_call(
        _outproj_kernel,
        out_shape=jax.ShapeDtypeStruct((t, D_MODEL), F32),
        grid=(t // tm,),
        in_specs=[tok(D_MODEL), tok(GW), tok(GW), tok(GW), tok(GW),
                  pl.BlockSpec((D_MODEL // 2, D_MODEL), lambda i: (0, 0))],
        out_specs=tok(D_MODEL),
        compiler_params=_cparams(("parallel",)),
        name="outproj",
    )(h, oa, ob, oc, od, w_bf16)


def _ple_kernel(h_ref, p_ref, g_ref, wg_ref, wp_ref, gf_ref, o_ref, *, final):
    h = h_ref[...]
    xn = _rms(h, g_ref[...])
    gate = _sigmoid(jnp.dot(xn.astype(BF16), _rows_bf16(wg_ref[...]), preferred_element_type=F32))
    proj = jnp.dot(p_ref[...].astype(BF16), _rows_bf16(wp_ref[...]), preferred_element_type=F32)
    out = h + gate * proj
    if final:
        out = _rms(out, gf_ref[...])
    o_ref[...] = out


def _ple(h, p, g, wg_bf16, wp_bf16, gfinal, tm, final):
    t = h.shape[0]
    tok = lambda w: pl.BlockSpec((tm, w), lambda i: (i, 0))
    full = lambda r, c: pl.BlockSpec((r, c), lambda i: (0, 0))
    return pl.pallas_call(
        functools.partial(_ple_kernel, final=final),
        out_shape=jax.ShapeDtypeStruct((t, D_MODEL), F32),
        grid=(t // tm,),
        in_specs=[tok(D_MODEL), tok(D_PLE), full(1, D_MODEL), full(D_MODEL // 2, D_MODEL), full(D_PLE // 2, D_MODEL),
                  full(1, D_MODEL)],
        out_specs=tok(D_MODEL),
        compiler_params=_cparams(("parallel",)),
        name="ple_final" if final else "ple",
    )(h, p, g, wg_bf16, wp_bf16, gfinal)


def _hgrn_kernel(z_ref, lb_ref, nw_ref, s0_ref, o_ref, sf_ref, st_scr, la_scr, q_scr, k_scr, v_scr, p_scr, *, nb, c):
    ci = pl.program_id(1)

    @pl.when(ci == 0)
    def _():
        st_scr[...] = s0_ref[...]

    lb = lb_ref[...]
    bones = _block_ones(GW, DH)
    tri = _tri_incl(c)
    rows = _iota((c, GW), 0)
    seqs = range(nb)
    for b in seqs:
        zf = z_ref[b, :, 1 * GW:2 * GW]
        q_scr[b] = _silu(z_ref[b, :, 0 * GW:1 * GW])
        k_scr[b] = (1.0 - lb) * _sigmoid(-zf)
        v_scr[b] = z_ref[b, :, 2 * GW:3 * GW]
        f = lb + (1.0 - lb) * _sigmoid(zf)
        la_scr[b] = _xdot_left(tri, jnp.log(f))

    def build(s, carry):
        for b in seqs:
            la_s = la_scr[b, pl.ds(s, 1), :]
            k_s = k_scr[b, pl.ds(s, 1), :]
            dec = jnp.exp(jnp.minimum(la_scr[b] - la_s, 0.0))
            p_scr[b, pl.ds(pl.multiple_of(s * c, c), c), :] = jnp.where(rows >= s, q_scr[b] * k_s * dec, 0.0)
        return carry

    lax.fori_loop(0, c, build, 0, unroll=min(4, c))
    for b in seqs:
        p_scr[b] = jnp.dot(p_scr[b].astype(BF16), bones.astype(BF16), preferred_element_type=F32)

    def consume(s, accs):
        return tuple(acc + p_scr[b, pl.ds(pl.multiple_of(s * c, c), c), :] * v_scr[b, pl.ds(s, 1), :]
                     for b, acc in zip(seqs, accs))

    os_ = lax.fori_loop(0, c, consume, tuple(jnp.zeros((c, GW), F32) for _ in seqs), unroll=min(4, c))
    sts = [st_scr[b] for b in seqs]
    las = [la_scr[b] for b in seqs]
    os_ = [o + _bdot_nt(q_scr[b] * jnp.exp(la), st) for b, o, la, st in zip(seqs, os_, las, sts)]
    upds = []
    for b, la in zip(seqs, las):
        k_dec = k_scr[b] * jnp.exp(la[c - 1:c, :] - la)
        upds.append(lax.dot_general(v_scr[b].astype(BF16), k_dec.astype(BF16), (((0,), (0,)), ((), ())),
                                    preferred_element_type=F32))
    for b, la, st, upd in zip(seqs, las, sts, upds):
        st_scr[b] = st * jnp.exp(la[c - 1:c, :]) + upd * bones
    mss = [_xdot(o * o, bones) * (1.0 / DH) for o in os_]
    for b, o, ms in zip(seqs, os_, mss):
        o_ref[b] = o * lax.rsqrt(ms + EPS) * nw_ref[...] * _sigmoid(z_ref[b, :, 3 * GW:4 * GW])

    @pl.when(ci == pl.num_programs(1) - 1)
    def _():
        sf_ref[...] = st_scr[...]


def _hgrn(z3, lb, nw, s0t, nb, c):
    bsz, l, _ = z3.shape
    return pl.pallas_call(
        functools.partial(_hgrn_kernel, nb=nb, c=c),
        out_shape=(jax.ShapeDtypeStruct((bsz, l, GW), F32), jax.ShapeDtypeStruct((bsz, GW, GW), F32)),
        grid=(bsz // nb, l // c),
        in_specs=[pl.BlockSpec((nb, c, 4 * GW), lambda b, i: (b, i, 0)),
                  pl.BlockSpec((1, GW), lambda b, i: (0, 0)),
                  pl.BlockSpec((1, GW), lambda b, i: (0, 0)),
                  pl.BlockSpec((nb, GW, GW), lambda b, i: (b, 0, 0))],
        out_specs=(pl.BlockSpec((nb, c, GW), lambda b, i: (b, i, 0)),
                   pl.BlockSpec((nb, GW, GW), lambda b, i: (b, 0, 0))),
        scratch_shapes=[pltpu.VMEM((nb, GW, GW), F32)] + [pltpu.VMEM((nb, c, GW), F32)] * 4
                       + [pltpu.VMEM((nb, c * c, GW), F32)],
        compiler_params=_cparams(("parallel", "arbitrary")),
        name="hgrn",
    )(z3, lb, nw, s0t)


def _gdn_kernel(zq_ref, zk_ref, zv_ref, zg_ref, zb_ref, za_ref, cw_ref, alog_ref, dtb_ref, nw_ref, buf_ref, s0_ref,
                o_ref, sf_ref, s_scr, xp_scr, *, nb, c):
    ci = pl.program_id(1)

    @pl.when(ci == 0)
    def _():
        s_scr[...] = s0_ref[...]
        xp_scr[:, 0:8, :] = buf_ref[...]

    bones = _block_ones(GW, DH)
    tri = _tri_incl(c)
    eye_c = _eye(c)
    eye_h = _eye(DH)
    tt = _iota((c, c), 0)
    ss = _iota((c, c), 1)
    causal = tt >= ss
    strict = tt > ss
    items = []
    for b in range(nb):
        for j, r in enumerate((zq_ref, zk_ref, zv_ref)):
            xp_scr[b, 8:8 + c, j * GW:(j + 1) * GW] = r[b]
        conv = cw_ref[3:4, :] * xp_scr[b, 8:8 + c, :]
        for j in range(B_CONV - 1):
            conv = conv + cw_ref[j:j + 1, :] * xp_scr[b, 5 + j:5 + j + c, :]
        xp_scr[b, 0:8, :] = xp_scr[b, c:c + 8, :]
        qkv = _silu(conv)
        q = qkv[:, 0:GW]
        k = qkv[:, GW:2 * GW]
        v = qkv[:, 2 * GW:3 * GW]
        q = q * lax.rsqrt(_xdot(q * q, bones) + EPS) * (DH ** -0.5)
        k = k * lax.rsqrt(_xdot(k * k, bones) + EPS)
        beta = _sigmoid(zb_ref[b])
        loga = -jnp.exp(alog_ref[...]) * jax.nn.softplus(za_ref[b] + dtb_ref[...])
        g = _xdot_left(tri, loga)
        for h in range(HEADS):
            hs = slice(h * DH, (h + 1) * DH)
            items.append(dict(b=b, h=h, hs=hs, q=q[:, hs], k=k[:, hs], v=v[:, hs], beta=beta[:, hs], g=g[:, hs]))
    for it in items:
        it['gcol'] = it['g'][:, 0:c]
    for it in items:
        it['grow'] = _xtranspose(it['gcol'], eye_c)
    for it in items:
        it['kk'] = _bdot_nt(it['k'], it['k'])
        it['qk'] = _bdot_nt(it['q'], it['k'])
    for it in items:
        it['dec'] = jnp.exp(jnp.where(causal, it['gcol'] - it['grow'], NEG_BIG))
        it['a'] = jnp.where(strict, it['beta'][:, 0:c] * it['kk'] * it['dec'], 0.0)
        it['r'] = jnp.concatenate([it['beta'] * it['v'], it['beta'] * jnp.exp(it['g']) * it['k']], axis=1)
    blk = min(SOLVE_BLOCK, c)
    for it in items:
        it['solved'] = []
    for j0 in range(0, c, blk):
        rbs = [it['r'][j0:j0 + blk, :] for it in items]
        abs_ = [it['a'][j0:j0 + blk, j0:j0 + blk] for it in items]
        for s in range(blk - 1):
            rbs = [rb - ab[:, s:s + 1] * rb[s:s + 1, :] for rb, ab in zip(rbs, abs_)]
        for it, rb in zip(items, rbs):
            it['solved'].append(rb)
        if j0 + blk < c:
            belows = [_dot2(it['a'][j0 + blk:, j0:j0 + blk], rb) for it, rb in zip(items, rbs)]
            for it, below in zip(items, belows):
                it['r'] = jnp.concatenate([it['r'][:j0 + blk, :], it['r'][j0 + blk:, :] - below], axis=0)
    for it in items:
        w = it['solved'][0] if len(it['solved']) == 1 else jnp.concatenate(it['solved'], axis=0)
        it['w1'] = w[:, 0:DH]
        it['w2'] = w[:, DH:2 * DH]
        it['s'] = s_scr[it['b'], it['h']]
    for it in items:
        it['u'] = it['w1'] - _bdot(it['w2'], it['s'])
    for it in items:
        it['o'] = _bdot(it['q'] * jnp.exp(it['g']), it['s']) + _bdot(it['qk'] * it['dec'], it['u'])
    for it in items:
        g_last = it['g'][c - 1:c, :]
        kd = it['k'] * jnp.exp(g_last - it['g'])
        kd_t = lax.dot_general(eye_h.astype(BF16), kd.astype(BF16), (((1,), (1,)), ((), ())),
                               preferred_element_type=F32)
        it['snew'] = jnp.exp(g_last) * it['s'] + _bdot(kd_t, it['u'])
    for it in items:
        s_scr[it['b'], it['h']] = it['snew']
        o = it['o']
        zg = zg_ref[it['b'], :, it['hs']]
        o = o * lax.rsqrt(jnp.mean(o * o, axis=-1, keepdims=True) + EPS) * nw_ref[...] * _silu(zg)
        o_ref[it['b'], :, it['hs']] = o

    @pl.when(ci == pl.num_programs(1) - 1)
    def _():
        sf_ref[...] = s_scr[...]


def _gdn(z3, cw, alog_x, dtb_x, nw, buf8, s0, nb, c):
    bsz, l, _ = z3.shape
    zblk = lambda j: pl.BlockSpec((nb, c, GW), lambda b, i, j=j: (b, i, j))
    full = lambda r, w: pl.BlockSpec((r, w), lambda b, i: (0, 0))
    return pl.pallas_call(
        functools.partial(_gdn_kernel, nb=nb, c=c),
        out_shape=(jax.ShapeDtypeStruct((bsz, l, GW), F32), jax.ShapeDtypeStruct((bsz, HEADS, DH, DH), F32)),
        grid=(bsz // nb, l // c),
        in_specs=[zblk(ZB_BQ), zblk(ZB_BK), zblk(ZB_BV), zblk(ZB_BG), zblk(ZB_BB), zblk(ZB_BA),
                  full(B_CONV, 3 * GW), full(1, GW), full(1, GW), full(1, DH),
                  pl.BlockSpec((nb, 8, 3 * GW), lambda b, i: (b, 0, 0)),
                  pl.BlockSpec((nb, HEADS, DH, DH), lambda b, i: (b, 0, 0, 0))],
        out_specs=(pl.BlockSpec((nb, c, GW), lambda b, i: (b, i, 0)),
                   pl.BlockSpec((nb, HEADS, DH, DH), lambda b, i: (b, 0, 0, 0))),
        scratch_shapes=[pltpu.VMEM((nb, HEADS, DH, DH), F32), pltpu.VMEM((nb, 8 + c, 3 * GW), F32)],
        compiler_params=_cparams(("parallel", "arbitrary")),
        name="gdn",
    )(z3, z3, z3, z3, z3, z3, cw, alog_x, dtb_x, nw, buf8, s0)


CPAD = 32
DPAD = 16


def _convpool_kernel(za_ref, zb_ref, zd_ref, dw_ref, db_ref, lg_ref, lb_ref, pw_ref, ps_ref, cbuf_ref, dbuf_ref,
                     oc_ref, u_ref, od_ref, xc_scr, xd_scr, *, nb, c, pos0):
    ci = pl.program_id(1)

    @pl.when(ci == 0)
    def _():
        xc_scr[:, 0:CPAD, :] = cbuf_ref[...]
        xd_scr[:, 0:DPAD, :] = dbuf_ref[...]

    lane = _iota((c, GW), 1)
    wl = jnp.where(lane < 64, 2.0, jnp.where(lane < 128, 4.0, jnp.where(lane < 192, 8.0, 16.0)))
    pos = (_iota((c, GW), 0) + (ci * c + pos0 + 1)).astype(F32)
    cnt = jnp.minimum(wl, pos)
    for b in range(nb):
        u = za_ref[b] * _sigmoid(zb_ref[b])
        u_ref[b] = u
        xc_scr[b, CPAD:CPAD + c, :] = u
        y = dw_ref[C_CONV - 1:C_CONV, :] * u
        for j in range(C_CONV - 1):
            y = y + dw_ref[j:j + 1, :] * xc_scr[b, 2 + j:2 + j + c, :]
        xc_scr[b, 0:CPAD, :] = xc_scr[b, c:c + CPAD, :]
        y = y + db_ref[...]
        mu = jnp.mean(y, axis=-1, keepdims=True)
        yc = y - mu
        var = jnp.mean(yc * yc, axis=-1, keepdims=True)
        oc_ref[b] = _silu(yc * lax.rsqrt(var + EPS) * lg_ref[...] + lb_ref[...])
        x = zd_ref[b]
        xd_scr[b, DPAD:DPAD + c, :] = x
        acc = x
        sums = {}
        for i in range(1, 16):
            acc = acc + xd_scr[b, DPAD - i:DPAD - i + c, :]
            if i + 1 in POOL_WINDOWS:
                sums[i + 1] = acc
        xd_scr[b, 0:DPAD, :] = xd_scr[b, c:c + DPAD, :]
        ssel = jnp.where(lane < 64, sums[2], jnp.where(lane < 128, sums[4], jnp.where(lane < 192, sums[8], sums[16])))
        diff = ssel / cnt - x
        od_ref[b] = _bdot(diff, pw_ref[...]) * ps_ref[...]


def _convpool(z3, dw, db, lg, lb, pwbd, ps, cbuf, dbuf, nb, c, pos0):
    bsz, l, _ = z3.shape
    zblk = lambda j: pl.BlockSpec((nb, c, GW), lambda b, i, j=j: (b, i, j))
    full = lambda r, w: pl.BlockSpec((r, w), lambda b, i: (0, 0))
    oblk = pl.BlockSpec((nb, c, GW), lambda b, i: (b, i, 0))
    osh = jax.ShapeDtypeStruct((bsz, l, GW), F32)
    return pl.pallas_call(
        functools.partial(_convpool_kernel, nb=nb, c=c, pos0=pos0),
        out_shape=(osh, osh, osh),
        grid=(bsz // nb, l // c),
        in_specs=[zblk(ZB_CA), zblk(ZB_CB), zblk(ZB_D), full(C_CONV, GW), full(1, GW), full(1, GW), full(1, GW),
                  full(GW, GW), full(1, GW),
                  pl.BlockSpec((nb, CPAD, GW), lambda b, i: (b, 0, 0)),
                  pl.BlockSpec((nb, DPAD, GW), lambda b, i: (b, 0, 0))],
        out_specs=(oblk, oblk, oblk),
        scratch_shapes=[pltpu.VMEM((nb, CPAD + c, GW), F32), pltpu.VMEM((nb, DPAD + c, GW), F32)],
        compiler_params=_cparams(("parallel", "arbitrary")),
        name="convpool",
    )(z3, z3, z3, dw, db, lg, lb, pwbd, ps, cbuf, dbuf)


NKH = 2 * PEER_HEADS * PEER_NKEYS
_CANDS = [(a, b) for a in range(PEER_TOPK) for b in range(PEER_TOPK) if (a + 1) * (b + 1) <= PEER_TOPK]


def _oddeven_merge_sort_pairs(n):
    pairs = []

    def merge(lo, hi, r):
        step = r * 2
        if step < hi - lo:
            merge(lo, hi, step)
            merge(lo + r, hi, step)
            pairs.extend((i, i + r) for i in range(lo + r, hi - r, step))
        else:
            pairs.append((lo, lo + r))

    def sort(lo, hi):
        if hi - lo >= 1:
            mid = lo + (hi - lo) // 2
            sort(lo, mid)
            sort(mid + 1, hi)
            merge(lo, hi, 1)

    sort(0, n - 1)
    return pairs


_SORT16 = _oddeven_merge_sort_pairs(PEER_NKEYS // 8)


def _peer1_kernel(h_ref, g_ref, wq_ref, kbd_ref, xn_ref, e1_ref, c_ref, e2_ref, r2_ref, s_scr, sv_scr, d_scr, zi_scr,
                  *, nlt):
    xn = _rms(h_ref[...], g_ref[...]).T.astype(BF16)
    xn_ref[...] = pltpu.bitcast(xn, jnp.uint32)
    qt = jnp.dot(_rows_bf16(wq_ref[...]), xn, preferred_element_type=F32).astype(BF16)
    for hp in range(2 * PEER_HEADS):
        sc = jnp.dot(kbd_ref[hp], qt[hp * PEER_HALF:(hp + 1) * PEER_HALF, :], preferred_element_type=F32)
        for lt in range(nlt):
            s_scr[lt, hp * PEER_NKEYS:(hp + 1) * PEER_NKEYS, :] = sc[:, lt * LANES:(lt + 1) * LANES]

    def lane_tile(lt, carry):
        for h in range(PEER_HEADS):
            for p in range(2):
                s = s_scr[lt, (2 * h + p) * PEER_NKEYS:(2 * h + p + 1) * PEER_NKEYS, :]
                lists = [s[8 * i:8 * (i + 1), :] for i in range(PEER_NKEYS // 8)]
                for i, j in _SORT16:
                    lists[i], lists[j] = jnp.maximum(lists[i], lists[j]), jnp.minimum(lists[i], lists[j])
                svs = []
                for a in range(PEER_TOPK):
                    m = jnp.max(lists[0], axis=0, keepdims=True)
                    sv_scr[p, a, h:h + 1, :] = m
                    svs.append(m)
                    popped = lists[0] == m
                    for i in range(PEER_TOPK - 1 - a):
                        lists[i] = jnp.where(popped, lists[i + 1], lists[i])
                if p == 1:
                    rank = jnp.full(s.shape, float(PEER_TOPK), F32)
                    for a in reversed(range(PEER_TOPK)):
                        rank = jnp.where(s >= svs[a], float(a), rank)
                    r2_ref[lt, h] = pltpu.bitcast(rank.astype(BF16), jnp.uint32)
                    e2_ref[lt, h] = pltpu.bitcast(jnp.exp(s - svs[0]).astype(BF16), jnp.uint32)
        sv1 = [sv_scr[0, a] for a in range(PEER_TOPK)]
        sv2 = [sv_scr[1, a] for a in range(PEER_TOPK)]
        vals = [sv1[a] + sv2[b] for a, b in _CANDS]
        n = len(_CANDS)
        before = [jnp.zeros((PEER_HEADS, LANES), F32) for _ in range(n)]
        for i in range(n):
            ai, bi = _CANDS[i]
            for j in range(i + 1, n):
                aj, bj = _CANDS[j]
                if ai <= aj and bi <= bj:
                    before[j] = before[j] + 1.0
                else:
                    t = jnp.where(vals[i] >= vals[j], 1.0, 0.0)
                    before[j] = before[j] + t
                    before[i] = before[i] + (1.0 - t)
        ex1 = [jnp.exp(sv1[a] - sv1[0]) for a in range(PEER_TOPK)]
        ex2 = [jnp.exp(sv2[b] - sv2[0]) for b in range(PEER_TOPK)]
        cnt = [jnp.zeros((PEER_HEADS, LANES), F32) for _ in range(PEER_TOPK)]
        zsum = jnp.zeros((PEER_HEADS, LANES), F32)
        for i, (a, b) in enumerate(_CANDS):
            sel = jnp.where(before[i] < float(PEER_TOPK), 1.0, 0.0)
            cnt[a] = cnt[a] + sel
            zsum = zsum + sel * ex1[a] * ex2[b]
        for a in range(PEER_TOPK):
            d_scr[a] = cnt[a]
        zi_scr[...] = 1.0 / zsum
        for h in range(PEER_HEADS):
            s = s_scr[lt, 2 * h * PEER_NKEYS:(2 * h + 1) * PEER_NKEYS, :]
            cc = jnp.zeros(s.shape, F32)
            for a in reversed(range(PEER_TOPK)):
                cc = jnp.where(s >= sv_scr[0, a, h:h + 1, :], d_scr[a, h:h + 1, :], cc)
            c_ref[lt, h] = cc
            e1_ref[lt, h] = jnp.exp(s - sv_scr[0, 0, h:h + 1, :]) * zi_scr[h:h + 1, :]
        return carry

    lax.fori_loop(0, nlt, lane_tile, 0)


def _peer1(h, g, wq_bf16, kbd_bf16, tt):
    t = h.shape[0]
    nlt = tt // LANES
    gsh = jax.ShapeDtypeStruct((t // LANES, PEER_HEADS, PEER_NKEYS, LANES), F32)
    gblk = pl.BlockSpec((nlt, PEER_HEADS, PEER_NKEYS, LANES), lambda i: (i, 0, 0, 0))
    psh = jax.ShapeDtypeStruct((t // LANES, PEER_HEADS, PEER_NKEYS // 2, LANES), jnp.uint32)
    pblk = pl.BlockSpec((nlt, PEER_HEADS, PEER_NKEYS // 2, LANES), lambda i: (i, 0, 0, 0))
    return pl.pallas_call(
        functools.partial(_peer1_kernel, nlt=nlt),
        out_shape=(jax.ShapeDtypeStruct((D_MODEL // 2, t), jnp.uint32), gsh, gsh, psh, psh),
        grid=(t // tt,),
        in_specs=[pl.BlockSpec((tt, D_MODEL), lambda i: (i, 0)),
                  pl.BlockSpec((1, D_MODEL), lambda i: (0, 0)),
                  pl.BlockSpec((D_MODEL // 2, D_MODEL), lambda i: (0, 0)),
                  pl.BlockSpec((2 * PEER_HEADS, PEER_NKEYS, PEER_HALF), lambda i: (0, 0, 0))],
        out_specs=(pl.BlockSpec((D_MODEL // 2, tt), lambda i: (0, i)), gblk, gblk, pblk, pblk),
        scratch_shapes=[pltpu.VMEM((nlt, NKH, LANES), F32),
                        pltpu.VMEM((2, PEER_TOPK, PEER_HEADS, LANES), F32),
                        pltpu.VMEM((PEER_TOPK, PEER_HEADS, LANES), F32),
                        pltpu.VMEM((PEER_HEADS, LANES), F32)],
        compiler_params=_cparams(("parallel",)),
        name="peer1",
    )(h, g, wq_bf16, kbd_bf16)


NE1 = 8
ETILE = NE1 * PEER_NKEYS


def _peer2_step(u_ref, xn_ref, vt_ref, e1_ref, c_ref, e2_ref, r2_ref, acc_scr, act_w, act_r, w_w, w_r, nlt,
                do_value=True, do_gate=True, do_act=True):
    tile = (PEER_NKEYS, LANES)
    zero = jnp.zeros(tile, BF16)
    grp = 2
    nmb = 4
    mblk = D_MODEL // nmb

    def value_rows(m):
        rs = slice(m * mblk, (m + 1) * mblk)
        vrows = _rows_bf16(vt_ref[m * mblk // 2:(m + 1) * mblk // 2, :])
        acc_scr[rs, :] += jnp.dot(vrows, _rows_bf16(w_r[...]), preferred_element_type=F32)

    def act_rows(m):
        urows = _rows_bf16(u_ref[m * mblk // 2:(m + 1) * mblk // 2, :])
        a = jnp.dot(urows, _rows_bf16(xn_ref[...]), preferred_element_type=F32).astype(BF16)
        a = 0.5 * a * (1.0 + lax.erf(a * 0.7071067811865476))
        act_w[m * mblk // 2:(m + 1) * mblk // 2, :] = pltpu.bitcast(a, jnp.uint32)

    def gate_group(lt, g0):
        ls = slice(lt * LANES, (lt + 1) * LANES)
        gates = [zero] * grp
        for h in range(PEER_HEADS):
            rank = pltpu.bitcast(r2_ref[lt, h], BF16)
            wkey = pltpu.bitcast(e2_ref[lt, h], BF16)
            for j in range(grp):
                i1 = g0 + j
                cnt = jnp.broadcast_to(c_ref[lt, h, i1:i1 + 1, :], tile).astype(BF16)
                wgt = jnp.broadcast_to(e1_ref[lt, h, i1:i1 + 1, :], tile).astype(BF16)
                gates[j] = gates[j] + jnp.where(rank < cnt, wkey, zero) * wgt
        for j in range(grp):
            ps = slice((g0 + j) * PEER_NKEYS // 2, (g0 + j + 1) * PEER_NKEYS // 2)
            w_w[ps, ls] = pltpu.bitcast(gates[j] * pltpu.bitcast(act_r[ps, ls], BF16), jnp.uint32)

    groups = [(lt, g0) for lt in range(nlt) for g0 in range(0, NE1, grp)] if do_gate else []
    mxu_work = [f for m in range(nmb) for f, on in ((functools.partial(value_rows, m), do_value),
                                                  (functools.partial(act_rows, m), do_act)) if on]
    if not mxu_work:
        mxu_work = [lambda: None]
    per = max(1, len(groups) // len(mxu_work))
    gi = 0
    for k, mm in enumerate(mxu_work):
        mm()
        take = len(groups) - gi if k == len(mxu_work) - 1 else per
        for _ in range(take):
            if gi < len(groups):
                gate_group(*groups[gi])
                gi += 1


def _peer2_kernel(h_ref, xn_ref, u_ref, vt_ref, e1_ref, c_ref, e2_ref, r2_ref, o_ref, acc_scr, act0, act1, w0, w1, *,
                  nlt):
    e = pl.program_id(1)

    last = pl.num_programs(1) - 1
    args = (u_ref, xn_ref, vt_ref, e1_ref, c_ref, e2_ref, r2_ref, acc_scr)
    even = (act0, act1, w1, w0)
    odd = (act1, act0, w0, w1)
    steady = jnp.logical_and(e >= 2, e <= last - 2)

    @pl.when(e == 0)
    def _():
        acc_scr[...] = jnp.zeros(acc_scr.shape, F32)
        _peer2_step(*args, *even, nlt, do_value=False, do_gate=False)

    @pl.when(e == 1)
    def _():
        _peer2_step(*args, *odd, nlt, do_value=False)

    @pl.when(jnp.logical_and(steady, e % 2 == 0))
    def _():
        _peer2_step(*args, *even, nlt)

    @pl.when(jnp.logical_and(steady, e % 2 == 1))
    def _():
        _peer2_step(*args, *odd, nlt)

    @pl.when(e == last - 1)
    def _():
        _peer2_step(*args, *even, nlt, do_act=False)

    @pl.when(e == last)
    def _():
        _peer2_step(*args, *odd, nlt, do_act=False, do_gate=False)
        o_ref[...] = h_ref[...] + acc_scr[...].T


def _peer2(h, xn, u_bf16, vt_bf16, e1, cc, e2, r2, tt):
    t = h.shape[0]
    nlt = tt // LANES
    ne = PEER_NKEYS // NE1
    assert ne % 2 == 0
    last = ne - 1
    pfull = pl.BlockSpec((nlt, PEER_HEADS, PEER_NKEYS // 2, LANES), lambda i, e: (i, 0, 0, 0))
    gtile = pl.BlockSpec((nlt, PEER_HEADS, NE1, LANES), lambda i, e: (i, 0, jnp.clip(e - 1, 0, last), 0))
    slot = pltpu.VMEM((ETILE // 2, tt), jnp.uint32)
    return pl.pallas_call(
        functools.partial(_peer2_kernel, nlt=nlt),
        out_shape=jax.ShapeDtypeStruct((t, D_MODEL), F32),
        grid=(t // tt, ne + 2),
        in_specs=[pl.BlockSpec((tt, D_MODEL), lambda i, e: (i, 0)),
                  pl.BlockSpec((D_MODEL // 2, tt), lambda i, e: (0, i)),
                  pl.BlockSpec((ETILE // 2, D_MODEL), lambda i, e: (jnp.minimum(e, last), 0)),
                  pl.BlockSpec((D_MODEL // 2, ETILE), lambda i, e: (0, jnp.clip(e - 2, 0, last))),
                  gtile, gtile, pfull, pfull],
        out_specs=pl.BlockSpec((tt, D_MODEL), lambda i, e: (i, 0)),
        scratch_shapes=[pltpu.VMEM((D_MODEL, tt), F32), slot, slot, slot, slot],
        compiler_params=_cparams(("parallel", "arbitrary")),
        name="peer2",
    )(h, xn, u_bf16, vt_bf16, e1, cc, e2, r2)


def _prep_layer(i, w_in, lbs, hgrn_norm, gdn_conv_w, gdn_a_log, gdn_dt_bias, gdn_norm, conf_dw_w, conf_dw_b,
                conf_ln_g, conf_ln_b, pool_w, pool_scale, w_out, peer_wq, peer_keys, peer_u, peer_v, ple_gate_w,
                ple_proj_w):
    wi = w_in[i]
    hk = HEADS * DH
    c0 = 4 * hk
    qkv = wi[:, c0:c0 + 3 * hk]
    zbg = wi[:, c0 + 3 * hk:c0 + 4 * hk]
    zbb = wi[:, c0 + 4 * hk:c0 + 4 * hk + HEADS]
    zba = wi[:, c0 + 4 * hk + HEADS:c0 + 4 * hk + 2 * HEADS]
    rest = wi[:, c0 + 4 * hk + 2 * HEADS:]
    w_perm = jnp.concatenate([wi[:, :c0], qkv, zbg, rest, jnp.repeat(zbb, DH, axis=1), jnp.repeat(zba, DH, axis=1)],
                             axis=1)
    kbd = peer_keys[i].reshape(2 * PEER_HEADS, PEER_NKEYS, PEER_HALF).astype(BF16)
    pw = pool_w[i]
    eye4 = jnp.eye(4, dtype=F32)
    pwbd = (pw[:, :, None, :] * eye4[:, None, :, None]).reshape(GW, GW)
    return dict(
        w_in=_pack_rows(w_perm), lb=lbs[i][None, :], hgrn_norm=hgrn_norm[i][None, :],
        gdn_conv_w=gdn_conv_w[i], gdn_a_log=jnp.repeat(gdn_a_log[i], DH)[None, :],
        gdn_dt_bias=jnp.repeat(gdn_dt_bias[i], DH)[None, :], gdn_norm=gdn_norm[i][None, :],
        conf_dw_w=conf_dw_w[i], conf_dw_b=conf_dw_b[i][None, :], conf_ln_g=conf_ln_g[i][None, :],
        conf_ln_b=conf_ln_b[i][None, :], pool_w=pwbd, pool_scale=pool_scale[i][None, :],
        w_out=_pack_rows(w_out[i]), peer_wq=_pack_rows(peer_wq[i].T), kbd=kbd,
        peer_u=_pack_rows(peer_u[i]), peer_vt=_pack_rows(peer_v[i].T),
        ple_gate_w=_pack_rows(ple_gate_w[i]), ple_proj_w=_pack_rows(ple_proj_w[i]))


def _tail(buf, x, n):
    l = x.shape[1]
    if l >= n:
        return x[:, l - n:]
    return jnp.concatenate([buf[:, l:], x], axis=1)


def _trunk(x, p, pos0, s_a, s_b, c_b, c_c, c_d, layers, norms, norm_final):
    bsz, l, _ = x.shape
    t = bsz * l
    prompt = l >= 64
    c = 64 if prompt else l
    nb = 1 if prompt else 8
    cc = min(256, l)
    tm = min(512, t)
    tt = min(512, t)
    h = x.reshape(t, D_MODEL)
    n_a, n_b, nb_b, nb_c, nb_d = [], [], [], [], []
    for i, lw in enumerate(layers):
        z = _inproj(h, norms['mix'][i][None, :], lw['w_in'], tm)
        z3 = z.reshape(bsz, l, ZW)
        eye4 = jnp.eye(HEADS, dtype=F32)
        s0t = (jnp.swapaxes(s_a[i], 2, 3)[:, :, :, None, :] * eye4[None, :, None, :, None]).reshape(bsz, GW, GW)
        o_a, st = _hgrn(z3, lw['lb'], lw['hgrn_norm'], s0t, min(4, bsz) if prompt else nb, min(c, 32))
        st = st.reshape(bsz, HEADS, DH, HEADS, DH)
        n_a.append(jnp.stack([jnp.swapaxes(st[:, j, :, j, :], 1, 2) for j in range(HEADS)], axis=1))
        buf8 = jnp.pad(c_b[i], ((0, 0), (8 - (B_CONV - 1), 0), (0, 0)))
        o_b, sb = _gdn(z3, lw['gdn_conv_w'], lw['gdn_a_log'], lw['gdn_dt_bias'], lw['gdn_norm'], buf8, s_b[i],
                       min(4, bsz) if prompt else nb, c)
        n_b.append(sb)
        nb_b.append(_tail(c_b[i], z3[:, :, ZB_BQ * GW:(ZB_BV + 1) * GW], B_CONV - 1))
        cbuf = jnp.pad(c_c[i], ((0, 0), (CPAD - (C_CONV - 1), 0), (0, 0)))
        dbuf = jnp.pad(c_d[i], ((0, 0), (DPAD - POOL_BUF, 0), (0, 0)))
        o_c, u_c, o_d = _convpool(z3, lw['conf_dw_w'], lw['conf_dw_b'], lw['conf_ln_g'], lw['conf_ln_b'],
                                  lw['pool_w'], lw['pool_scale'], cbuf, dbuf, nb, cc, pos0)
        nb_c.append(_tail(c_c[i], u_c, C_CONV - 1))
        nb_d.append(_tail(c_d[i], z3[:, :, ZB_D * GW:(ZB_D + 1) * GW], POOL_BUF))
        h = _outproj(h, o_a.reshape(t, GW), o_b.reshape(t, GW), o_c.reshape(t, GW), o_d.reshape(t, GW),
                     lw['w_out'], tm)
        xn, e1, cnt, e2, r2 = _peer1(h, norms['ffn'][i][None, :], lw['peer_wq'], lw['kbd'], tt)
        h = _peer2(h, xn, lw['peer_u'], lw['peer_vt'], e1, cnt, e2, r2, min(1024, t))
        h = _ple(h, p[i].reshape(t, D_PLE), norms['ple'][i][None, :], lw['ple_gate_w'], lw['ple_proj_w'],
                 norm_final[None, :], tm, final=(i == len(layers) - 1))
    st = lambda lst: jnp.stack(lst, axis=0)
    return h.reshape(bsz, l, D_MODEL), st(n_a), st(n_b), st(nb_b), st(nb_c), st(nb_d)


def kernel(x_prompt, x_sample, state_hgrn, state_gdn, state_gdn_conv, state_conf_conv, state_pool, p_prompt, p_sample, norm_mix, w_in, hgrn_lb_logits, hgrn_norm, gdn_conv_w, gdn_a_log, gdn_dt_bias, gdn_norm, conf_dw_w, conf_dw_b, conf_ln_g, conf_ln_b, pool_w, pool_scale, w_out, norm_ffn, peer_wq, peer_keys, peer_u, peer_v, norm_ple, ple_gate_w, ple_proj_w, norm_final):
    sm = jax.nn.softmax(hgrn_lb_logits.astype(F32), axis=0)
    lbs = jnp.cumsum(sm, axis=0) - sm[0:1]
    layers = [_prep_layer(i, w_in, lbs, hgrn_norm, gdn_conv_w, gdn_a_log, gdn_dt_bias, gdn_norm, conf_dw_w,
                          conf_dw_b, conf_ln_g, conf_ln_b, pool_w, pool_scale, w_out, peer_wq, peer_keys, peer_u,
                          peer_v, ple_gate_w, ple_proj_w) for i in range(DEPTH)]
    norms = dict(mix=norm_mix, ffn=norm_ffn, ple=norm_ple)
    bp = x_prompt.shape[0]
    z = lambda *s: jnp.zeros((DEPTH, bp) + s, F32)
    yp, hp, gp, gcp, ccp, pp = _trunk(x_prompt, p_prompt, 0, z(HEADS, DH, DH), z(HEADS, DH, DH),
                                      z(B_CONV - 1, 3 * GW), z(C_CONV - 1, GW), z(POOL_BUF, GW),
                                      layers, norms, norm_final)
    ys, hs, gs, gcs, ccs, ps = _trunk(x_sample, p_sample, PAST_LEN, state_hgrn, state_gdn, state_gdn_conv,
                                      state_conf_conv, state_pool, layers, norms, norm_final)
    return (yp, ys, hp, gp, gcp, ccp, pp, hs, gs, gcs, ccs, ps)
```

```python
import functools

import numpy as np
import jax
import jax.numpy as jnp
from jax import lax
from jax.experimental import pallas as pl
from jax.experimental.pallas import tpu as pltpu

F32 = jnp.float32
BF16 = jnp.bfloat16

D_MODEL = 1024
DEPTH = 4
PAST_LEN = 16384
EPS = 1e-6
NEG_BIG = -1e30
D_PLE = 256
HEADS = 4
DH = 64
GW = 256
B_CONV = 4
C_CONV = 31
POOL_WINDOWS = (2, 4, 8, 16)
POOL_BUF = 15
PEER_HEADS = 8
PEER_NKEYS = 128
PEER_HALF = 64
PEER_TOPK = 16

ZB_AQ, ZB_AF, ZB_AI, ZB_AG, ZB_BQ, ZB_BK, ZB_BV, ZB_BG, ZB_CA, ZB_CB, ZB_D, ZB_BB, ZB_BA = range(13)
ZW = 13 * GW

LANES = 128
SOLVE_BLOCK = 16
VMEM_LIMIT = 56 * 1024 * 1024


def _cparams(sem, flags=None):
    return pltpu.CompilerParams(dimension_semantics=sem, vmem_limit_bytes=VMEM_LIMIT, flags=flags)


def _bdot(a, b):
    return jnp.dot(a.astype(BF16), b.astype(BF16), preferred_element_type=F32)


def _bdot_nt(a, b):
    return lax.dot_general(a.astype(BF16), b.astype(BF16), (((1,), (1,)), ((), ())),
                           preferred_element_type=F32)


def _split3(a):
    a1 = a.astype(BF16)
    r1 = a - a1.astype(F32)
    a2 = r1.astype(BF16)
    a3 = (r1 - a2.astype(F32)).astype(BF16)
    return a1, a2, a3


def _xdot(a, m01):
    a1, a2, a3 = _split3(a)
    m = m01.astype(BF16)
    return (jnp.dot(a1, m, preferred_element_type=F32) + jnp.dot(a2, m, preferred_element_type=F32)
            + jnp.dot(a3, m, preferred_element_type=F32))


def _xdot_left(m01, a):
    a1, a2, a3 = _split3(a)
    m = m01.astype(BF16)
    return (jnp.dot(m, a1, preferred_element_type=F32) + jnp.dot(m, a2, preferred_element_type=F32)
            + jnp.dot(m, a3, preferred_element_type=F32))


def _dot2(a, b):
    ah = a.astype(BF16)
    al = (a - ah.astype(F32)).astype(BF16)
    bh = b.astype(BF16)
    bl = (b - bh.astype(F32)).astype(BF16)
    return (jnp.dot(ah, bh, preferred_element_type=F32) + jnp.dot(ah, bl, preferred_element_type=F32)
            + jnp.dot(al, bh, preferred_element_type=F32))


def _xtranspose(a, eye):
    a1, a2, a3 = _split3(a)
    e = eye.astype(BF16)
    dn = (((1,), (1,)), ((), ()))
    return (lax.dot_general(e, a1, dn, preferred_element_type=F32)
            + lax.dot_general(e, a2, dn, preferred_element_type=F32)
            + lax.dot_general(e, a3, dn, preferred_element_type=F32))


def _pack_kernel(x_ref, o_ref, *, transpose):
    x = x_ref[...]
    if transpose:
        x = x.T
    o_ref[...] = pltpu.bitcast(x.astype(BF16), jnp.uint32)


def _pack_rows(w, transpose=False):
    r, c = w.shape
    rb = min(r, 1024)
    if transpose:
        out_shape, out_spec = (c // 2, r), pl.BlockSpec((c // 2, rb), lambda i: (0, i))
    else:
        out_shape, out_spec = (r // 2, c), pl.BlockSpec((rb // 2, c), lambda i: (i, 0))
    return pl.pallas_call(
        functools.partial(_pack_kernel, transpose=transpose),
        out_shape=jax.ShapeDtypeStruct(out_shape, jnp.uint32),
        grid=(r // rb,),
        in_specs=[pl.BlockSpec((rb, c), lambda i: (i, 0))],
        out_specs=out_spec,
        compiler_params=_cparams(("parallel",)),
        name="pack_t" if transpose else "pack",
    )(w)


def _rows_bf16(words):
    return pltpu.bitcast(words, BF16)


def _iota(shape, axis):
    return lax.broadcasted_iota(jnp.int32, shape, axis)


def _tri_incl(c):
    return jnp.where(_iota((c, c), 1) <= _iota((c, c), 0), 1.0, 0.0).astype(F32)


def _eye(n):
    return jnp.where(_iota((n, n), 0) == _iota((n, n), 1), 1.0, 0.0).astype(F32)


def _block_ones(n, blk):
    return jnp.where(_iota((n, n), 0) // blk == _iota((n, n), 1) // blk, 1.0, 0.0).astype(F32)


def _rms(x, g):
    return x * lax.rsqrt(jnp.mean(x * x, axis=-1, keepdims=True) + EPS) * g


def _sigmoid(x):
    return jax.nn.sigmoid(x)


def _silu(x):
    return x * jax.nn.sigmoid(x)


def _inproj_kernel(x_ref, g_ref, w_ref, o_ref):
    xn = _rms(x_ref[...], g_ref[...])
    o_ref[...] = jnp.dot(xn.astype(BF16), _rows_bf16(w_ref[...]), preferred_element_type=F32)


def _inproj(h, g, w_bf16, tm):
    t = h.shape[0]
    return pl.pallas_call(
        _inproj_kernel,
        out_shape=jax.ShapeDtypeStruct((t, ZW), F32),
        grid=(t // tm,),
        in_specs=[pl.BlockSpec((tm, D_MODEL), lambda i: (i, 0)),
                  pl.BlockSpec((1, D_MODEL), lambda i: (0, 0)),
                  pl.BlockSpec((D_MODEL // 2, ZW), lambda i: (0, 0))],
        out_specs=pl.BlockSpec((tm, ZW), lambda i: (i, 0)),
        compiler_params=_cparams(("parallel",)),
        name="inproj",
    )(h, g, w_bf16)


def _outproj_kernel(h_ref, a_ref, b_ref, c_ref, d_ref, w_ref, o_ref):
    acc = h_ref[...]
    for i, r in enumerate((a_ref, b_ref, c_ref, d_ref)):
        acc = acc + jnp.dot(r[...].astype(BF16), _rows_bf16(w_ref[i * GW // 2:(i + 1) * GW // 2, :]),
                            preferred_element_type=F32)
    o_ref[...] = acc


def _outproj(h, oa, ob, oc, od, w_bf16, tm):
    t = h.shape[0]
    tok = lambda w: pl.BlockSpec((tm, w), lambda i: (i, 0))
    return pl.pallas_call(
        _outproj_kernel,
        out_shape=jax.ShapeDtypeStruct((t, D_MODEL), F32),
        grid=(t // tm,),
        in_specs=[tok(D_MODEL), tok(GW), tok(GW), tok(GW), tok(GW),
                  pl.BlockSpec((D_MODEL // 2, D_MODEL), lambda i: (0, 0))],
        out_specs=tok(D_MODEL),
        compiler_params=_cparams(("parallel",)),
        name="outproj",
    )(h, oa, ob, oc, od, w_bf16)


def _ple_kernel(h_ref, p_ref, g_ref, wg_ref, wp_ref, gf_ref, o_ref, *, final):
    h = h_ref[...]
    xn = _rms(h, g_ref[...])
    gate = _sigmoid(jnp.dot(xn.astype(BF16), _rows_bf16(wg_ref[...]), preferred_element_type=F32))
    proj = jnp.dot(p_ref[...].astype(BF16), _rows_bf16(wp_ref[...]), preferred_element_type=F32)
    out = h + gate * proj
    if final:
        out = _rms(out, gf_ref[...])
    o_ref[...] = out


def _ple(h, p, g, wg_bf16, wp_bf16, gfinal, tm, final):
    t = h.shape[0]
    tok = lambda w: pl.BlockSpec((tm, w), lambda i: (i, 0))
    full = lambda r, c: pl.BlockSpec((r, c), lambda i: (0, 0))
    return pl.pallas_call(
        functools.partial(_ple_kernel, final=final),
        out_shape=jax.ShapeDtypeStruct((t, D_MODEL), F32),
        grid=(t // tm,),
        in_specs=[tok(D_MODEL), tok(D_PLE), full(1, D_MODEL), full(D_MODEL // 2, D_MODEL), full(D_PLE // 2, D_MODEL),
                  full(1, D_MODEL)],
        out_specs=tok(D_MODEL),
        compiler_params=_cparams(("parallel",)),
        name="ple_final" if final else "ple",
    )(h, p, g, wg_bf16, wp_bf16, gfinal)


def _hgrn_kernel(z_ref, lb_ref, nw_ref, s0_ref, o_ref, sf_ref, st_scr, la_scr, q_scr, k_scr, v_scr, p_scr, *, nb, c):
    ci = pl.program_id(1)

    @pl.when(ci == 0)
    def _():
        st_scr[...] = s0_ref[...]

    lb = lb_ref[...]
    bones = _block_ones(GW, DH)
    tri = _tri_incl(c)
    rows = _iota((c, GW), 0)
    seqs = range(nb)
    for b in seqs:
        zf = z_ref[b, :, 1 * GW:2 * GW]
        q_scr[b] = _silu(z_ref[b, :, 0 * GW:1 * GW])
        k_scr[b] = (1.0 - lb) * _sigmoid(-zf)
        v_scr[b] = z_ref[b, :, 2 * GW:3 * GW]
        f = lb + (1.0 - lb) * _sigmoid(zf)
        la_scr[b] = _xdot_left(tri, jnp.log(f))

    def build(s, carry):
        for b in seqs:
            la_s = la_scr[b, pl.ds(s, 1), :]
            k_s = k_scr[b, pl.ds(s, 1), :]
            dec = jnp.exp(jnp.minimum(la_scr[b] - la_s, 0.0))
            p_scr[b, pl.ds(pl.multiple_of(s * c, c), c), :] = jnp.where(rows >= s, q_scr[b] * k_s * dec, 0.0)
        return carry

    lax.fori_loop(0, c, build, 0, unroll=min(4, c))
    for b in seqs:
        p_scr[b] = jnp.dot(p_scr[b].astype(BF16), bones.astype(BF16), preferred_element_type=F32)

    def consume(s, accs):
        return tuple(acc + p_scr[b, pl.ds(pl.multiple_of(s * c, c), c), :] * v_scr[b, pl.ds(s, 1), :]
                     for b, acc in zip(seqs, accs))

    os_ = lax.fori_loop(0, c, consume, tuple(jnp.zeros((c, GW), F32) for _ in seqs), unroll=min(4, c))
    sts = [st_scr[b] for b in seqs]
    las = [la_scr[b] for b in seqs]
    os_ = [o + _bdot_nt(q_scr[b] * jnp.exp(la), st) for b, o, la, st in zip(seqs, os_, las, sts)]
    upds = []
    for b, la in zip(seqs, las):
        k_dec = k_scr[b] * jnp.exp(la[c - 1:c, :] - la)
        upds.append(lax.dot_general(v_scr[b].astype(BF16), k_dec.astype(BF16), (((0,), (0,)), ((), ())),
                                    preferred_element_type=F32))
    for b, la, st, upd in zip(seqs, las, sts, upds):
        st_scr[b] = st * jnp.exp(la[c - 1:c, :]) + upd * bones
    mss = [_xdot(o * o, bones) * (1.0 / DH) for o in os_]
    for b, o, ms in zip(seqs, os_, mss):
        o_ref[b] = o * lax.rsqrt(ms + EPS) * nw_ref[...] * _sigmoid(z_ref[b, :, 3 * GW:4 * GW])

    @pl.when(ci == pl.num_programs(1) - 1)
    def _():
        sf_ref[...] = st_scr[...]


def _hgrn(z3, lb, nw, s0t, nb, c):
    bsz, l, _ = z3.shape
    return pl.pallas_call(
        functools.partial(_hgrn_kernel, nb=nb, c=c),
        out_shape=(jax.ShapeDtypeStruct((bsz, l, GW), F32), jax.ShapeDtypeStruct((bsz, GW, GW), F32)),
        grid=(bsz // nb, l // c),
        in_specs=[pl.BlockSpec((nb, c, 4 * GW), lambda b, i: (b, i, 0)),
                  pl.BlockSpec((1, GW), lambda b, i: (0, 0)),
                  pl.BlockSpec((1, GW), lambda b, i: (0, 0)),
                  pl.BlockSpec((nb, GW, GW), lambda b, i: (b, 0, 0))],
        out_specs=(pl.BlockSpec((nb, c, GW), lambda b, i: (b, i, 0)),
                   pl.BlockSpec((nb, GW, GW), lambda b, i: (b, 0, 0))),
        scratch_shapes=[pltpu.VMEM((nb, GW, GW), F32)] + [pltpu.VMEM((nb, c, GW), F32)] * 4
                       + [pltpu.VMEM((nb, c * c, GW), F32)],
        compiler_params=_cparams(("parallel", "arbitrary")),
        name="hgrn",
    )(z3, lb, nw, s0t)


def _gdn_kernel(zq_ref, zk_ref, zv_ref, zg_ref, zb_ref, za_ref, cw_ref, alog_ref, dtb_ref, nw_ref, buf_ref, s0_ref,
                o_ref, sf_ref, s_scr, xp_scr, *, nb, c):
    ci = pl.program_id(1)

    @pl.when(ci == 0)
    def _():
        s_scr[...] = s0_ref[...]
        xp_scr[:, 0:8, :] = buf_ref[...]

    bones = _block_ones(GW, DH)
    tri = _tri_incl(c)
    eye_c = _eye(c)
    eye_h = _eye(DH)
    tt = _iota((c, c), 0)
    ss = _iota((c, c), 1)
    causal = tt >= ss
    strict = tt > ss
    items = []
    for b in range(nb):
        for j, r in enumerate((zq_ref, zk_ref, zv_ref)):
            xp_scr[b, 8:8 + c, j * GW:(j + 1) * GW] = r[b]
        conv = cw_ref[3:4, :] * xp_scr[b, 8:8 + c, :]
        for j in range(B_CONV - 1):
            conv = conv + cw_ref[j:j + 1, :] * xp_scr[b, 5 + j:5 + j + c, :]
        xp_scr[b, 0:8, :] = xp_scr[b, c:c + 8, :]
        qkv = _silu(conv)
        q = qkv[:, 0:GW]
        k = qkv[:, GW:2 * GW]
        v = qkv[:, 2 * GW:3 * GW]
        q = q * lax.rsqrt(_xdot(q * q, bones) + EPS) * (DH ** -0.5)
        k = k * lax.rsqrt(_xdot(k * k, bones) + EPS)
        beta = _sigmoid(zb_ref[b])
        loga = -jnp.exp(alog_ref[...]) * jax.nn.softplus(za_ref[b] + dtb_ref[...])
        g = _xdot_left(tri, loga)
        for h in range(HEADS):
            hs = slice(h * DH, (h + 1) * DH)
            items.append(dict(b=b, h=h, hs=hs, q=q[:, hs], k=k[:, hs], v=v[:, hs], beta=beta[:, hs], g=g[:, hs]))
    for it in items:
        it['gcol'] = it['g'][:, 0:c]
    for it in items:
        it['grow'] = _xtranspose(it['gcol'], eye_c)
    for it in items:
        it['kk'] = _bdot_nt(it['k'], it['k'])
        it['qk'] = _bdot_nt(it['q'], it['k'])
    for it in items:
        it['dec'] = jnp.exp(jnp.where(causal, it['gcol'] - it['grow'], NEG_BIG))
        it['a'] = jnp.where(strict, it['beta'][:, 0:c] * it['kk'] * it['dec'], 0.0)
        it['r'] = jnp.concatenate([it['beta'] * it['v'], it['beta'] * jnp.exp(it['g']) * it['k']], axis=1)
    blk = min(SOLVE_BLOCK, c)
    for it in items:
        it['solved'] = []
    for j0 in range(0, c, blk):
        rbs = [it['r'][j0:j0 + blk, :] for it in items]
        abs_ = [it['a'][j0:j0 + blk, j0:j0 + blk] for it in items]
        for s in range(blk - 1):
            rbs = [rb - ab[:, s:s + 1] * rb[s:s + 1, :] for rb, ab in zip(rbs, abs_)]
        for it, rb in zip(items, rbs):
            it['solved'].append(rb)
        if j0 + blk < c:
            belows = [_dot2(it['a'][j0 + blk:, j0:j0 + blk], rb) for it, rb in zip(items, rbs)]
            for it, below in zip(items, belows):
                it['r'] = jnp.concatenate([it['r'][:j0 + blk, :], it['r'][j0 + blk:, :] - below], axis=0)
    for it in items:
        w = it['solved'][0] if len(it['solved']) == 1 else jnp.concatenate(it['solved'], axis=0)
        it['w1'] = w[:, 0:DH]
        it['w2'] = w[:, DH:2 * DH]
        it['s'] = s_scr[it['b'], it['h']]
    for it in items:
        it['u'] = it['w1'] - _bdot(it['w2'], it['s'])
    for it in items:
        it['o'] = _bdot(it['q'] * jnp.exp(it['g']), it['s']) + _bdot(it['qk'] * it['dec'], it['u'])
    for it in items:
        g_last = it['g'][c - 1:c, :]
        kd = it['k'] * jnp.exp(g_last - it['g'])
        kd_t = lax.dot_general(eye_h.astype(BF16), kd.astype(BF16), (((1,), (1,)), ((), ())),
                               preferred_element_type=F32)
        it['snew'] = jnp.exp(g_last) * it['s'] + _bdot(kd_t, it['u'])
    for it in items:
        s_scr[it['b'], it['h']] = it['snew']
        o = it['o']
        zg = zg_ref[it['b'], :, it['hs']]
        o = o * lax.rsqrt(jnp.mean(o * o, axis=-1, keepdims=True) + EPS) * nw_ref[...] * _silu(zg)
        o_ref[it['b'], :, it['hs']] = o

    @pl.when(ci == pl.num_programs(1) - 1)
    def _():
        sf_ref[...] = s_scr[...]


def _gdn(z3, cw, alog_x, dtb_x, nw, buf8, s0, nb, c):
    bsz, l, _ = z3.shape
    zblk = lambda j: pl.BlockSpec((nb, c, GW), lambda b, i, j=j: (b, i, j))
    full = lambda r, w: pl.BlockSpec((r, w), lambda b, i: (0, 0))
    return pl.pallas_call(
        functools.partial(_gdn_kernel, nb=nb, c=c),
        out_shape=(jax.ShapeDtypeStruct((bsz, l, GW), F32), jax.ShapeDtypeStruct((bsz, HEADS, DH, DH), F32)),
        grid=(bsz // nb, l // c),
        in_specs=[zblk(ZB_BQ), zblk(ZB_BK), zblk(ZB_BV), zblk(ZB_BG), zblk(ZB_BB), zblk(ZB_BA),
                  full(B_CONV, 3 * GW), full(1, GW), full(1, GW), full(1, DH),
                  pl.BlockSpec((nb, 8, 3 * GW), lambda b, i: (b, 0, 0)),
                  pl.BlockSpec((nb, HEADS, DH, DH), lambda b, i: (b, 0, 0, 0))],
        out_specs=(pl.BlockSpec((nb, c, GW), lambda b, i: (b, i, 0)),
                   pl.BlockSpec((nb, HEADS, DH, DH), lambda b, i: (b, 0, 0, 0))),
        scratch_shapes=[pltpu.VMEM((nb, HEADS, DH, DH), F32), pltpu.VMEM((nb, 8 + c, 3 * GW), F32)],
        compiler_params=_cparams(("parallel", "arbitrary")),
        name="gdn",
    )(z3, z3, z3, z3, z3, z3, cw, alog_x, dtb_x, nw, buf8, s0)


CPAD = 32
DPAD = 16


def _convpool_kernel(za_ref, zb_ref, zd_ref, dw_ref, db_ref, lg_ref, lb_ref, pw_ref, ps_ref, cbuf_ref, dbuf_ref,
                     oc_ref, u_ref, od_ref, xc_scr, xd_scr, *, nb, c, pos0):
    ci = pl.program_id(1)

    @pl.when(ci == 0)
    def _():
        xc_scr[:, 0:CPAD, :] = cbuf_ref[...]
        xd_scr[:, 0:DPAD, :] = dbuf_ref[...]

    lane = _iota((c, GW), 1)
    wl = jnp.where(lane < 64, 2.0, jnp.where(lane < 128, 4.0, jnp.where(lane < 192, 8.0, 16.0)))
    pos = (_iota((c, GW), 0) + (ci * c + pos0 + 1)).astype(F32)
    cnt = jnp.minimum(wl, pos)
    for b in range(nb):
        u = za_ref[b] * _sigmoid(zb_ref[b])
        u_ref[b] = u
        xc_scr[b, CPAD:CPAD + c, :] = u
        y = dw_ref[C_CONV - 1:C_CONV, :] * u
        for j in range(C_CONV - 1):
            y = y + dw_ref[j:j + 1, :] * xc_scr[b, 2 + j:2 + j + c, :]
        xc_scr[b, 0:CPAD, :] = xc_scr[b, c:c + CPAD, :]
        y = y + db_ref[...]
        mu = jnp.mean(y, axis=-1, keepdims=True)
        yc = y - mu
        var = jnp.mean(yc * yc, axis=-1, keepdims=True)
        oc_ref[b] = _silu(yc * lax.rsqrt(var + EPS) * lg_ref[...] + lb_ref[...])
        x = zd_ref[b]
        xd_scr[b, DPAD:DPAD + c, :] = x
        acc = x
        sums = {}
        for i in range(1, 16):
            acc = acc + xd_scr[b, DPAD - i:DPAD - i + c, :]
            if i + 1 in POOL_WINDOWS:
                sums[i + 1] = acc
        xd_scr[b, 0:DPAD, :] = xd_scr[b, c:c + DPAD, :]
        ssel = jnp.where(lane < 64, sums[2], jnp.where(lane < 128, sums[4], jnp.where(lane < 192, sums[8], sums[16])))
        diff = ssel / cnt - x
        od_ref[b] = _bdot(diff, pw_ref[...]) * ps_ref[...]


def _convpool(z3, dw, db, lg, lb, pwbd, ps, cbuf, dbuf, nb, c, pos0):
    bsz, l, _ = z3.shape
    zblk = lambda j: pl.BlockSpec((nb, c, GW), lambda b, i, j=j: (b, i, j))
    full = lambda r, w: pl.BlockSpec((r, w), lambda b, i: (0, 0))
    oblk = pl.BlockSpec((nb, c, GW), lambda b, i: (b, i, 0))
    osh = jax.ShapeDtypeStruct((bsz, l, GW), F32)
    return pl.pallas_call(
        functools.partial(_convpool_kernel, nb=nb, c=c, pos0=pos0),
        out_shape=(osh, osh, osh),
        grid=(bsz // nb, l // c),
        in_specs=[zblk(ZB_CA), zblk(ZB_CB), zblk(ZB_D), full(C_CONV, GW), full(1, GW), full(1, GW), full(1, GW),
                  full(GW, GW), full(1, GW),
                  pl.BlockSpec((nb, CPAD, GW), lambda b, i: (b, 0, 0)),
                  pl.BlockSpec((nb, DPAD, GW), lambda b, i: (b, 0, 0))],
        out_specs=(oblk, oblk, oblk),
        scratch_shapes=[pltpu.VMEM((nb, CPAD + c, GW), F32), pltpu.VMEM((nb, DPAD + c, GW), F32)],
        compiler_params=_cparams(("parallel", "arbitrary")),
        name="convpool",
    )(z3, z3, z3, dw, db, lg, lb, pwbd, ps, cbuf, dbuf)


NKH = 2 * PEER_HEADS * PEER_NKEYS
_CANDS = [(a, b) for a in range(PEER_TOPK) for b in range(PEER_TOPK) if (a + 1) * (b + 1) <= PEER_TOPK]


def _oddeven_merge_sort_pairs(n):
    pairs = []

    def merge(lo, hi, r):
        step = r * 2
        if step < hi - lo:
            merge(lo, hi, step)
            merge(lo + r, hi, step)
            pairs.extend((i, i + r) for i in range(lo + r, hi - r, step))
        else:
            pairs.append((lo, lo + r))

    def sort(lo, hi):
        if hi - lo >= 1:
            mid = lo + (hi - lo) // 2
            sort(lo, mid)
            sort(mid + 1, hi)
            merge(lo, hi, 1)

    sort(0, n - 1)
    return pairs


_SORT16 = _oddeven_merge_sort_pairs(PEER_NKEYS // 8)


def _peer1_kernel(h_ref, g_ref, wq_ref, kbd_ref, xn_ref, e1_ref, c_ref, e2_ref, r2_ref, s_scr, sv_scr, d_scr, zi_scr,
                  *, nlt):
    xn = _rms(h_ref[...], g_ref[...]).T.astype(BF16)
    xn_ref[...] = pltpu.bitcast(xn, jnp.uint32)
    qt = jnp.dot(_rows_bf16(wq_ref[...]), xn, preferred_element_type=F32).astype(BF16)
    for hp in range(2 * PEER_HEADS):
        sc = jnp.dot(kbd_ref[hp], qt[hp * PEER_HALF:(hp + 1) * PEER_HALF, :], preferred_element_type=F32)
        for lt in range(nlt):
            s_scr[lt, hp * PEER_NKEYS:(hp + 1) * PEER_NKEYS, :] = sc[:, lt * LANES:(lt + 1) * LANES]

    def lane_tile(lt, carry):
        for h in range(PEER_HEADS):
            for p in range(2):
                s = s_scr[lt, (2 * h + p) * PEER_NKEYS:(2 * h + p + 1) * PEER_NKEYS, :]
                lists = [s[8 * i:8 * (i + 1), :] for i in range(PEER_NKEYS // 8)]
                for i, j in _SORT16:
                    lists[i], lists[j] = jnp.maximum(lists[i], lists[j]), jnp.minimum(lists[i], lists[j])
                svs = []
                for a in range(PEER_TOPK):
                    m = jnp.max(lists[0], axis=0, keepdims=True)
                    sv_scr[p, a, h:h + 1, :] = m
                    svs.append(m)
                    popped = lists[0] == m
                    for i in range(PEER_TOPK - 1 - a):
                        lists[i] = jnp.where(popped, lists[i + 1], lists[i])
                if p == 1:
                    rank = jnp.full(s.shape, float(PEER_TOPK), F32)
                    for a in reversed(range(PEER_TOPK)):
                        rank = jnp.where(s >= svs[a], float(a), rank)
                    r2_ref[lt, h] = pltpu.bitcast(rank.astype(BF16), jnp.uint32)
                    e2_ref[lt, h] = pltpu.bitcast(jnp.exp(s - svs[0]).astype(BF16), jnp.uint32)
        sv1 = [sv_scr[0, a] for a in range(PEER_TOPK)]
        sv2 = [sv_scr[1, a] for a in range(PEER_TOPK)]
        vals = [sv1[a] + sv2[b] for a, b in _CANDS]
        n = len(_CANDS)
        before = [jnp.zeros((PEER_HEADS, LANES), F32) for _ in range(n)]
        for i in range(n):
            ai, bi = _CANDS[i]
            for j in range(i + 1, n):
                aj, bj = _CANDS[j]
                if ai <= aj and bi <= bj:
                    before[j] = before[j] + 1.0
                else:
                    t = jnp.where(vals[i] >= vals[j], 1.0, 0.0)
                    before[j] = before[j] + t
                    before[i] = before[i] + (1.0 - t)
        ex1 = [jnp.exp(sv1[a] - sv1[0]) for a in range(PEER_TOPK)]
        ex2 = [jnp.exp(sv2[b] - sv2[0]) for b in range(PEER_TOPK)]
        cnt = [jnp.zeros((PEER_HEADS, LANES), F32) for _ in range(PEER_TOPK)]
        zsum = jnp.zeros((PEER_HEADS, LANES), F32)
        for i, (a, b) in enumerate(_CANDS):
            sel = jnp.where(before[i] < float(PEER_TOPK), 1.0, 0.0)
            cnt[a] = cnt[a] + sel
            zsum = zsum + sel * ex1[a] * ex2[b]
        for a in range(PEER_TOPK):
            d_scr[a] = cnt[a]
        zi_scr[...] = 1.0 / zsum
        for h in range(PEER_HEADS):
            s = s_scr[lt, 2 * h * PEER_NKEYS:(2 * h + 1) * PEER_NKEYS, :]
            cc = jnp.zeros(s.shape, F32)
            for a in reversed(range(PEER_TOPK)):
                cc = jnp.where(s >= sv_scr[0, a, h:h + 1, :], d_scr[a, h:h + 1, :], cc)
            c_ref[lt, h] = cc
            e1_ref[lt, h] = jnp.exp(s - sv_scr[0, 0, h:h + 1, :]) * zi_scr[h:h + 1, :]
        return carry

    lax.fori_loop(0, nlt, lane_tile, 0)


def _peer1(h, g, wq_bf16, kbd_bf16, tt):
    t = h.shape[0]
    nlt = tt // LANES
    gsh = jax.ShapeDtypeStruct((t // LANES, PEER_HEADS, PEER_NKEYS, LANES), F32)
    gblk = pl.BlockSpec((nlt, PEER_HEADS, PEER_NKEYS, LANES), lambda i: (i, 0, 0, 0))
    psh = jax.ShapeDtypeStruct((t // LANES, PEER_HEADS, PEER_NKEYS // 2, LANES), jnp.uint32)
    pblk = pl.BlockSpec((nlt, PEER_HEADS, PEER_NKEYS // 2, LANES), lambda i: (i, 0, 0, 0))
    return pl.pallas_call(
        functools.partial(_peer1_kernel, nlt=nlt),
        out_shape=(jax.ShapeDtypeStruct((D_MODEL // 2, t), jnp.uint32), gsh, gsh, psh, psh),
        grid=(t // tt,),
        in_specs=[pl.BlockSpec((tt, D_MODEL), lambda i: (i, 0)),
                  pl.BlockSpec((1, D_MODEL), lambda i: (0, 0)),
                  pl.BlockSpec((D_MODEL // 2, D_MODEL), lambda i: (0, 0)),
                  pl.BlockSpec((2 * PEER_HEADS, PEER_NKEYS, PEER_HALF), lambda i: (0, 0, 0))],
        out_specs=(pl.BlockSpec((D_MODEL // 2, tt), lambda i: (0, i)), gblk, gblk, pblk, pblk),
        scratch_shapes=[pltpu.VMEM((nlt, NKH, LANES), F32),
                        pltpu.VMEM((2, PEER_TOPK, PEER_HEADS, LANES), F32),
                        pltpu.VMEM((PEER_TOPK, PEER_HEADS, LANES), F32),
                        pltpu.VMEM((PEER_HEADS, LANES), F32)],
        compiler_params=_cparams(("parallel",)),
        name="peer1",
    )(h, g, wq_bf16, kbd_bf16)


NE1 = 8
ETILE = NE1 * PEER_NKEYS


def _peer2_step(u_ref, xn_ref, vt_ref, e1_ref, c_ref, e2_ref, r2_ref, acc_scr, act_w, act_r, w_w, w_r, nlt,
                do_value=True, do_gate=True, do_act=True):
    tile = (PEER_NKEYS, LANES)
    zero = jnp.zeros(tile, BF16)
    grp = 2
    nmb = 4
    mblk = D_MODEL // nmb

    def value_rows(m):
        rs = slice(m * mblk, (m + 1) * mblk)
        vrows = _rows_bf16(vt_ref[m * mblk // 2:(m + 1) * mblk // 2, :])
        acc_scr[rs, :] += jnp.dot(vrows, _rows_bf16(w_r[...]), preferred_element_type=F32)

    def act_rows(m):
        urows = _rows_bf16(u_ref[m * mblk // 2:(m + 1) * mblk // 2, :])
        a = jnp.dot(urows, _rows_bf16(xn_ref[...]), preferred_element_type=F32).astype(BF16)
        a = 0.5 * a * (1.0 + lax.erf(a * 0.7071067811865476))
        act_w[m * mblk // 2:(m + 1) * mblk // 2, :] = pltpu.bitcast(a, jnp.uint32)

    def gate_group(lt, g0):
        ls = slice(lt * LANES, (lt + 1) * LANES)
        gates = [zero] * grp
        for h in range(PEER_HEADS):
            rank = pltpu.bitcast(r2_ref[lt, h], BF16)
            wkey = pltpu.bitcast(e2_ref[lt, h], BF16)
            for j in range(grp):
                i1 = g0 + j
                cnt = jnp.broadcast_to(c_ref[lt, h, i1:i1 + 1, :], tile).astype(BF16)
                wgt = jnp.broadcast_to(e1_ref[lt, h, i1:i1 + 1, :], tile).astype(BF16)
                gates[j] = gates[j] + jnp.where(rank < cnt, wkey, zero) * wgt
        for j in range(grp):
            ps = slice((g0 + j) * PEER_NKEYS // 2, (g0 + j + 1) * PEER_NKEYS // 2)
            w_w[ps, ls] = pltpu.bitcast(gates[j] * pltpu.bitcast(act_r[ps, ls], BF16), jnp.uint32)

    groups = [(lt, g0) for lt in range(nlt) for g0 in range(0, NE1, grp)] if do_gate else []
    mxu_work = [f for m in range(nmb) for f, on in ((functools.partial(value_rows, m), do_value),
                                                  (functools.partial(act_rows, m), do_act)) if on]
    if not mxu_work:
        mxu_work = [lambda: None]
    per = max(1, len(groups) // len(mxu_work))
    gi = 0
    for k, mm in enumerate(mxu_work):
        mm()
        take = len(groups) - gi if k == len(mxu_work) - 1 else per
        for _ in range(take):
            if gi < len(groups):
                gate_group(*groups[gi])
                gi += 1


def _peer2_kernel(h_ref, xn_ref, u_ref, vt_ref, e1_ref, c_ref, e2_ref, r2_ref, o_ref, acc_scr, act0, act1, w0, w1, *,
                  nlt):
    e = pl.program_id(1)

    last = pl.num_programs(1) - 1
    args = (u_ref, xn_ref, vt_ref, e1_ref, c_ref, e2_ref, r2_ref, acc_scr)
    even = (act0, act1, w1, w0)
    odd = (act1, act0, w0, w1)
    steady = jnp.logical_and(e >= 2, e <= last - 2)

    @pl.when(e == 0)
    def _():
        acc_scr[...] = jnp.zeros(acc_scr.shape, F32)
        _peer2_step(*args, *even, nlt, do_value=False, do_gate=False)

    @pl.when(e == 1)
    def _():
        _peer2_step(*args, *odd, nlt, do_value=False)

    @pl.when(jnp.logical_and(steady, e % 2 == 0))
    def _():
        _peer2_step(*args, *even, nlt)

    @pl.when(jnp.logical_and(steady, e % 2 == 1))
    def _():
        _peer2_step(*args, *odd, nlt)

    @pl.when(e == last - 1)
    def _():
        _peer2_step(*args, *even, nlt, do_act=False)

    @pl.when(e == last)
    def _():
        _peer2_step(*args, *odd, nlt, do_act=False, do_gate=False)
        o_ref[...] = h_ref[...] + acc_scr[...].T


def _peer2(h, xn, u_bf16, vt_bf16, e1, cc, e2, r2, tt):
    t = h.shape[0]
    nlt = tt // LANES
    ne = PEER_NKEYS // NE1
    assert ne % 2 == 0
    last = ne - 1
    pfull = pl.BlockSpec((nlt, PEER_HEADS, PEER_NKEYS // 2, LANES), lambda i, e: (i, 0, 0, 0))
    gtile = pl.BlockSpec((nlt, PEER_HEADS, NE1, LANES), lambda i, e: (i, 0, jnp.clip(e - 1, 0, last), 0))
    slot = pltpu.VMEM((ETILE // 2, tt), jnp.uint32)
    return pl.pallas_call(
        functools.partial(_peer2_kernel, nlt=nlt),
        out_shape=jax.ShapeDtypeStruct((t, D_MODEL), F32),
        grid=(t // tt, ne + 2),
        in_specs=[pl.BlockSpec((tt, D_MODEL), lambda i, e: (i, 0)),
                  pl.BlockSpec((D_MODEL // 2, tt), lambda i, e: (0, i)),
                  pl.BlockSpec((ETILE // 2, D_MODEL), lambda i, e: (jnp.minimum(e, last), 0)),
                  pl.BlockSpec((D_MODEL // 2, ETILE), lambda i, e: (0, jnp.clip(e - 2, 0, last))),
                  gtile, gtile, pfull, pfull],
        out_specs=pl.BlockSpec((tt, D_MODEL), lambda i, e: (i, 0)),
        scratch_shapes=[pltpu.VMEM((D_MODEL, tt), F32), slot, slot, slot, slot],
        compiler_params=_cparams(("parallel", "arbitrary")),
        name="peer2",
    )(h, xn, u_bf16, vt_bf16, e1, cc, e2, r2)


def _prep_layer(i, w_in, lbs, hgrn_norm, gdn_conv_w, gdn_a_log, gdn_dt_bias, gdn_norm, conf_dw_w, conf_dw_b,
                conf_ln_g, conf_ln_b, pool_w, pool_scale, w_out, peer_wq, peer_keys, peer_u, peer_v, ple_gate_w,
                ple_proj_w):
    wi = w_in[i]
    hk = HEADS * DH
    c0 = 4 * hk
    qkv = wi[:, c0:c0 + 3 * hk]
    zbg = wi[:, c0 + 3 * hk:c0 + 4 * hk]
    zbb = wi[:, c0 + 4 * hk:c0 + 4 * hk + HEADS]
    zba = wi[:, c0 + 4 * hk + HEADS:c0 + 4 * hk + 2 * HEADS]
    rest = wi[:, c0 + 4 * hk + 2 * HEADS:]
    w_perm = jnp.concatenate([wi[:, :c0], qkv, zbg, rest, jnp.repeat(zbb, DH, axis=1), jnp.repeat(zba, DH, axis=1)],
                             axis=1)
    kbd = peer_keys[i].reshape(2 * PEER_HEADS, PEER_NKEYS, PEER_HALF).astype(BF16)
    pw = pool_w[i]
    eye4 = jnp.eye(4, dtype=F32)
    pwbd = (pw[:, :, None, :] * eye4[:, None, :, None]).reshape(GW, GW)
    return dict(
        w_in=_pack_rows(w_perm), lb=lbs[i][None, :], hgrn_norm=hgrn_norm[i][None, :],
        gdn_conv_w=gdn_conv_w[i], gdn_a_log=jnp.repeat(gdn_a_log[i], DH)[None, :],
        gdn_dt_bias=jnp.repeat(gdn_dt_bias[i], DH)[None, :], gdn_norm=gdn_norm[i][None, :],
        conf_dw_w=conf_dw_w[i], conf_dw_b=conf_dw_b[i][None, :], conf_ln_g=conf_ln_g[i][None, :],
        conf_ln_b=conf_ln_b[i][None, :], pool_w=pwbd, pool_scale=pool_scale[i][None, :],
        w_out=_pack_rows(w_out[i]), peer_wq=_pack_rows(peer_wq[i], transpose=True), kbd=kbd,
        peer_u=_pack_rows(peer_u[i]), peer_vt=_pack_rows(peer_v[i], transpose=True),
        ple_gate_w=_pack_rows(ple_gate_w[i]), ple_proj_w=_pack_rows(ple_proj_w[i]))


def _tail(buf, x, n):
    l = x.shape[1]
    if l >= n:
        return x[:, l - n:]
    return jnp.concatenate([buf[:, l:], x], axis=1)


def _trunk(x, p, pos0, s_a, s_b, c_b, c_c, c_d, layers, norms, norm_final):
    bsz, l, _ = x.shape
    t = bsz * l
    prompt = l >= 64
    c = 64 if prompt else l
    nb = 1 if prompt else 8
    cc = min(256, l)
    tm = min(512, t)
    tt = min(512, t)
    h = x.reshape(t, D_MODEL)
    n_a, n_b, nb_b, nb_c, nb_d = [], [], [], [], []
    for i, lw in enumerate(layers):
        z = _inproj(h, norms['mix'][i][None, :], lw['w_in'], tm)
        z3 = z.reshape(bsz, l, ZW)
        eye4 = jnp.eye(HEADS, dtype=F32)
        s0t = (jnp.swapaxes(s_a[i], 2, 3)[:, :, :, None, :] * eye4[None, :, None, :, None]).reshape(bsz, GW, GW)
        o_a, st = _hgrn(z3, lw['lb'], lw['hgrn_norm'], s0t, min(4, bsz) if prompt else nb, min(c, 32))
        st = st.reshape(bsz, HEADS, DH, HEADS, DH)
        n_a.append(jnp.stack([jnp.swapaxes(st[:, j, :, j, :], 1, 2) for j in range(HEADS)], axis=1))
        buf8 = jnp.pad(c_b[i], ((0, 0), (8 - (B_CONV - 1), 0), (0, 0)))
        o_b, sb = _gdn(z3, lw['gdn_conv_w'], lw['gdn_a_log'], lw['gdn_dt_bias'], lw['gdn_norm'], buf8, s_b[i],
                       min(4, bsz) if prompt else nb, c)
        n_b.append(sb)
        nb_b.append(_tail(c_b[i], z3[:, :, ZB_BQ * GW:(ZB_BV + 1) * GW], B_CONV - 1))
        cbuf = jnp.pad(c_c[i], ((0, 0), (CPAD - (C_CONV - 1), 0), (0, 0)))
        dbuf = jnp.pad(c_d[i], ((0, 0), (DPAD - POOL_BUF, 0), (0, 0)))
        o_c, u_c, o_d = _convpool(z3, lw['conf_dw_w'], lw['conf_dw_b'], lw['conf_ln_g'], lw['conf_ln_b'],
                                  lw['pool_w'], lw['pool_scale'], cbuf, dbuf, nb, cc, pos0)
        nb_c.append(_tail(c_c[i], u_c, C_CONV - 1))
        nb_d.append(_tail(c_d[i], z3[:, :, ZB_D * GW:(ZB_D + 1) * GW], POOL_BUF))
        h = _outproj(h, o_a.reshape(t, GW), o_b.reshape(t, GW), o_c.reshape(t, GW), o_d.reshape(t, GW),
                     lw['w_out'], tm)
        xn, e1, cnt, e2, r2 = _peer1(h, norms['ffn'][i][None, :], lw['peer_wq'], lw['kbd'], tt)
        h = _peer2(h, xn, lw['peer_u'], lw['peer_vt'], e1, cnt, e2, r2, min(1024, t))
        h = _ple(h, p[i].reshape(t, D_PLE), norms['ple'][i][None, :], lw['ple_gate_w'], lw['ple_proj_w'],
                 norm_final[None, :], tm, final=(i == len(layers) - 1))
    st = lambda lst: jnp.stack(lst, axis=0)
    return h.reshape(bsz, l, D_MODEL), st(n_a), st(n_b), st(nb_b), st(nb_c), st(nb_d)


def kernel(x_prompt, x_sample, state_hgrn, state_gdn, state_gdn_conv, state_conf_conv, state_pool, p_prompt, p_sample, norm_mix, w_in, hgrn_lb_logits, hgrn_norm, gdn_conv_w, gdn_a_log, gdn_dt_bias, gdn_norm, conf_dw_w, conf_dw_b, conf_ln_g, conf_ln_b, pool_w, pool_scale, w_out, norm_ffn, peer_wq, peer_keys, peer_u, peer_v, norm_ple, ple_gate_w, ple_proj_w, norm_final):
    sm = jax.nn.softmax(hgrn_lb_logits.astype(F32), axis=0)
    lbs = jnp.cumsum(sm, axis=0) - sm[0:1]
    layers = [_prep_layer(i, w_in, lbs, hgrn_norm, gdn_conv_w, gdn_a_log, gdn_dt_bias, gdn_norm, conf_dw_w,
                          conf_dw_b, conf_ln_g, conf_ln_b, pool_w, pool_scale, w_out, peer_wq, peer_keys, peer_u,
                          peer_v, ple_gate_w, ple_proj_w) for i in range(DEPTH)]
    norms = dict(mix=norm_mix, ffn=norm_ffn, ple=norm_ple)
    bp = x_prompt.shape[0]
    z = lambda *s: jnp.zeros((DEPTH, bp) + s, F32)
    yp, hp, gp, gcp, ccp, pp = _trunk(x_prompt, p_prompt, 0, z(HEADS, DH, DH), z(HEADS, DH, DH),
                                      z(B_CONV - 1, 3 * GW), z(C_CONV - 1, GW), z(POOL_BUF, GW),
                                      layers, norms, norm_final)
    ys, hs, gs, gcs, ccs, ps = _trunk(x_sample, p_sample, PAST_LEN, state_hgrn, state_gdn, state_gdn_conv,
                                      state_conf_conv, state_pool, layers, norms, norm_final)
    return (yp, ys, hp, gp, gcp, ccp, pp, hs, gs, gcs, ccs, ps)
```

```python
import functools

import numpy as np
import jax
import jax.numpy as jnp
from jax import lax
from jax.experimental import pallas as pl
from jax.experimental.pallas import tpu as pltpu

F32 = jnp.float32
BF16 = jnp.bfloat16

D_MODEL = 1024
DEPTH = 4
PAST_LEN = 16384
EPS = 1e-6
NEG_BIG = -1e30
D_PLE = 256
HEADS = 4
DH = 64
GW = 256
B_CONV = 4
C_CONV = 31
POOL_WINDOWS = (2, 4, 8, 16)
POOL_BUF = 15
PEER_HEADS = 8
PEER_NKEYS = 128
PEER_HALF = 64
PEER_TOPK = 16

ZB_AQ, ZB_AF, ZB_AI, ZB_AG, ZB_BQ, ZB_BK, ZB_BV, ZB_BG, ZB_CA, ZB_CB, ZB_D, ZB_BB, ZB_BA = range(13)
ZW = 13 * GW

LANES = 128
SOLVE_BLOCK = 16
VMEM_LIMIT = 56 * 1024 * 1024


def _cparams(sem, flags=None):
    return pltpu.CompilerParams(dimension_semantics=sem, vmem_limit_bytes=VMEM_LIMIT, flags=flags)


def _bdot(a, b):
    return jnp.dot(a.astype(BF16), b.astype(BF16), preferred_element_type=F32)


def _bdot_nt(a, b):
    return lax.dot_general(a.astype(BF16), b.astype(BF16), (((1,), (1,)), ((), ())),
                           preferred_element_type=F32)


def _split3(a):
    a1 = a.astype(BF16)
    r1 = a - a1.astype(F32)
    a2 = r1.astype(BF16)
    a3 = (r1 - a2.astype(F32)).astype(BF16)
    return a1, a2, a3


def _xdot(a, m01):
    a1, a2, a3 = _split3(a)
    m = m01.astype(BF16)
    return (jnp.dot(a1, m, preferred_element_type=F32) + jnp.dot(a2, m, preferred_element_type=F32)
            + jnp.dot(a3, m, preferred_element_type=F32))


def _xdot_left(m01, a):
    a1, a2, a3 = _split3(a)
    m = m01.astype(BF16)
    return (jnp.dot(m, a1, preferred_element_type=F32) + jnp.dot(m, a2, preferred_element_type=F32)
            + jnp.dot(m, a3, preferred_element_type=F32))


def _dot2(a, b):
    ah = a.astype(BF16)
    al = (a - ah.astype(F32)).astype(BF16)
    bh = b.astype(BF16)
    bl = (b - bh.astype(F32)).astype(BF16)
    return (jnp.dot(ah, bh, preferred_element_type=F32) + jnp.dot(ah, bl, preferred_element_type=F32)
            + jnp.dot(al, bh, preferred_element_type=F32))


def _xtranspose(a, eye):
    a1, a2, a3 = _split3(a)
    e = eye.astype(BF16)
    dn = (((1,), (1,)), ((), ()))
    return (lax.dot_general(e, a1, dn, preferred_element_type=F32)
            + lax.dot_general(e, a2, dn, preferred_element_type=F32)
            + lax.dot_general(e, a3, dn, preferred_element_type=F32))


def _pack_kernel(x_ref, o_ref, *, transpose):
    x = x_ref[...]
    if transpose:
        x = x.T
    o_ref[...] = pltpu.bitcast(x.astype(BF16), jnp.uint32)


def _pack_rows(w, transpose=False, layer=None):
    r, c = w.shape[-2:]
    rb = min(r, 1024)
    if layer is None:
        in_spec = pl.BlockSpec((rb, c), lambda i: (i, 0))
    else:
        in_spec = pl.BlockSpec((None, rb, c), lambda i: (layer, i, 0))
    if transpose:
        out_shape, out_spec = (c // 2, r), pl.BlockSpec((c // 2, rb), lambda i: (0, i))
    else:
        out_shape, out_spec = (r // 2, c), pl.BlockSpec((rb // 2, c), lambda i: (i, 0))
    return pl.pallas_call(
        functools.partial(_pack_kernel, transpose=transpose),
        out_shape=jax.ShapeDtypeStruct(out_shape, jnp.uint32),
        grid=(r // rb,),
        in_specs=[in_spec],
        out_specs=out_spec,
        compiler_params=_cparams(("parallel",)),
        name="pack_t" if transpose else "pack",
    )(w)


def _rows_bf16(words):
    return pltpu.bitcast(words, BF16)


def _iota(shape, axis):
    return lax.broadcasted_iota(jnp.int32, shape, axis)


def _tri_incl(c):
    return jnp.where(_iota((c, c), 1) <= _iota((c, c), 0), 1.0, 0.0).astype(F32)


def _eye(n):
    return jnp.where(_iota((n, n), 0) == _iota((n, n), 1), 1.0, 0.0).astype(F32)


def _block_ones(n, blk):
    return jnp.where(_iota((n, n), 0) // blk == _iota((n, n), 1) // blk, 1.0, 0.0).astype(F32)


def _rms(x, g):
    return x * lax.rsqrt(jnp.mean(x * x, axis=-1, keepdims=True) + EPS) * g


def _sigmoid(x):
    return jax.nn.sigmoid(x)


def _silu(x):
    return x * jax.nn.sigmoid(x)


def _inproj_kernel(x_ref, g_ref, w_ref, o_ref):
    xn = _rms(x_ref[...], g_ref[...])
    o_ref[...] = jnp.dot(xn.astype(BF16), _rows_bf16(w_ref[...]), preferred_element_type=F32)


def _inproj(h, g, w_bf16, tm):
    t = h.shape[0]
    return pl.pallas_call(
        _inproj_kernel,
        out_shape=jax.ShapeDtypeStruct((t, ZW), F32),
        grid=(t // tm,),
        in_specs=[pl.BlockSpec((tm, D_MODEL), lambda i: (i, 0)),
                  pl.BlockSpec((1, D_MODEL), lambda i: (0, 0)),
                  pl.BlockSpec((D_MODEL // 2, ZW), lambda i: (0, 0))],
        out_specs=pl.BlockSpec((tm, ZW), lambda i: (i, 0)),
        compiler_params=_cparams(("parallel",)),
        name="inproj",
    )(h, g, w_bf16)


def _outproj_kernel(h_ref, a_ref, b_ref, c_ref, d_ref, w_ref, o_ref):
    acc = h_ref[...]
    for i, r in enumerate((a_ref, b_ref, c_ref, d_ref)):
        acc = acc + jnp.dot(r[...].astype(BF16), _rows_bf16(w_ref[i * GW // 2:(i + 1) * GW // 2, :]),
                            preferred_element_type=F32)
    o_ref[...] = acc


def _outproj(h, oa, ob, oc, od, w_bf16, tm):
    t = h.shape[0]
    tok = lambda w: pl.BlockSpec((tm, w), lambda i: (i, 0))
    return pl.pallas_call(
        _outproj_kernel,
        out_shape=jax.ShapeDtypeStruct((t, D_MODEL), F32),
        grid=(t // tm,),
        in_specs=[tok(D_MODEL), tok(GW), tok(GW), tok(GW), tok(GW),
                  pl.BlockSpec((D_MODEL // 2, D_MODEL), lambda i: (0, 0))],
        out_specs=tok(D_MODEL),
        compiler_params=_cparams(("parallel",)),
        name="outproj",
    )(h, oa, ob, oc, od, w_bf16)


def _ple_kernel(h_ref, p_ref, g_ref, wg_ref, wp_ref, gf_ref, o_ref, *, final):
    h = h_ref[...]
    xn = _rms(h, g_ref[...])
    gate = _sigmoid(jnp.dot(xn.astype(BF16), _rows_bf16(wg_ref[...]), preferred_element_type=F32))
    proj = jnp.dot(p_ref[...].astype(BF16), _rows_bf16(wp_ref[...]), preferred_element_type=F32)
    out = h + gate * proj
    if final:
        out = _rms(out, gf_ref[...])
    o_ref[...] = out


def _ple(h, p, layer, g, wg_bf16, wp_bf16, gfinal, tm, final):
    t = h.shape[0]
    tok = lambda w: pl.BlockSpec((tm, w), lambda i: (i, 0))
    ptok = pl.BlockSpec((None, tm, D_PLE), lambda i: (layer, i, 0))
    full = lambda r, c: pl.BlockSpec((r, c), lambda i: (0, 0))
    return pl.pallas_call(
        functools.partial(_ple_kernel, final=final),
        out_shape=jax.ShapeDtypeStruct((t, D_MODEL), F32),
        grid=(t // tm,),
        in_specs=[tok(D_MODEL), ptok, full(1, D_MODEL), full(D_MODEL // 2, D_MODEL), full(D_PLE // 2, D_MODEL),
                  full(1, D_MODEL)],
        out_specs=tok(D_MODEL),
        compiler_params=_cparams(("parallel",)),
        name="ple_final" if final else "ple",
    )(h, p, g, wg_bf16, wp_bf16, gfinal)


def _hgrn_kernel(z_ref, lb_ref, nw_ref, s0_ref, o_ref, sf_ref, st_scr, la_scr, q_scr, k_scr, v_scr, p_scr, *, nb, c):
    ci = pl.program_id(1)

    @pl.when(ci == 0)
    def _():
        st_scr[...] = s0_ref[...]

    lb = lb_ref[...]
    bones = _block_ones(GW, DH)
    tri = _tri_incl(c)
    rows = _iota((c, GW), 0)
    seqs = range(nb)
    for b in seqs:
        zf = z_ref[b, :, 1 * GW:2 * GW]
        q_scr[b] = _silu(z_ref[b, :, 0 * GW:1 * GW])
        k_scr[b] = (1.0 - lb) * _sigmoid(-zf)
        v_scr[b] = z_ref[b, :, 2 * GW:3 * GW]
        f = lb + (1.0 - lb) * _sigmoid(zf)
        la_scr[b] = _xdot_left(tri, jnp.log(f))

    def build(s, carry):
        for b in seqs:
            la_s = la_scr[b, pl.ds(s, 1), :]
            k_s = k_scr[b, pl.ds(s, 1), :]
            dec = jnp.exp(jnp.minimum(la_scr[b] - la_s, 0.0))
            p_scr[b, pl.ds(pl.multiple_of(s * c, c), c), :] = jnp.where(rows >= s, q_scr[b] * k_s * dec, 0.0)
        return carry

    lax.fori_loop(0, c, build, 0, unroll=min(4, c))
    for b in seqs:
        p_scr[b] = jnp.dot(p_scr[b].astype(BF16), bones.astype(BF16), preferred_element_type=F32)

    def consume(s, accs):
        return tuple(acc + p_scr[b, pl.ds(pl.multiple_of(s * c, c), c), :] * v_scr[b, pl.ds(s, 1), :]
                     for b, acc in zip(seqs, accs))

    os_ = lax.fori_loop(0, c, consume, tuple(jnp.zeros((c, GW), F32) for _ in seqs), unroll=min(4, c))
    sts = [st_scr[b] for b in seqs]
    las = [la_scr[b] for b in seqs]
    os_ = [o + _bdot_nt(q_scr[b] * jnp.exp(la), st) for b, o, la, st in zip(seqs, os_, las, sts)]
    upds = []
    for b, la in zip(seqs, las):
        k_dec = k_scr[b] * jnp.exp(la[c - 1:c, :] - la)
        upds.append(lax.dot_general(v_scr[b].astype(BF16), k_dec.astype(BF16), (((0,), (0,)), ((), ())),
                                    preferred_element_type=F32))
    for b, la, st, upd in zip(seqs, las, sts, upds):
        st_scr[b] = st * jnp.exp(la[c - 1:c, :]) + upd * bones
    mss = [_xdot(o * o, bones) * (1.0 / DH) for o in os_]
    for b, o, ms in zip(seqs, os_, mss):
        o_ref[b] = o * lax.rsqrt(ms + EPS) * nw_ref[...] * _sigmoid(z_ref[b, :, 3 * GW:4 * GW])

    @pl.when(ci == pl.num_programs(1) - 1)
    def _():
        sf_ref[...] = st_scr[...]


def _hgrn(z3, lb, nw, s0t, nb, c):
    bsz, l, _ = z3.shape
    return pl.pallas_call(
        functools.partial(_hgrn_kernel, nb=nb, c=c),
        out_shape=(jax.ShapeDtypeStruct((bsz, l, GW), F32), jax.ShapeDtypeStruct((bsz, GW, GW), F32)),
        grid=(bsz // nb, l // c),
        in_specs=[pl.BlockSpec((nb, c, 4 * GW), lambda b, i: (b, i, 0)),
                  pl.BlockSpec((1, GW), lambda b, i: (0, 0)),
                  pl.BlockSpec((1, GW), lambda b, i: (0, 0)),
                  pl.BlockSpec((nb, GW, GW), lambda b, i: (b, 0, 0))],
        out_specs=(pl.BlockSpec((nb, c, GW), lambda b, i: (b, i, 0)),
                   pl.BlockSpec((nb, GW, GW), lambda b, i: (b, 0, 0))),
        scratch_shapes=[pltpu.VMEM((nb, GW, GW), F32)] + [pltpu.VMEM((nb, c, GW), F32)] * 4
                       + [pltpu.VMEM((nb, c * c, GW), F32)],
        compiler_params=_cparams(("parallel", "arbitrary")),
        name="hgrn",
    )(z3, lb, nw, s0t)


def _gdn_kernel(zq_ref, zk_ref, zv_ref, zg_ref, zb_ref, za_ref, cw_ref, alog_ref, dtb_ref, nw_ref, buf_ref, s0_ref,
                o_ref, sf_ref, s_scr, xp_scr, *, nb, c):
    ci = pl.program_id(1)

    @pl.when(ci == 0)
    def _():
        s_scr[...] = s0_ref[...]
        xp_scr[:, 0:8, :] = buf_ref[...]

    bones = _block_ones(GW, DH)
    tri = _tri_incl(c)
    eye_c = _eye(c)
    eye_h = _eye(DH)
    tt = _iota((c, c), 0)
    ss = _iota((c, c), 1)
    causal = tt >= ss
    strict = tt > ss
    items = []
    for b in range(nb):
        for j, r in enumerate((zq_ref, zk_ref, zv_ref)):
            xp_scr[b, 8:8 + c, j * GW:(j + 1) * GW] = r[b]
        conv = cw_ref[3:4, :] * xp_scr[b, 8:8 + c, :]
        for j in range(B_CONV - 1):
            conv = conv + cw_ref[j:j + 1, :] * xp_scr[b, 5 + j:5 + j + c, :]
        xp_scr[b, 0:8, :] = xp_scr[b, c:c + 8, :]
        qkv = _silu(conv)
        q = qkv[:, 0:GW]
        k = qkv[:, GW:2 * GW]
        v = qkv[:, 2 * GW:3 * GW]
        q = q * lax.rsqrt(_xdot(q * q, bones) + EPS) * (DH ** -0.5)
        k = k * lax.rsqrt(_xdot(k * k, bones) + EPS)
        beta = _sigmoid(zb_ref[b])
        loga = -jnp.exp(alog_ref[...]) * jax.nn.softplus(za_ref[b] + dtb_ref[...])
        g = _xdot_left(tri, loga)
        for h in range(HEADS):
            hs = slice(h * DH, (h + 1) * DH)
            items.append(dict(b=b, h=h, hs=hs, q=q[:, hs], k=k[:, hs], v=v[:, hs], beta=beta[:, hs], g=g[:, hs]))
    for it in items:
        it['gcol'] = it['g'][:, 0:c]
    for it in items:
        it['grow'] = _xtranspose(it['gcol'], eye_c)
    for it in items:
        it['kk'] = _bdot_nt(it['k'], it['k'])
        it['qk'] = _bdot_nt(it['q'], it['k'])
    for it in items:
        it['dec'] = jnp.exp(jnp.where(causal, it['gcol'] - it['grow'], NEG_BIG))
        it['a'] = jnp.where(strict, it['beta'][:, 0:c] * it['kk'] * it['dec'], 0.0)
        it['r'] = jnp.concatenate([it['beta'] * it['v'], it['beta'] * jnp.exp(it['g']) * it['k']], axis=1)
    blk = min(SOLVE_BLOCK, c)
    for it in items:
        it['solved'] = []
    for j0 in range(0, c, blk):
        rbs = [it['r'][j0:j0 + blk, :] for it in items]
        abs_ = [it['a'][j0:j0 + blk, j0:j0 + blk] for it in items]
        for s in range(blk - 1):
            rbs = [rb - ab[:, s:s + 1] * rb[s:s + 1, :] for rb, ab in zip(rbs, abs_)]
        for it, rb in zip(items, rbs):
            it['solved'].append(rb)
        if j0 + blk < c:
            belows = [_dot2(it['a'][j0 + blk:, j0:j0 + blk], rb) for it, rb in zip(items, rbs)]
            for it, below in zip(items, belows):
                it['r'] = jnp.concatenate([it['r'][:j0 + blk, :], it['r'][j0 + blk:, :] - below], axis=0)
    for it in items:
        w = it['solved'][0] if len(it['solved']) == 1 else jnp.concatenate(it['solved'], axis=0)
        it['w1'] = w[:, 0:DH]
        it['w2'] = w[:, DH:2 * DH]
        it['s'] = s_scr[it['b'], it['h']]
    for it in items:
        it['u'] = it['w1'] - _bdot(it['w2'], it['s'])
    for it in items:
        it['o'] = _bdot(it['q'] * jnp.exp(it['g']), it['s']) + _bdot(it['qk'] * it['dec'], it['u'])
    for it in items:
        g_last = it['g'][c - 1:c, :]
        kd = it['k'] * jnp.exp(g_last - it['g'])
        kd_t = lax.dot_general(eye_h.astype(BF16), kd.astype(BF16), (((1,), (1,)), ((), ())),
                               preferred_element_type=F32)
        it['snew'] = jnp.exp(g_last) * it['s'] + _bdot(kd_t, it['u'])
    for it in items:
        s_scr[it['b'], it['h']] = it['snew']
        o = it['o']
        zg = zg_ref[it['b'], :, it['hs']]
        o = o * lax.rsqrt(jnp.mean(o * o, axis=-1, keepdims=True) + EPS) * nw_ref[...] * _silu(zg)
        o_ref[it['b'], :, it['hs']] = o

    @pl.when(ci == pl.num_programs(1) - 1)
    def _():
        sf_ref[...] = s_scr[...]


def _gdn(z3, cw, alog_x, dtb_x, nw, buf8, s0, nb, c):
    bsz, l, _ = z3.shape
    zblk = lambda j: pl.BlockSpec((nb, c, GW), lambda b, i, j=j: (b, i, j))
    full = lambda r, w: pl.BlockSpec((r, w), lambda b, i: (0, 0))
    return pl.pallas_call(
        functools.partial(_gdn_kernel, nb=nb, c=c),
        out_shape=(jax.ShapeDtypeStruct((bsz, l, GW), F32), jax.ShapeDtypeStruct((bsz, HEADS, DH, DH), F32)),
        grid=(bsz // nb, l // c),
        in_specs=[zblk(ZB_BQ), zblk(ZB_BK), zblk(ZB_BV), zblk(ZB_BG), zblk(ZB_BB), zblk(ZB_BA),
                  full(B_CONV, 3 * GW), full(1, GW), full(1, GW), full(1, DH),
                  pl.BlockSpec((nb, 8, 3 * GW), lambda b, i: (b, 0, 0)),
                  pl.BlockSpec((nb, HEADS, DH, DH), lambda b, i: (b, 0, 0, 0))],
        out_specs=(pl.BlockSpec((nb, c, GW), lambda b, i: (b, i, 0)),
                   pl.BlockSpec((nb, HEADS, DH, DH), lambda b, i: (b, 0, 0, 0))),
        scratch_shapes=[pltpu.VMEM((nb, HEADS, DH, DH), F32), pltpu.VMEM((nb, 8 + c, 3 * GW), F32)],
        compiler_params=_cparams(("parallel", "arbitrary")),
        name="gdn",
    )(z3, z3, z3, z3, z3, z3, cw, alog_x, dtb_x, nw, buf8, s0)


CPAD = 32
DPAD = 16


def _convpool_kernel(za_ref, zb_ref, zd_ref, dw_ref, db_ref, lg_ref, lb_ref, pw_ref, ps_ref, cbuf_ref, dbuf_ref,
                     oc_ref, u_ref, od_ref, xc_scr, xd_scr, *, nb, c, pos0):
    ci = pl.program_id(1)

    @pl.when(ci == 0)
    def _():
        xc_scr[:, 0:CPAD, :] = cbuf_ref[...]
        xd_scr[:, 0:DPAD, :] = dbuf_ref[...]

    lane = _iota((c, GW), 1)
    wl = jnp.where(lane < 64, 2.0, jnp.where(lane < 128, 4.0, jnp.where(lane < 192, 8.0, 16.0)))
    pos = (_iota((c, GW), 0) + (ci * c + pos0 + 1)).astype(F32)
    cnt = jnp.minimum(wl, pos)
    for b in range(nb):
        u = za_ref[b] * _sigmoid(zb_ref[b])
        u_ref[b] = u
        xc_scr[b, CPAD:CPAD + c, :] = u
        y = dw_ref[C_CONV - 1:C_CONV, :] * u
        for j in range(C_CONV - 1):
            y = y + dw_ref[j:j + 1, :] * xc_scr[b, 2 + j:2 + j + c, :]
        xc_scr[b, 0:CPAD, :] = xc_scr[b, c:c + CPAD, :]
        y = y + db_ref[...]
        mu = jnp.mean(y, axis=-1, keepdims=True)
        yc = y - mu
        var = jnp.mean(yc * yc, axis=-1, keepdims=True)
        oc_ref[b] = _silu(yc * lax.rsqrt(var + EPS) * lg_ref[...] + lb_ref[...])
        x = zd_ref[b]
        xd_scr[b, DPAD:DPAD + c, :] = x
        acc = x
        sums = {}
        for i in range(1, 16):
            acc = acc + xd_scr[b, DPAD - i:DPAD - i + c, :]
            if i + 1 in POOL_WINDOWS:
                sums[i + 1] = acc
        xd_scr[b, 0:DPAD, :] = xd_scr[b, c:c + DPAD, :]
        ssel = jnp.where(lane < 64, sums[2], jnp.where(lane < 128, sums[4], jnp.where(lane < 192, sums[8], sums[16])))
        diff = ssel / cnt - x
        od_ref[b] = _bdot(diff, pw_ref[...]) * ps_ref[...]


def _convpool(z3, dw, db, lg, lb, pwbd, ps, cbuf, dbuf, nb, c, pos0):
    bsz, l, _ = z3.shape
    zblk = lambda j: pl.BlockSpec((nb, c, GW), lambda b, i, j=j: (b, i, j))
    full = lambda r, w: pl.BlockSpec((r, w), lambda b, i: (0, 0))
    oblk = pl.BlockSpec((nb, c, GW), lambda b, i: (b, i, 0))
    osh = jax.ShapeDtypeStruct((bsz, l, GW), F32)
    return pl.pallas_call(
        functools.partial(_convpool_kernel, nb=nb, c=c, pos0=pos0),
        out_shape=(osh, osh, osh),
        grid=(bsz // nb, l // c),
        in_specs=[zblk(ZB_CA), zblk(ZB_CB), zblk(ZB_D), full(C_CONV, GW), full(1, GW), full(1, GW), full(1, GW),
                  full(GW, GW), full(1, GW),
                  pl.BlockSpec((nb, CPAD, GW), lambda b, i: (b, 0, 0)),
                  pl.BlockSpec((nb, DPAD, GW), lambda b, i: (b, 0, 0))],
        out_specs=(oblk, oblk, oblk),
        scratch_shapes=[pltpu.VMEM((nb, CPAD + c, GW), F32), pltpu.VMEM((nb, DPAD + c, GW), F32)],
        compiler_params=_cparams(("parallel", "arbitrary")),
        name="convpool",
    )(z3, z3, z3, dw, db, lg, lb, pwbd, ps, cbuf, dbuf)


NKH = 2 * PEER_HEADS * PEER_NKEYS
_CANDS = [(a, b) for a in range(PEER_TOPK) for b in range(PEER_TOPK) if (a + 1) * (b + 1) <= PEER_TOPK]


def _oddeven_merge_sort_pairs(n):
    pairs = []

    def merge(lo, hi, r):
        step = r * 2
        if step < hi - lo:
            merge(lo, hi, step)
            merge(lo + r, hi, step)
            pairs.extend((i, i + r) for i in range(lo + r, hi - r, step))
        else:
            pairs.append((lo, lo + r))

    def sort(lo, hi):
        if hi - lo >= 1:
            mid = lo + (hi - lo) // 2
            sort(lo, mid)
            sort(mid + 1, hi)
            merge(lo, hi, 1)

    sort(0, n - 1)
    return pairs


_SORT16 = _oddeven_merge_sort_pairs(PEER_NKEYS // 8)


def _peer1_kernel(h_ref, g_ref, wq_ref, kbd_ref, xn_ref, e1_ref, c_ref, e2_ref, r2_ref, s_scr, sv_scr, d_scr, zi_scr,
                  *, nlt):
    xn = _rms(h_ref[...], g_ref[...]).T.astype(BF16)
    xn_ref[...] = pltpu.bitcast(xn, jnp.uint32)
    qt = jnp.dot(_rows_bf16(wq_ref[...]), xn, preferred_element_type=F32).astype(BF16)
    for hp in range(2 * PEER_HEADS):
        sc = jnp.dot(kbd_ref[hp], qt[hp * PEER_HALF:(hp + 1) * PEER_HALF, :], preferred_element_type=F32)
        for lt in range(nlt):
            s_scr[lt, hp * PEER_NKEYS:(hp + 1) * PEER_NKEYS, :] = sc[:, lt * LANES:(lt + 1) * LANES]

    def lane_tile(lt, carry):
        for h in range(PEER_HEADS):
            for p in range(2):
                s = s_scr[lt, (2 * h + p) * PEER_NKEYS:(2 * h + p + 1) * PEER_NKEYS, :]
                lists = [s[8 * i:8 * (i + 1), :] for i in range(PEER_NKEYS // 8)]
                for i, j in _SORT16:
                    lists[i], lists[j] = jnp.maximum(lists[i], lists[j]), jnp.minimum(lists[i], lists[j])
                svs = []
                for a in range(PEER_TOPK):
                    m = jnp.max(lists[0], axis=0, keepdims=True)
                    sv_scr[p, a, h:h + 1, :] = m
                    svs.append(m)
                    popped = lists[0] == m
                    for i in range(PEER_TOPK - 1 - a):
                        lists[i] = jnp.where(popped, lists[i + 1], lists[i])
                if p == 1:
                    rank = jnp.full(s.shape, float(PEER_TOPK), F32)
                    for a in reversed(range(PEER_TOPK)):
                        rank = jnp.where(s >= svs[a], float(a), rank)
                    r2_ref[lt, h] = pltpu.bitcast(rank.astype(BF16), jnp.uint32)
                    e2_ref[lt, h] = pltpu.bitcast(jnp.exp(s - svs[0]).astype(BF16), jnp.uint32)
        sv1 = [sv_scr[0, a] for a in range(PEER_TOPK)]
        sv2 = [sv_scr[1, a] for a in range(PEER_TOPK)]
        vals = [sv1[a] + sv2[b] for a, b in _CANDS]
        n = len(_CANDS)
        before = [jnp.zeros((PEER_HEADS, LANES), F32) for _ in range(n)]
        for i in range(n):
            ai, bi = _CANDS[i]
            for j in range(i + 1, n):
                aj, bj = _CANDS[j]
                if ai <= aj and bi <= bj:
                    before[j] = before[j] + 1.0
                else:
                    t = jnp.where(vals[i] >= vals[j], 1.0, 0.0)
                    before[j] = before[j] + t
                    before[i] = before[i] + (1.0 - t)
        ex1 = [jnp.exp(sv1[a] - sv1[0]) for a in range(PEER_TOPK)]
        ex2 = [jnp.exp(sv2[b] - sv2[0]) for b in range(PEER_TOPK)]
        cnt = [jnp.zeros((PEER_HEADS, LANES), F32) for _ in range(PEER_TOPK)]
        zsum = jnp.zeros((PEER_HEADS, LANES), F32)
        for i, (a, b) in enumerate(_CANDS):
            sel = jnp.where(before[i] < float(PEER_TOPK), 1.0, 0.0)
            cnt[a] = cnt[a] + sel
            zsum = zsum + sel * ex1[a] * ex2[b]
        for a in range(PEER_TOPK):
            d_scr[a] = cnt[a]
        zi_scr[...] = 1.0 / zsum
        for h in range(PEER_HEADS):
            s = s_scr[lt, 2 * h * PEER_NKEYS:(2 * h + 1) * PEER_NKEYS, :]
            cc = jnp.zeros(s.shape, F32)
            for a in reversed(range(PEER_TOPK)):
                cc = jnp.where(s >= sv_scr[0, a, h:h + 1, :], d_scr[a, h:h + 1, :], cc)
            c_ref[lt, h] = cc
            e1_ref[lt, h] = jnp.exp(s - sv_scr[0, 0, h:h + 1, :]) * zi_scr[h:h + 1, :]
        return carry

    lax.fori_loop(0, nlt, lane_tile, 0)


def _peer1(h, g, wq_bf16, kbd_bf16, tt):
    t = h.shape[0]
    nlt = tt // LANES
    gsh = jax.ShapeDtypeStruct((t // LANES, PEER_HEADS, PEER_NKEYS, LANES), F32)
    gblk = pl.BlockSpec((nlt, PEER_HEADS, PEER_NKEYS, LANES), lambda i: (i, 0, 0, 0))
    psh = jax.ShapeDtypeStruct((t // LANES, PEER_HEADS, PEER_NKEYS // 2, LANES), jnp.uint32)
    pblk = pl.BlockSpec((nlt, PEER_HEADS, PEER_NKEYS // 2, LANES), lambda i: (i, 0, 0, 0))
    return pl.pallas_call(
        functools.partial(_peer1_kernel, nlt=nlt),
        out_shape=(jax.ShapeDtypeStruct((D_MODEL // 2, t), jnp.uint32), gsh, gsh, psh, psh),
        grid=(t // tt,),
        in_specs=[pl.BlockSpec((tt, D_MODEL), lambda i: (i, 0)),
                  pl.BlockSpec((1, D_MODEL), lambda i: (0, 0)),
                  pl.BlockSpec((D_MODEL // 2, D_MODEL), lambda i: (0, 0)),
                  pl.BlockSpec((2 * PEER_HEADS, PEER_NKEYS, PEER_HALF), lambda i: (0, 0, 0))],
        out_specs=(pl.BlockSpec((D_MODEL // 2, tt), lambda i: (0, i)), gblk, gblk, pblk, pblk),
        scratch_shapes=[pltpu.VMEM((nlt, NKH, LANES), F32),
                        pltpu.VMEM((2, PEER_TOPK, PEER_HEADS, LANES), F32),
                        pltpu.VMEM((PEER_TOPK, PEER_HEADS, LANES), F32),
                        pltpu.VMEM((PEER_HEADS, LANES), F32)],
        compiler_params=_cparams(("parallel",)),
        name="peer1",
    )(h, g, wq_bf16, kbd_bf16)


NE1 = 8
ETILE = NE1 * PEER_NKEYS


def _peer2_step(u_ref, xn_ref, vt_ref, e1_ref, c_ref, e2_ref, r2_ref, acc_scr, act_w, act_r, w_w, w_r, nlt,
                do_value=True, do_gate=True, do_act=True):
    tile = (PEER_NKEYS, LANES)
    zero = jnp.zeros(tile, BF16)
    grp = 2
    nmb = 4
    mblk = D_MODEL // nmb

    def value_rows(m):
        rs = slice(m * mblk, (m + 1) * mblk)
        vrows = _rows_bf16(vt_ref[m * mblk // 2:(m + 1) * mblk // 2, :])
        acc_scr[rs, :] += jnp.dot(vrows, _rows_bf16(w_r[...]), preferred_element_type=F32)

    def act_rows(m):
        urows = _rows_bf16(u_ref[m * mblk // 2:(m + 1) * mblk // 2, :])
        a = jnp.dot(urows, _rows_bf16(xn_ref[...]), preferred_element_type=F32).astype(BF16)
        a = 0.5 * a * (1.0 + lax.erf(a * 0.7071067811865476))
        act_w[m * mblk // 2:(m + 1) * mblk // 2, :] = pltpu.bitcast(a, jnp.uint32)

    def gate_group(lt, g0):
        ls = slice(lt * LANES, (lt + 1) * LANES)
        gates = [zero] * grp
        for h in range(PEER_HEADS):
            rank = pltpu.bitcast(r2_ref[lt, h], BF16)
            wkey = pltpu.bitcast(e2_ref[lt, h], BF16)
            for j in range(grp):
                i1 = g0 + j
                cnt = jnp.broadcast_to(c_ref[lt, h, i1:i1 + 1, :], tile).astype(BF16)
                wgt = jnp.broadcast_to(e1_ref[lt, h, i1:i1 + 1, :], tile).astype(BF16)
                gates[j] = gates[j] + jnp.where(rank < cnt, wkey, zero) * wgt
        for j in range(grp):
            ps = slice((g0 + j) * PEER_NKEYS // 2, (g0 + j + 1) * PEER_NKEYS // 2)
            w_w[ps, ls] = pltpu.bitcast(gates[j] * pltpu.bitcast(act_r[ps, ls], BF16), jnp.uint32)

    groups = [(lt, g0) for lt in range(nlt) for g0 in range(0, NE1, grp)] if do_gate else []
    mxu_work = [f for m in range(nmb) for f, on in ((functools.partial(value_rows, m), do_value),
                                                  (functools.partial(act_rows, m), do_act)) if on]
    if not mxu_work:
        mxu_work = [lambda: None]
    per = max(1, len(groups) // len(mxu_work))
    gi = 0
    for k, mm in enumerate(mxu_work):
        mm()
        take = len(groups) - gi if k == len(mxu_work) - 1 else per
        for _ in range(take):
            if gi < len(groups):
                gate_group(*groups[gi])
                gi += 1


def _peer2_kernel(h_ref, xn_ref, u_ref, vt_ref, e1_ref, c_ref, e2_ref, r2_ref, o_ref, acc_scr, act0, act1, w0, w1, *,
                  nlt):
    e = pl.program_id(1)

    last = pl.num_programs(1) - 1
    args = (u_ref, xn_ref, vt_ref, e1_ref, c_ref, e2_ref, r2_ref, acc_scr)
    even = (act0, act1, w1, w0)
    odd = (act1, act0, w0, w1)
    steady = jnp.logical_and(e >= 2, e <= last - 2)

    @pl.when(e == 0)
    def _():
        acc_scr[...] = jnp.zeros(acc_scr.shape, F32)
        _peer2_step(*args, *even, nlt, do_value=False, do_gate=False)

    @pl.when(e == 1)
    def _():
        _peer2_step(*args, *odd, nlt, do_value=False)

    @pl.when(jnp.logical_and(steady, e % 2 == 0))
    def _():
        _peer2_step(*args, *even, nlt)

    @pl.when(jnp.logical_and(steady, e % 2 == 1))
    def _():
        _peer2_step(*args, *odd, nlt)

    @pl.when(e == last - 1)
    def _():
        _peer2_step(*args, *even, nlt, do_act=False)

    @pl.when(e == last)
    def _():
        _peer2_step(*args, *odd, nlt, do_act=False, do_gate=False)
        o_ref[...] = h_ref[...] + acc_scr[...].T


def _peer2(h, xn, u_bf16, vt_bf16, e1, cc, e2, r2, tt):
    t = h.shape[0]
    nlt = tt // LANES
    ne = PEER_NKEYS // NE1
    assert ne % 2 == 0
    last = ne - 1
    pfull = pl.BlockSpec((nlt, PEER_HEADS, PEER_NKEYS // 2, LANES), lambda i, e: (i, 0, 0, 0))
    gtile = pl.BlockSpec((nlt, PEER_HEADS, NE1, LANES), lambda i, e: (i, 0, jnp.clip(e - 1, 0, last), 0))
    slot = pltpu.VMEM((ETILE // 2, tt), jnp.uint32)
    return pl.pallas_call(
        functools.partial(_peer2_kernel, nlt=nlt),
        out_shape=jax.ShapeDtypeStruct((t, D_MODEL), F32),
        grid=(t // tt, ne + 2),
        in_specs=[pl.BlockSpec((tt, D_MODEL), lambda i, e: (i, 0)),
                  pl.BlockSpec((D_MODEL // 2, tt), lambda i, e: (0, i)),
                  pl.BlockSpec((ETILE // 2, D_MODEL), lambda i, e: (jnp.minimum(e, last), 0)),
                  pl.BlockSpec((D_MODEL // 2, ETILE), lambda i, e: (0, jnp.clip(e - 2, 0, last))),
                  gtile, gtile, pfull, pfull],
        out_specs=pl.BlockSpec((tt, D_MODEL), lambda i, e: (i, 0)),
        scratch_shapes=[pltpu.VMEM((D_MODEL, tt), F32), slot, slot, slot, slot],
        compiler_params=_cparams(("parallel", "arbitrary")),
        name="peer2",
    )(h, xn, u_bf16, vt_bf16, e1, cc, e2, r2)


def _prep_layer(i, w_in, lbs, hgrn_norm, gdn_conv_w, gdn_a_log, gdn_dt_bias, gdn_norm, conf_dw_w, conf_dw_b,
                conf_ln_g, conf_ln_b, pool_w, pool_scale, w_out, peer_wq, peer_keys, peer_u, peer_v, ple_gate_w,
                ple_proj_w):
    wi = w_in[i]
    hk = HEADS * DH
    c0 = 4 * hk
    qkv = wi[:, c0:c0 + 3 * hk]
    zbg = wi[:, c0 + 3 * hk:c0 + 4 * hk]
    zbb = wi[:, c0 + 4 * hk:c0 + 4 * hk + HEADS]
    zba = wi[:, c0 + 4 * hk + HEADS:c0 + 4 * hk + 2 * HEADS]
    rest = wi[:, c0 + 4 * hk + 2 * HEADS:]
    w_perm = jnp.concatenate([wi[:, :c0], qkv, zbg, rest, jnp.repeat(zbb, DH, axis=1), jnp.repeat(zba, DH, axis=1)],
                             axis=1)
    kbd = peer_keys[i].reshape(2 * PEER_HEADS, PEER_NKEYS, PEER_HALF).astype(BF16)
    pw = pool_w[i]
    eye4 = jnp.eye(4, dtype=F32)
    pwbd = (pw[:, :, None, :] * eye4[:, None, :, None]).reshape(GW, GW)
    return dict(
        w_in=_pack_rows(w_perm), lb=lbs[i][None, :], hgrn_norm=hgrn_norm[i][None, :],
        gdn_conv_w=gdn_conv_w[i], gdn_a_log=jnp.repeat(gdn_a_log[i], DH)[None, :],
        gdn_dt_bias=jnp.repeat(gdn_dt_bias[i], DH)[None, :], gdn_norm=gdn_norm[i][None, :],
        conf_dw_w=conf_dw_w[i], conf_dw_b=conf_dw_b[i][None, :], conf_ln_g=conf_ln_g[i][None, :],
        conf_ln_b=conf_ln_b[i][None, :], pool_w=pwbd, pool_scale=pool_scale[i][None, :],
        w_out=_pack_rows(w_out, layer=i), peer_wq=_pack_rows(peer_wq, transpose=True, layer=i), kbd=kbd,
        peer_u=_pack_rows(peer_u, layer=i), peer_vt=_pack_rows(peer_v, transpose=True, layer=i),
        ple_gate_w=_pack_rows(ple_gate_w, layer=i), ple_proj_w=_pack_rows(ple_proj_w, layer=i))


def _tail(buf, x, n):
    l = x.shape[1]
    if l >= n:
        return x[:, l - n:]
    return jnp.concatenate([buf[:, l:], x], axis=1)


def _trunk(x, p, pos0, s_a, s_b, c_b, c_c, c_d, layers, norms, norm_final):
    bsz, l, _ = x.shape
    t = bsz * l
    prompt = l >= 64
    c = 64 if prompt else l
    nb = 1 if prompt else 8
    cc = min(256, l)
    tm = min(512, t)
    tt = min(512, t)
    h = x.reshape(t, D_MODEL)
    n_a, n_b, nb_b, nb_c, nb_d = [], [], [], [], []
    for i, lw in enumerate(layers):
        z = _inproj(h, norms['mix'][i][None, :], lw['w_in'], tm)
        z3 = z.reshape(bsz, l, ZW)
        eye4 = jnp.eye(HEADS, dtype=F32)
        s0t = (jnp.swapaxes(s_a[i], 2, 3)[:, :, :, None, :] * eye4[None, :, None, :, None]).reshape(bsz, GW, GW)
        o_a, st = _hgrn(z3, lw['lb'], lw['hgrn_norm'], s0t, min(4, bsz) if prompt else nb, min(c, 32))
        st = st.reshape(bsz, HEADS, DH, HEADS, DH)
        n_a.append(jnp.stack([jnp.swapaxes(st[:, j, :, j, :], 1, 2) for j in range(HEADS)], axis=1))
        buf8 = jnp.pad(c_b[i], ((0, 0), (8 - (B_CONV - 1), 0), (0, 0)))
        o_b, sb = _gdn(z3, lw['gdn_conv_w'], lw['gdn_a_log'], lw['gdn_dt_bias'], lw['gdn_norm'], buf8, s_b[i],
                       min(4, bsz) if prompt else nb, c)
        n_b.append(sb)
        nb_b.append(_tail(c_b[i], z3[:, :, ZB_BQ * GW:(ZB_BV + 1) * GW], B_CONV - 1))
        cbuf = jnp.pad(c_c[i], ((0, 0), (CPAD - (C_CONV - 1), 0), (0, 0)))
        dbuf = jnp.pad(c_d[i], ((0, 0), (DPAD - POOL_BUF, 0), (0, 0)))
        o_c, u_c, o_d = _convpool(z3, lw['conf_dw_w'], lw['conf_dw_b'], lw['conf_ln_g'], lw['conf_ln_b'],
                                  lw['pool_w'], lw['pool_scale'], cbuf, dbuf, nb, cc, pos0)
        nb_c.append(_tail(c_c[i], u_c, C_CONV - 1))
        nb_d.append(_tail(c_d[i], z3[:, :, ZB_D * GW:(ZB_D + 1) * GW], POOL_BUF))
        h = _outproj(h, o_a.reshape(t, GW), o_b.reshape(t, GW), o_c.reshape(t, GW), o_d.reshape(t, GW),
                     lw['w_out'], tm)
        xn, e1, cnt, e2, r2 = _peer1(h, norms['ffn'][i][None, :], lw['peer_wq'], lw['kbd'], tt)
        h = _peer2(h, xn, lw['peer_u'], lw['peer_vt'], e1, cnt, e2, r2, min(1024, t))
        h = _ple(h, p.reshape(p.shape[0], t, D_PLE), i, norms['ple'][i][None, :], lw['ple_gate_w'], lw['ple_proj_w'],
                 norm_final[None, :], tm, final=(i == len(layers) - 1))
    st = lambda lst: jnp.stack(lst, axis=0)
    return h.reshape(bsz, l, D_MODEL), st(n_a), st(n_b), st(nb_b), st(nb_c), st(nb_d)


def kernel(x_prompt, x_sample, state_hgrn, state_gdn, state_gdn_conv, state_conf_conv, state_pool, p_prompt, p_sample, norm_mix, w_in, hgrn_lb_logits, hgrn_norm, gdn_conv_w, gdn_a_log, gdn_dt_bias, gdn_norm, conf_dw_w, conf_dw_b, conf_ln_g, conf_ln_b, pool_w, pool_scale, w_out, norm_ffn, peer_wq, peer_keys, peer_u, peer_v, norm_ple, ple_gate_w, ple_proj_w, norm_final):
    sm = jax.nn.softmax(hgrn_lb_logits.astype(F32), axis=0)
    lbs = jnp.cumsum(sm, axis=0) - sm[0:1]
    layers = [_prep_layer(i, w_in, lbs, hgrn_norm, gdn_conv_w, gdn_a_log, gdn_dt_bias, gdn_norm, conf_dw_w,
                          conf_dw_b, conf_ln_g, conf_ln_b, pool_w, pool_scale, w_out, peer_wq, peer_keys, peer_u,
                          peer_v, ple_gate_w, ple_proj_w) for i in range(DEPTH)]
    norms = dict(mix=norm_mix, ffn=norm_ffn, ple=norm_ple)
    bp = x_prompt.shape[0]
    z = lambda *s: jnp.zeros((DEPTH, bp) + s, F32)
    yp, hp, gp, gcp, ccp, pp = _trunk(x_prompt, p_prompt, 0, z(HEADS, DH, DH), z(HEADS, DH, DH),
                                      z(B_CONV - 1, 3 * GW), z(C_CONV - 1, GW), z(POOL_BUF, GW),
                                      layers, norms, norm_final)
    ys, hs, gs, gcs, ccs, ps = _trunk(x_sample, p_sample, PAST_LEN, state_hgrn, state_gdn, state_gdn_conv,
                                      state_conf_conv, state_pool, layers, norms, norm_final)
    return (yp, ys, hp, gp, gcp, ccp, pp, hs, gs, gcs, ccs, ps)
```

```python
import functools

import numpy as np
import jax
import jax.numpy as jnp
from jax import lax
from jax.experimental import pallas as pl
from jax.experimental.pallas import tpu as pltpu

F32 = jnp.float32
BF16 = jnp.bfloat16

D_MODEL = 1024
DEPTH = 4
PAST_LEN = 16384
EPS = 1e-6
NEG_BIG = -1e30
D_PLE = 256
HEADS = 4
DH = 64
GW = 256
B_CONV = 4
C_CONV = 31
POOL_WINDOWS = (2, 4, 8, 16)
POOL_BUF = 15
PEER_HEADS = 8
PEER_NKEYS = 128
PEER_HALF = 64
PEER_TOPK = 16

ZB_AQ, ZB_AF, ZB_AI, ZB_AG, ZB_BQ, ZB_BK, ZB_BV, ZB_BG, ZB_CA, ZB_CB, ZB_D, ZB_BB, ZB_BA = range(13)
ZW = 13 * GW

LANES = 128
SOLVE_BLOCK = 16
VMEM_LIMIT = 56 * 1024 * 1024


def _cparams(sem, flags=None):
    return pltpu.CompilerParams(dimension_semantics=sem, vmem_limit_bytes=VMEM_LIMIT, flags=flags)


def _bdot(a, b):
    return jnp.dot(a.astype(BF16), b.astype(BF16), preferred_element_type=F32)


def _bdot_nt(a, b):
    return lax.dot_general(a.astype(BF16), b.astype(BF16), (((1,), (1,)), ((), ())),
                           preferred_element_type=F32)


def _split3(a):
    a1 = a.astype(BF16)
    r1 = a - a1.astype(F32)
    a2 = r1.astype(BF16)
    a3 = (r1 - a2.astype(F32)).astype(BF16)
    return a1, a2, a3


def _xdot(a, m01):
    a1, a2, a3 = _split3(a)
    m = m01.astype(BF16)
    return (jnp.dot(a1, m, preferred_element_type=F32) + jnp.dot(a2, m, preferred_element_type=F32)
            + jnp.dot(a3, m, preferred_element_type=F32))


def _xdot_left(m01, a):
    a1, a2, a3 = _split3(a)
    m = m01.astype(BF16)
    return (jnp.dot(m, a1, preferred_element_type=F32) + jnp.dot(m, a2, preferred_element_type=F32)
            + jnp.dot(m, a3, preferred_element_type=F32))


def _dot2(a, b):
    ah = a.astype(BF16)
    al = (a - ah.astype(F32)).astype(BF16)
    bh = b.astype(BF16)
    bl = (b - bh.astype(F32)).astype(BF16)
    return (jnp.dot(ah, bh, preferred_element_type=F32) + jnp.dot(ah, bl, preferred_element_type=F32)
            + jnp.dot(al, bh, preferred_element_type=F32))


def _xtranspose(a, eye):
    a1, a2, a3 = _split3(a)
    e = eye.astype(BF16)
    dn = (((1,), (1,)), ((), ()))
    return (lax.dot_general(e, a1, dn, preferred_element_type=F32)
            + lax.dot_general(e, a2, dn, preferred_element_type=F32)
            + lax.dot_general(e, a3, dn, preferred_element_type=F32))


def _pack_kernel(x_ref, o_ref, *, transpose):
    x = x_ref[...]
    if transpose:
        x = x.T
    o_ref[...] = pltpu.bitcast(x.astype(BF16), jnp.uint32)


def _pack_rows(w, transpose=False, layer=None):
    r, c = w.shape[-2:]
    rb = min(r, 1024)
    if layer is None:
        in_spec = pl.BlockSpec((rb, c), lambda i: (i, 0))
    else:
        in_spec = pl.BlockSpec((None, rb, c), lambda i: (layer, i, 0))
    if transpose:
        out_shape, out_spec = (c // 2, r), pl.BlockSpec((c // 2, rb), lambda i: (0, i))
    else:
        out_shape, out_spec = (r // 2, c), pl.BlockSpec((rb // 2, c), lambda i: (i, 0))
    return pl.pallas_call(
        functools.partial(_pack_kernel, transpose=transpose),
        out_shape=jax.ShapeDtypeStruct(out_shape, jnp.uint32),
        grid=(r // rb,),
        in_specs=[in_spec],
        out_specs=out_spec,
        compiler_params=_cparams(("parallel",)),
        name="pack_t" if transpose else "pack",
    )(w)


def _rows_bf16(words):
    return pltpu.bitcast(words, BF16)


def _iota(shape, axis):
    return lax.broadcasted_iota(jnp.int32, shape, axis)


def _tri_incl(c):
    return jnp.where(_iota((c, c), 1) <= _iota((c, c), 0), 1.0, 0.0).astype(F32)


def _eye(n):
    return jnp.where(_iota((n, n), 0) == _iota((n, n), 1), 1.0, 0.0).astype(F32)


def _block_ones(n, blk):
    return jnp.where(_iota((n, n), 0) // blk == _iota((n, n), 1) // blk, 1.0, 0.0).astype(F32)


def _rms(x, g):
    return x * lax.rsqrt(jnp.mean(x * x, axis=-1, keepdims=True) + EPS) * g


def _sigmoid(x):
    return jax.nn.sigmoid(x)


def _silu(x):
    return x * jax.nn.sigmoid(x)


def _inproj_kernel(x_ref, g_ref, w_ref, o_ref):
    xn = _rms(x_ref[...], g_ref[...])
    o_ref[...] = jnp.dot(xn.astype(BF16), _rows_bf16(w_ref[...]), preferred_element_type=F32)


def _inproj(h, g, w_bf16, tm):
    t = h.shape[0]
    return pl.pallas_call(
        _inproj_kernel,
        out_shape=jax.ShapeDtypeStruct((t, ZW), F32),
        grid=(t // tm,),
        in_specs=[pl.BlockSpec((tm, D_MODEL), lambda i: (i, 0)),
                  pl.BlockSpec((1, D_MODEL), lambda i: (0, 0)),
                  pl.BlockSpec((D_MODEL // 2, ZW), lambda i: (0, 0))],
        out_specs=pl.BlockSpec((tm, ZW), lambda i: (i, 0)),
        compiler_params=_cparams(("parallel",)),
        name="inproj",
    )(h, g, w_bf16)


def _outproj_kernel(h_ref, a_ref, b_ref, c_ref, d_ref, w_ref, o_ref):
    acc = h_ref[...]
    for i, r in enumerate((a_ref, b_ref, c_ref, d_ref)):
        acc = acc + jnp.dot(r[...].astype(BF16), _rows_bf16(w_ref[i * GW // 2:(i + 1) * GW // 2, :]),
                            preferred_element_type=F32)
    o_ref[...] = acc


def _outproj(h, oa, ob, oc, od, w_bf16, tm):
    t = h.shape[0]
    tok = lambda w: pl.BlockSpec((tm, w), lambda i: (i, 0))
    return pl.pallas_call(
        _outproj_kernel,
        out_shape=jax.ShapeDtypeStruct((t, D_MODEL), F32),
        grid=(t // tm,),
        in_specs=[tok(D_MODEL), tok(GW), tok(GW), tok(GW), tok(GW),
                  pl.BlockSpec((D_MODEL // 2, D_MODEL), lambda i: (0, 0))],
        out_specs=tok(D_MODEL),
        compiler_params=_cparams(("parallel",)),
        name="outproj",
    )(h, oa, ob, oc, od, w_bf16)


def _ple_kernel(h_ref, p_ref, g_ref, wg_ref, wp_ref, gf_ref, o_ref, *, final):
    h = h_ref[...]
    xn = _rms(h, g_ref[...])
    gate = _sigmoid(jnp.dot(xn.astype(BF16), _rows_bf16(wg_ref[...]), preferred_element_type=F32))
    proj = jnp.dot(p_ref[...].astype(BF16), _rows_bf16(wp_ref[...]), preferred_element_type=F32)
    out = h + gate * proj
    if final:
        out = _rms(out, gf_ref[...])
    o_ref[...] = out


def _ple(h, p, layer, g, wg_bf16, wp_bf16, gfinal, tm, final):
    t = h.shape[0]
    tok = lambda w: pl.BlockSpec((tm, w), lambda i: (i, 0))
    ptok = pl.BlockSpec((None, tm, D_PLE), lambda i: (layer, i, 0))
    full = lambda r, c: pl.BlockSpec((r, c), lambda i: (0, 0))
    return pl.pallas_call(
        functools.partial(_ple_kernel, final=final),
        out_shape=jax.ShapeDtypeStruct((t, D_MODEL), F32),
        grid=(t // tm,),
        in_specs=[tok(D_MODEL), ptok, full(1, D_MODEL), full(D_MODEL // 2, D_MODEL), full(D_PLE // 2, D_MODEL),
                  full(1, D_MODEL)],
        out_specs=tok(D_MODEL),
        compiler_params=_cparams(("parallel",)),
        name="ple_final" if final else "ple",
    )(h, p, g, wg_bf16, wp_bf16, gfinal)


def _hgrn_kernel(z_ref, lb_ref, nw_ref, s0_ref, o_ref, sf_ref, st_scr, la_scr, q_scr, k_scr, v_scr, p_scr, *, nb, c):
    ci = pl.program_id(1)

    @pl.when(ci == 0)
    def _():
        st_scr[...] = s0_ref[...]

    lb = lb_ref[...]
    bones = _block_ones(GW, DH)
    tri = _tri_incl(c)
    rows = _iota((c, GW), 0)
    seqs = range(nb)
    for b in seqs:
        zf = z_ref[b, :, 1 * GW:2 * GW]
        q_scr[b] = _silu(z_ref[b, :, 0 * GW:1 * GW])
        k_scr[b] = (1.0 - lb) * _sigmoid(-zf)
        v_scr[b] = z_ref[b, :, 2 * GW:3 * GW]
        f = lb + (1.0 - lb) * _sigmoid(zf)
        la_scr[b] = _xdot_left(tri, jnp.log(f))

    def build(s, carry):
        for b in seqs:
            la_s = la_scr[b, pl.ds(s, 1), :]
            k_s = k_scr[b, pl.ds(s, 1), :]
            dec = jnp.exp(jnp.minimum(la_scr[b] - la_s, 0.0))
            p_scr[b, pl.ds(pl.multiple_of(s * c, c), c), :] = jnp.where(rows >= s, q_scr[b] * k_s * dec, 0.0)
        return carry

    lax.fori_loop(0, c, build, 0, unroll=min(4, c))
    for b in seqs:
        p_scr[b] = jnp.dot(p_scr[b].astype(BF16), bones.astype(BF16), preferred_element_type=F32)

    def consume(s, accs):
        return tuple(acc + p_scr[b, pl.ds(pl.multiple_of(s * c, c), c), :] * v_scr[b, pl.ds(s, 1), :]
                     for b, acc in zip(seqs, accs))

    os_ = lax.fori_loop(0, c, consume, tuple(jnp.zeros((c, GW), F32) for _ in seqs), unroll=min(4, c))
    sts = [st_scr[b] for b in seqs]
    las = [la_scr[b] for b in seqs]
    os_ = [o + _bdot_nt(q_scr[b] * jnp.exp(la), st) for b, o, la, st in zip(seqs, os_, las, sts)]
    upds = []
    for b, la in zip(seqs, las):
        k_dec = k_scr[b] * jnp.exp(la[c - 1:c, :] - la)
        upds.append(lax.dot_general(v_scr[b].astype(BF16), k_dec.astype(BF16), (((0,), (0,)), ((), ())),
                                    preferred_element_type=F32))
    for b, la, st, upd in zip(seqs, las, sts, upds):
        st_scr[b] = st * jnp.exp(la[c - 1:c, :]) + upd * bones
    mss = [_xdot(o * o, bones) * (1.0 / DH) for o in os_]
    for b, o, ms in zip(seqs, os_, mss):
        o_ref[b] = o * lax.rsqrt(ms + EPS) * nw_ref[...] * _sigmoid(z_ref[b, :, 3 * GW:4 * GW])

    @pl.when(ci == pl.num_programs(1) - 1)
    def _():
        sf_ref[...] = st_scr[...]


def _hgrn(z3, lb, nw, s0t, nb, c):
    bsz, l, _ = z3.shape
    return pl.pallas_call(
        functools.partial(_hgrn_kernel, nb=nb, c=c),
        out_shape=(jax.ShapeDtypeStruct((bsz, l, GW), F32), jax.ShapeDtypeStruct((bsz, GW, GW), F32)),
        grid=(bsz // nb, l // c),
        in_specs=[pl.BlockSpec((nb, c, 4 * GW), lambda b, i: (b, i, 0)),
                  pl.BlockSpec((1, GW), lambda b, i: (0, 0)),
                  pl.BlockSpec((1, GW), lambda b, i: (0, 0)),
                  pl.BlockSpec((nb, GW, GW), lambda b, i: (b, 0, 0))],
        out_specs=(pl.BlockSpec((nb, c, GW), lambda b, i: (b, i, 0)),
                   pl.BlockSpec((nb, GW, GW), lambda b, i: (b, 0, 0))),
        scratch_shapes=[pltpu.VMEM((nb, GW, GW), F32)] + [pltpu.VMEM((nb, c, GW), F32)] * 4
                       + [pltpu.VMEM((nb, c * c, GW), F32)],
        compiler_params=_cparams(("parallel", "arbitrary")),
        name="hgrn",
    )(z3, lb, nw, s0t)


def _gdn_kernel(zq_ref, zk_ref, zv_ref, zg_ref, zb_ref, za_ref, cw_ref, alog_ref, dtb_ref, nw_ref, buf_ref, s0_ref,
                o_ref, sf_ref, s_scr, xp_scr, *, nb, c):
    ci = pl.program_id(1)

    @pl.when(ci == 0)
    def _():
        s_scr[...] = s0_ref[...]
        xp_scr[:, 0:8, :] = buf_ref[...]

    bones = _block_ones(GW, DH)
    tri = _tri_incl(c)
    eye_c = _eye(c)
    eye_h = _eye(DH)
    tt = _iota((c, c), 0)
    ss = _iota((c, c), 1)
    causal = tt >= ss
    strict = tt > ss
    items = []
    for b in range(nb):
        for j, r in enumerate((zq_ref, zk_ref, zv_ref)):
            xp_scr[b, 8:8 + c, j * GW:(j + 1) * GW] = r[b]
        conv = cw_ref[3:4, :] * xp_scr[b, 8:8 + c, :]
        for j in range(B_CONV - 1):
            conv = conv + cw_ref[j:j + 1, :] * xp_scr[b, 5 + j:5 + j + c, :]
        xp_scr[b, 0:8, :] = xp_scr[b, c:c + 8, :]
        qkv = _silu(conv)
        q = qkv[:, 0:GW]
        k = qkv[:, GW:2 * GW]
        v = qkv[:, 2 * GW:3 * GW]
        q = q * lax.rsqrt(_xdot(q * q, bones) + EPS) * (DH ** -0.5)
        k = k * lax.rsqrt(_xdot(k * k, bones) + EPS)
        beta = _sigmoid(zb_ref[b])
        loga = -jnp.exp(alog_ref[...]) * jax.nn.softplus(za_ref[b] + dtb_ref[...])
        g = _xdot_left(tri, loga)
        for h in range(HEADS):
            hs = slice(h * DH, (h + 1) * DH)
            items.append(dict(b=b, h=h, hs=hs, q=q[:, hs], k=k[:, hs], v=v[:, hs], beta=beta[:, hs], g=g[:, hs]))
    for it in items:
        it['gcol'] = it['g'][:, 0:c]
    for it in items:
        it['grow'] = _xtranspose(it['gcol'], eye_c)
    for it in items:
        it['kk'] = _bdot_nt(it['k'], it['k'])
        it['qk'] = _bdot_nt(it['q'], it['k'])
    for it in items:
        it['dec'] = jnp.exp(jnp.where(causal, it['gcol'] - it['grow'], NEG_BIG))
        it['a'] = jnp.where(strict, it['beta'][:, 0:c] * it['kk'] * it['dec'], 0.0)
        it['r'] = jnp.concatenate([it['beta'] * it['v'], it['beta'] * jnp.exp(it['g']) * it['k']], axis=1)
    blk = min(SOLVE_BLOCK, c)
    for it in items:
        it['solved'] = []
    for j0 in range(0, c, blk):
        rbs = [it['r'][j0:j0 + blk, :] for it in items]
        abs_ = [it['a'][j0:j0 + blk, j0:j0 + blk] for it in items]
        for s in range(blk - 1):
            rbs = [rb - ab[:, s:s + 1] * rb[s:s + 1, :] for rb, ab in zip(rbs, abs_)]
        for it, rb in zip(items, rbs):
            it['solved'].append(rb)
        if j0 + blk < c:
            belows = [_dot2(it['a'][j0 + blk:, j0:j0 + blk], rb) for it, rb in zip(items, rbs)]
            for it, below in zip(items, belows):
                it['r'] = jnp.concatenate([it['r'][:j0 + blk, :], it['r'][j0 + blk:, :] - below], axis=0)
    for it in items:
        w = it['solved'][0] if len(it['solved']) == 1 else jnp.concatenate(it['solved'], axis=0)
        it['w1'] = w[:, 0:DH]
        it['w2'] = w[:, DH:2 * DH]
        it['s'] = s_scr[it['b'], it['h']]
    for it in items:
        it['u'] = it['w1'] - _bdot(it['w2'], it['s'])
    for it in items:
        it['o'] = _bdot(it['q'] * jnp.exp(it['g']), it['s']) + _bdot(it['qk'] * it['dec'], it['u'])
    for it in items:
        g_last = it['g'][c - 1:c, :]
        kd = it['k'] * jnp.exp(g_last - it['g'])
        kd_t = lax.dot_general(eye_h.astype(BF16), kd.astype(BF16), (((1,), (1,)), ((), ())),
                               preferred_element_type=F32)
        it['snew'] = jnp.exp(g_last) * it['s'] + _bdot(kd_t, it['u'])
    for it in items:
        s_scr[it['b'], it['h']] = it['snew']
        o = it['o']
        zg = zg_ref[it['b'], :, it['hs']]
        o = o * lax.rsqrt(jnp.mean(o * o, axis=-1, keepdims=True) + EPS) * nw_ref[...] * _silu(zg)
        o_ref[it['b'], :, it['hs']] = o

    @pl.when(ci == pl.num_programs(1) - 1)
    def _():
        sf_ref[...] = s_scr[...]


def _gdn(z3, cw, alog_x, dtb_x, nw, buf8, s0, nb, c):
    bsz, l, _ = z3.shape
    zblk = lambda j: pl.BlockSpec((nb, c, GW), lambda b, i, j=j: (b, i, j))
    full = lambda r, w: pl.BlockSpec((r, w), lambda b, i: (0, 0))
    return pl.pallas_call(
        functools.partial(_gdn_kernel, nb=nb, c=c),
        out_shape=(jax.ShapeDtypeStruct((bsz, l, GW), F32), jax.ShapeDtypeStruct((bsz, HEADS, DH, DH), F32)),
        grid=(bsz // nb, l // c),
        in_specs=[zblk(ZB_BQ), zblk(ZB_BK), zblk(ZB_BV), zblk(ZB_BG), zblk(ZB_BB), zblk(ZB_BA),
                  full(B_CONV, 3 * GW), full(1, GW), full(1, GW), full(1, DH),
                  pl.BlockSpec((nb, 8, 3 * GW), lambda b, i: (b, 0, 0)),
                  pl.BlockSpec((nb, HEADS, DH, DH), lambda b, i: (b, 0, 0, 0))],
        out_specs=(pl.BlockSpec((nb, c, GW), lambda b, i: (b, i, 0)),
                   pl.BlockSpec((nb, HEADS, DH, DH), lambda b, i: (b, 0, 0, 0))),
        scratch_shapes=[pltpu.VMEM((nb, HEADS, DH, DH), F32), pltpu.VMEM((nb, 8 + c, 3 * GW), F32)],
        compiler_params=_cparams(("parallel", "arbitrary")),
        name="gdn",
    )(z3, z3, z3, z3, z3, z3, cw, alog_x, dtb_x, nw, buf8, s0)


CPAD = 32
DPAD = 16


def _convpool_kernel(za_ref, zb_ref, zd_ref, dw_ref, db_ref, lg_ref, lb_ref, pw_ref, ps_ref, cbuf_ref, dbuf_ref,
                     oc_ref, u_ref, od_ref, xc_scr, xd_scr, *, nb, c, pos0):
    ci = pl.program_id(1)

    @pl.when(ci == 0)
    def _():
        xc_scr[:, 0:CPAD, :] = cbuf_ref[...]
        xd_scr[:, 0:DPAD, :] = dbuf_ref[...]

    lane = _iota((c, GW), 1)
    wl = jnp.where(lane < 64, 2.0, jnp.where(lane < 128, 4.0, jnp.where(lane < 192, 8.0, 16.0)))
    pos = (_iota((c, GW), 0) + (ci * c + pos0 + 1)).astype(F32)
    cnt = jnp.minimum(wl, pos)
    for b in range(nb):
        u = za_ref[b] * _sigmoid(zb_ref[b])
        u_ref[b] = u
        xc_scr[b, CPAD:CPAD + c, :] = u
        y = dw_ref[C_CONV - 1:C_CONV, :] * u
        for j in range(C_CONV - 1):
            y = y + dw_ref[j:j + 1, :] * xc_scr[b, 2 + j:2 + j + c, :]
        xc_scr[b, 0:CPAD, :] = xc_scr[b, c:c + CPAD, :]
        y = y + db_ref[...]
        mu = jnp.mean(y, axis=-1, keepdims=True)
        yc = y - mu
        var = jnp.mean(yc * yc, axis=-1, keepdims=True)
        oc_ref[b] = _silu(yc * lax.rsqrt(var + EPS) * lg_ref[...] + lb_ref[...])
        x = zd_ref[b]
        xd_scr[b, DPAD:DPAD + c, :] = x
        acc = x
        sums = {}
        for i in range(1, 16):
            acc = acc + xd_scr[b, DPAD - i:DPAD - i + c, :]
            if i + 1 in POOL_WINDOWS:
                sums[i + 1] = acc
        xd_scr[b, 0:DPAD, :] = xd_scr[b, c:c + DPAD, :]
        ssel = jnp.where(lane < 64, sums[2], jnp.where(lane < 128, sums[4], jnp.where(lane < 192, sums[8], sums[16])))
        diff = ssel / cnt - x
        od_ref[b] = _bdot(diff, pw_ref[...]) * ps_ref[...]


def _convpool(z3, dw, db, lg, lb, pwbd, ps, cbuf, dbuf, nb, c, pos0):
    bsz, l, _ = z3.shape
    zblk = lambda j: pl.BlockSpec((nb, c, GW), lambda b, i, j=j: (b, i, j))
    full = lambda r, w: pl.BlockSpec((r, w), lambda b, i: (0, 0))
    oblk = pl.BlockSpec((nb, c, GW), lambda b, i: (b, i, 0))
    osh = jax.ShapeDtypeStruct((bsz, l, GW), F32)
    return pl.pallas_call(
        functools.partial(_convpool_kernel, nb=nb, c=c, pos0=pos0),
        out_shape=(osh, osh, osh),
        grid=(bsz // nb, l // c),
        in_specs=[zblk(ZB_CA), zblk(ZB_CB), zblk(ZB_D), full(C_CONV, GW), full(1, GW), full(1, GW), full(1, GW),
                  full(GW, GW), full(1, GW),
                  pl.BlockSpec((nb, CPAD, GW), lambda b, i: (b, 0, 0)),
                  pl.BlockSpec((nb, DPAD, GW), lambda b, i: (b, 0, 0))],
        out_specs=(oblk, oblk, oblk),
        scratch_shapes=[pltpu.VMEM((nb, CPAD + c, GW), F32), pltpu.VMEM((nb, DPAD + c, GW), F32)],
        compiler_params=_cparams(("parallel", "arbitrary")),
        name="convpool",
    )(z3, z3, z3, dw, db, lg, lb, pwbd, ps, cbuf, dbuf)


NKH = 2 * PEER_HEADS * PEER_NKEYS
_CANDS = [(a, b) for a in range(PEER_TOPK) for b in range(PEER_TOPK) if (a + 1) * (b + 1) <= PEER_TOPK]


def _oddeven_merge_sort_pairs(n):
    pairs = []

    def merge(lo, hi, r):
        step = r * 2
        if step < hi - lo:
            merge(lo, hi, step)
            merge(lo + r, hi, step)
            pairs.extend((i, i + r) for i in range(lo + r, hi - r, step))
        else:
            pairs.append((lo, lo + r))

    def sort(lo, hi):
        if hi - lo >= 1:
            mid = lo + (hi - lo) // 2
            sort(lo, mid)
            sort(mid + 1, hi)
            merge(lo, hi, 1)

    sort(0, n - 1)
    return pairs


_SORT16 = _oddeven_merge_sort_pairs(PEER_NKEYS // 8)


def _peer1_kernel(h_ref, g_ref, wq_ref, kbd_ref, xn_ref, e1_ref, c_ref, e2_ref, r2_ref, s_scr, sv_scr, d_scr, zi_scr,
                  *, nlt):
    xn = _rms(h_ref[...], g_ref[...]).T.astype(BF16)
    xn_ref[...] = pltpu.bitcast(xn, jnp.uint32)
    qt = jnp.dot(_rows_bf16(wq_ref[...]), xn, preferred_element_type=F32).astype(BF16)
    for hp in range(2 * PEER_HEADS):
        sc = jnp.dot(kbd_ref[hp], qt[hp * PEER_HALF:(hp + 1) * PEER_HALF, :], preferred_element_type=F32)
        for lt in range(nlt):
            s_scr[lt, hp * PEER_NKEYS:(hp + 1) * PEER_NKEYS, :] = sc[:, lt * LANES:(lt + 1) * LANES]

    def lane_tile(lt, carry):
        for h in range(PEER_HEADS):
            for p in range(2):
                s = s_scr[lt, (2 * h + p) * PEER_NKEYS:(2 * h + p + 1) * PEER_NKEYS, :]
                lists = [s[8 * i:8 * (i + 1), :] for i in range(PEER_NKEYS // 8)]
                for i, j in _SORT16:
                    lists[i], lists[j] = jnp.maximum(lists[i], lists[j]), jnp.minimum(lists[i], lists[j])
                svs = []
                for a in range(PEER_TOPK):
                    m = jnp.max(lists[0], axis=0, keepdims=True)
                    sv_scr[p, a, h:h + 1, :] = m
                    svs.append(m)
                    popped = lists[0] == m
                    for i in range(PEER_TOPK - 1 - a):
                        lists[i] = jnp.where(popped, lists[i + 1], lists[i])
                if p == 1:
                    rank = jnp.full(s.shape, float(PEER_TOPK), F32)
                    for a in reversed(range(PEER_TOPK)):
                        rank = jnp.where(s >= svs[a], float(a), rank)
                    r2_ref[lt, h] = pltpu.bitcast(rank.astype(BF16), jnp.uint32)
                    e2_ref[lt, h] = pltpu.bitcast(jnp.exp(s - svs[0]).astype(BF16), jnp.uint32)
        sv1 = [sv_scr[0, a] for a in range(PEER_TOPK)]
        sv2 = [sv_scr[1, a] for a in range(PEER_TOPK)]
        vals = [sv1[a] + sv2[b] for a, b in _CANDS]
        n = len(_CANDS)
        before = [jnp.zeros((PEER_HEADS, LANES), F32) for _ in range(n)]
        for i in range(n):
            ai, bi = _CANDS[i]
            for j in range(i + 1, n):
                aj, bj = _CANDS[j]
                if ai <= aj and bi <= bj:
                    before[j] = before[j] + 1.0
                else:
                    t = jnp.where(vals[i] >= vals[j], 1.0, 0.0)
                    before[j] = before[j] + t
                    before[i] = before[i] + (1.0 - t)
        ex1 = [jnp.exp(sv1[a] - sv1[0]) for a in range(PEER_TOPK)]
        ex2 = [jnp.exp(sv2[b] - sv2[0]) for b in range(PEER_TOPK)]
        cnt = [jnp.zeros((PEER_HEADS, LANES), F32) for _ in range(PEER_TOPK)]
        zsum = jnp.zeros((PEER_HEADS, LANES), F32)
        for i, (a, b) in enumerate(_CANDS):
            sel = jnp.where(before[i] < float(PEER_TOPK), 1.0, 0.0)
            cnt[a] = cnt[a] + sel
            zsum = zsum + sel * ex1[a] * ex2[b]
        for a in range(PEER_TOPK):
            d_scr[a] = cnt[a]
        zi_scr[...] = 1.0 / zsum
        for h in range(PEER_HEADS):
            s = s_scr[lt, 2 * h * PEER_NKEYS:(2 * h + 1) * PEER_NKEYS, :]
            cc = jnp.zeros(s.shape, F32)
            for a in reversed(range(PEER_TOPK)):
                cc = jnp.where(s >= sv_scr[0, a, h:h + 1, :], d_scr[a, h:h + 1, :], cc)
            c_ref[lt, h] = cc
            e1_ref[lt, h] = jnp.exp(s - sv_scr[0, 0, h:h + 1, :]) * zi_scr[h:h + 1, :]
        return carry

    lax.fori_loop(0, nlt, lane_tile, 0)


def _peer1(h, g, wq_bf16, kbd_bf16, tt):
    t = h.shape[0]
    nlt = tt // LANES
    gsh = jax.ShapeDtypeStruct((t // LANES, PEER_HEADS, PEER_NKEYS, LANES), F32)
    gblk = pl.BlockSpec((nlt, PEER_HEADS, PEER_NKEYS, LANES), lambda i: (i, 0, 0, 0))
    psh = jax.ShapeDtypeStruct((t // LANES, PEER_HEADS, PEER_NKEYS // 2, LANES), jnp.uint32)
    pblk = pl.BlockSpec((nlt, PEER_HEADS, PEER_NKEYS // 2, LANES), lambda i: (i, 0, 0, 0))
    return pl.pallas_call(
        functools.partial(_peer1_kernel, nlt=nlt),
        out_shape=(jax.ShapeDtypeStruct((D_MODEL // 2, t), jnp.uint32), gsh, gsh, psh, psh),
        grid=(t // tt,),
        in_specs=[pl.BlockSpec((tt, D_MODEL), lambda i: (i, 0)),
                  pl.BlockSpec((1, D_MODEL), lambda i: (0, 0)),
                  pl.BlockSpec((D_MODEL // 2, D_MODEL), lambda i: (0, 0)),
                  pl.BlockSpec((2 * PEER_HEADS, PEER_NKEYS, PEER_HALF), lambda i: (0, 0, 0))],
        out_specs=(pl.BlockSpec((D_MODEL // 2, tt), lambda i: (0, i)), gblk, gblk, pblk, pblk),
        scratch_shapes=[pltpu.VMEM((nlt, NKH, LANES), F32),
                        pltpu.VMEM((2, PEER_TOPK, PEER_HEADS, LANES), F32),
                        pltpu.VMEM((PEER_TOPK, PEER_HEADS, LANES), F32),
                        pltpu.VMEM((PEER_HEADS, LANES), F32)],
        compiler_params=_cparams(("parallel",)),
        name="peer1",
    )(h, g, wq_bf16, kbd_bf16)


NE1 = 8
ETILE = NE1 * PEER_NKEYS


def _peer2_step(u_ref, xn_ref, vt_ref, e1_ref, c_ref, e2_ref, r2_ref, acc_scr, act_w, act_r, w_w, w_r, nlt,
                do_value=True, do_gate=True, do_act=True):
    tile = (PEER_NKEYS, LANES)
    zero = jnp.zeros(tile, BF16)
    grp = 2
    nmb = 4
    mblk = D_MODEL // nmb

    def value_rows(m):
        rs = slice(m * mblk, (m + 1) * mblk)
        vrows = _rows_bf16(vt_ref[m * mblk // 2:(m + 1) * mblk // 2, :])
        acc_scr[rs, :] += jnp.dot(vrows, _rows_bf16(w_r[...]), preferred_element_type=F32)

    def act_rows(m):
        urows = _rows_bf16(u_ref[m * mblk // 2:(m + 1) * mblk // 2, :])
        a = jnp.dot(urows, _rows_bf16(xn_ref[...]), preferred_element_type=F32).astype(BF16)
        a = 0.5 * a * (1.0 + lax.erf(a * 0.7071067811865476))
        act_w[m * mblk // 2:(m + 1) * mblk // 2, :] = pltpu.bitcast(a, jnp.uint32)

    def gate_group(lt, g0):
        ls = slice(lt * LANES, (lt + 1) * LANES)
        gates = [zero] * grp
        for h in range(PEER_HEADS):
            rank = pltpu.bitcast(r2_ref[lt, h], BF16)
            wkey = pltpu.bitcast(e2_ref[lt, h], BF16)
            for j in range(grp):
                i1 = g0 + j
                cnt = jnp.broadcast_to(c_ref[lt, h, i1:i1 + 1, :], tile).astype(BF16)
                wgt = jnp.broadcast_to(e1_ref[lt, h, i1:i1 + 1, :], tile).astype(BF16)
                gates[j] = gates[j] + jnp.where(rank < cnt, wkey, zero) * wgt
        for j in range(grp):
            ps = slice((g0 + j) * PEER_NKEYS // 2, (g0 + j + 1) * PEER_NKEYS // 2)
            w_w[ps, ls] = pltpu.bitcast(gates[j] * pltpu.bitcast(act_r[ps, ls], BF16), jnp.uint32)

    groups = [(lt, g0) for lt in range(nlt) for g0 in range(0, NE1, grp)] if do_gate else []
    mxu_work = [f for m in range(nmb) for f, on in ((functools.partial(value_rows, m), do_value),
                                                  (functools.partial(act_rows, m), do_act)) if on]
    if not mxu_work:
        mxu_work = [lambda: None]
    per = max(1, len(groups) // len(mxu_work))
    gi = 0
    for k, mm in enumerate(mxu_work):
        mm()
        take = len(groups) - gi if k == len(mxu_work) - 1 else per
        for _ in range(take):
            if gi < len(groups):
                gate_group(*groups[gi])
                gi += 1


def _peer2_kernel(h_ref, xn_ref, u_ref, vt_ref, e1_ref, c_ref, e2_ref, r2_ref, o_ref, acc_scr, act0, act1, w0, w1, *,
                  nlt):
    e = pl.program_id(1)

    last = pl.num_programs(1) - 1
    args = (u_ref, xn_ref, vt_ref, e1_ref, c_ref, e2_ref, r2_ref, acc_scr)
    even = (act0, act1, w1, w0)
    odd = (act1, act0, w0, w1)
    steady = jnp.logical_and(e >= 2, e <= last - 2)

    @pl.when(e == 0)
    def _():
        acc_scr[...] = jnp.zeros(acc_scr.shape, F32)
        _peer2_step(*args, *even, nlt, do_value=False, do_gate=False)

    @pl.when(e == 1)
    def _():
        _peer2_step(*args, *odd, nlt, do_value=False)

    @pl.when(jnp.logical_and(steady, e % 2 == 0))
    def _():
        _peer2_step(*args, *even, nlt)

    @pl.when(jnp.logical_and(steady, e % 2 == 1))
    def _():
        _peer2_step(*args, *odd, nlt)

    @pl.when(e == last - 1)
    def _():
        _peer2_step(*args, *even, nlt, do_act=False)

    @pl.when(e == last)
    def _():
        _peer2_step(*args, *odd, nlt, do_act=False, do_gate=False)
        o_ref[...] = h_ref[...] + acc_scr[...].T


def _peer2(h, xn, u_bf16, vt_bf16, e1, cc, e2, r2, tt):
    t = h.shape[0]
    nlt = tt // LANES
    ne = PEER_NKEYS // NE1
    assert ne % 2 == 0
    last = ne - 1
    pfull = pl.BlockSpec((nlt, PEER_HEADS, PEER_NKEYS // 2, LANES), lambda i, e: (i, 0, 0, 0))
    gtile = pl.BlockSpec((nlt, PEER_HEADS, NE1, LANES), lambda i, e: (i, 0, jnp.clip(e - 1, 0, last), 0))
    slot = pltpu.VMEM((ETILE // 2, tt), jnp.uint32)
    return pl.pallas_call(
        functools.partial(_peer2_kernel, nlt=nlt),
        out_shape=jax.ShapeDtypeStruct((t, D_MODEL), F32),
        grid=(t // tt, ne + 2),
        in_specs=[pl.BlockSpec((tt, D_MODEL), lambda i, e: (i, 0)),
                  pl.BlockSpec((D_MODEL // 2, tt), lambda i, e: (0, i)),
                  pl.BlockSpec((ETILE // 2, D_MODEL), lambda i, e: (jnp.minimum(e, last), 0)),
                  pl.BlockSpec((D_MODEL // 2, ETILE), lambda i, e: (0, jnp.clip(e - 2, 0, last))),
                  gtile, gtile, pfull, pfull],
        out_specs=pl.BlockSpec((tt, D_MODEL), lambda i, e: (i, 0)),
        scratch_shapes=[pltpu.VMEM((D_MODEL, tt), F32), slot, slot, slot, slot],
        compiler_params=_cparams(("parallel", "arbitrary")),
        name="peer2",
    )(h, xn, u_bf16, vt_bf16, e1, cc, e2, r2)


def _prep_layer(i, w_in, lbs, hgrn_norm, gdn_conv_w, gdn_a_log, gdn_dt_bias, gdn_norm, conf_dw_w, conf_dw_b,
                conf_ln_g, conf_ln_b, pool_w, pool_scale, w_out, peer_wq, peer_keys, peer_u, peer_v, ple_gate_w,
                ple_proj_w):
    wi = w_in[i]
    hk = HEADS * DH
    c0 = 4 * hk
    qkv = wi[:, c0:c0 + 3 * hk]
    zbg = wi[:, c0 + 3 * hk:c0 + 4 * hk]
    zbb = wi[:, c0 + 4 * hk:c0 + 4 * hk + HEADS]
    zba = wi[:, c0 + 4 * hk + HEADS:c0 + 4 * hk + 2 * HEADS]
    rest = wi[:, c0 + 4 * hk + 2 * HEADS:]
    w_perm = jnp.concatenate([wi[:, :c0], qkv, zbg, rest, jnp.repeat(zbb, DH, axis=1), jnp.repeat(zba, DH, axis=1)],
                             axis=1)
    kbd = peer_keys[i].reshape(2 * PEER_HEADS, PEER_NKEYS, PEER_HALF).astype(BF16)
    pw = pool_w[i]
    eye4 = jnp.eye(4, dtype=F32)
    pwbd = (pw[:, :, None, :] * eye4[:, None, :, None]).reshape(GW, GW)
    return dict(
        w_in=_pack_rows(w_perm), lb=lbs[i][None, :], hgrn_norm=hgrn_norm[i][None, :],
        gdn_conv_w=gdn_conv_w[i], gdn_a_log=jnp.repeat(gdn_a_log[i], DH)[None, :],
        gdn_dt_bias=jnp.repeat(gdn_dt_bias[i], DH)[None, :], gdn_norm=gdn_norm[i][None, :],
        conf_dw_w=conf_dw_w[i], conf_dw_b=conf_dw_b[i][None, :], conf_ln_g=conf_ln_g[i][None, :],
        conf_ln_b=conf_ln_b[i][None, :], pool_w=pwbd, pool_scale=pool_scale[i][None, :],
        w_out=_pack_rows(w_out, layer=i), peer_wq=_pack_rows(peer_wq, transpose=True, layer=i), kbd=kbd,
        peer_u=_pack_rows(peer_u, layer=i), peer_vt=_pack_rows(peer_v, transpose=True, layer=i),
        ple_gate_w=_pack_rows(ple_gate_w, layer=i), ple_proj_w=_pack_rows(ple_proj_w, layer=i))


def _tail(buf, x, n):
    l = x.shape[1]
    if l >= n:
        return x[:, l - n:]
    return jnp.concatenate([buf[:, l:], x], axis=1)


def _trunk(x, p, pos0, s_a, s_b, c_b, c_c, c_d, layers, norms, norm_final):
    bsz, l, _ = x.shape
    t = bsz * l
    prompt = l >= 64
    c = 64 if prompt else l
    nb = 1 if prompt else 8
    cc = min(256, l)
    tm = min(512, t)
    tt = min(512, t)
    h = x.reshape(t, D_MODEL)
    n_a, n_b, nb_b, nb_c, nb_d = [], [], [], [], []
    for i, lw in enumerate(layers):
        z = _inproj(h, norms['mix'][i][None, :], lw['w_in'], tm)
        z3 = z.reshape(bsz, l, ZW)
        eye4 = jnp.eye(HEADS, dtype=F32)
        s0t = (jnp.swapaxes(s_a[i], 2, 3)[:, :, :, None, :] * eye4[None, :, None, :, None]).reshape(bsz, GW, GW)
        o_a, st = _hgrn(z3, lw['lb'], lw['hgrn_norm'], s0t, min(8, bsz) if prompt else nb, min(c, 32))
        st = st.reshape(bsz, HEADS, DH, HEADS, DH)
        n_a.append(jnp.stack([jnp.swapaxes(st[:, j, :, j, :], 1, 2) for j in range(HEADS)], axis=1))
        buf8 = jnp.pad(c_b[i], ((0, 0), (8 - (B_CONV - 1), 0), (0, 0)))
        o_b, sb = _gdn(z3, lw['gdn_conv_w'], lw['gdn_a_log'], lw['gdn_dt_bias'], lw['gdn_norm'], buf8, s_b[i],
                       min(4, bsz) if prompt else nb, c)
        n_b.append(sb)
        nb_b.append(_tail(c_b[i], z3[:, :, ZB_BQ * GW:(ZB_BV + 1) * GW], B_CONV - 1))
        cbuf = jnp.pad(c_c[i], ((0, 0), (CPAD - (C_CONV - 1), 0), (0, 0)))
        dbuf = jnp.pad(c_d[i], ((0, 0), (DPAD - POOL_BUF, 0), (0, 0)))
        o_c, u_c, o_d = _convpool(z3, lw['conf_dw_w'], lw['conf_dw_b'], lw['conf_ln_g'], lw['conf_ln_b'],
                                  lw['pool_w'], lw['pool_scale'], cbuf, dbuf, nb, cc, pos0)
        nb_c.append(_tail(c_c[i], u_c, C_CONV - 1))
        nb_d.append(_tail(c_d[i], z3[:, :, ZB_D * GW:(ZB_D + 1) * GW], POOL_BUF))
        h = _outproj(h, o_a.reshape(t, GW), o_b.reshape(t, GW), o_c.reshape(t, GW), o_d.reshape(t, GW),
                     lw['w_out'], tm)
        xn, e1, cnt, e2, r2 = _peer1(h, norms['ffn'][i][None, :], lw['peer_wq'], lw['kbd'], tt)
        h = _peer2(h, xn, lw['peer_u'], lw['peer_vt'], e1, cnt, e2, r2, min(1024, t))
        h = _ple(h, p.reshape(p.shape[0], t, D_PLE), i, norms['ple'][i][None, :], lw['ple_gate_w'], lw['ple_proj_w'],
                 norm_final[None, :], tm, final=(i == len(layers) - 1))
    st = lambda lst: jnp.stack(lst, axis=0)
    return h.reshape(bsz, l, D_MODEL), st(n_a), st(n_b), st(nb_b), st(nb_c), st(nb_d)


def kernel(x_prompt, x_sample, state_hgrn, state_gdn, state_gdn_conv, state_conf_conv, state_pool, p_prompt, p_sample, norm_mix, w_in, hgrn_lb_logits, hgrn_norm, gdn_conv_w, gdn_a_log, gdn_dt_bias, gdn_norm, conf_dw_w, conf_dw_b, conf_ln_g, conf_ln_b, pool_w, pool_scale, w_out, norm_ffn, peer_wq, peer_keys, peer_u, peer_v, norm_ple, ple_gate_w, ple_proj_w, norm_final):
    sm = jax.nn.softmax(hgrn_lb_logits.astype(F32), axis=0)
    lbs = jnp.cumsum(sm, axis=0) - sm[0:1]
    layers = [_prep_layer(i, w_in, lbs, hgrn_norm, gdn_conv_w, gdn_a_log, gdn_dt_bias, gdn_norm, conf_dw_w,
                          conf_dw_b, conf_ln_g, conf_ln_b, pool_w, pool_scale, w_out, peer_wq, peer_keys, peer_u,
                          peer_v, ple_gate_w, ple_proj_w) for i in range(DEPTH)]
    norms = dict(mix=norm_mix, ffn=norm_ffn, ple=norm_ple)
    bp = x_prompt.shape[0]
    z = lambda *s: jnp.zeros((DEPTH, bp) + s, F32)
    yp, hp, gp, gcp, ccp, pp = _trunk(x_prompt, p_prompt, 0, z(HEADS, DH, DH), z(HEADS, DH, DH),
                                      z(B_CONV - 1, 3 * GW), z(C_CONV - 1, GW), z(POOL_BUF, GW),
                                      layers, norms, norm_final)
    ys, hs, gs, gcs, ccs, ps = _trunk(x_sample, p_sample, PAST_LEN, state_hgrn, state_gdn, state_gdn_conv,
                                      state_conf_conv, state_pool, layers, norms, norm_final)
    return (yp, ys, hp, gp, gcp, ccp, pp, hs, gs, gcs, ccs, ps)
```

```python
import functools

import numpy as np
import jax
import jax.numpy as jnp
from jax import lax
from jax.experimental import pallas as pl
from jax.experimental.pallas import tpu as pltpu

F32 = jnp.float32
BF16 = jnp.bfloat16

D_MODEL = 1024
DEPTH = 4
PAST_LEN = 16384
EPS = 1e-6
NEG_BIG = -1e30
D_PLE = 256
HEADS = 4
DH = 64
GW = 256
B_CONV = 4
C_CONV = 31
POOL_WINDOWS = (2, 4, 8, 16)
POOL_BUF = 15
PEER_HEADS = 8
PEER_NKEYS = 128
PEER_HALF = 64
PEER_TOPK = 16

ZB_AQ, ZB_AF, ZB_AI, ZB_AG, ZB_BQ, ZB_BK, ZB_BV, ZB_BG, ZB_CA, ZB_CB, ZB_D, ZB_BB, ZB_BA = range(13)
ZW = 13 * GW

LANES = 128
SOLVE_BLOCK = 16
VMEM_LIMIT = 56 * 1024 * 1024


def _cparams(sem, flags=None):
    return pltpu.CompilerParams(dimension_semantics=sem, vmem_limit_bytes=VMEM_LIMIT, flags=flags)


def _bdot(a, b):
    return jnp.dot(a.astype(BF16), b.astype(BF16), preferred_element_type=F32)


def _bdot_nt(a, b):
    return lax.dot_general(a.astype(BF16), b.astype(BF16), (((1,), (1,)), ((), ())),
                           preferred_element_type=F32)


def _split3(a):
    a1 = a.astype(BF16)
    r1 = a - a1.astype(F32)
    a2 = r1.astype(BF16)
    a3 = (r1 - a2.astype(F32)).astype(BF16)
    return a1, a2, a3


def _xdot(a, m01):
    a1, a2, a3 = _split3(a)
    m = m01.astype(BF16)
    return (jnp.dot(a1, m, preferred_element_type=F32) + jnp.dot(a2, m, preferred_element_type=F32)
            + jnp.dot(a3, m, preferred_element_type=F32))


def _xdot_left(m01, a):
    a1, a2, a3 = _split3(a)
    m = m01.astype(BF16)
    return (jnp.dot(m, a1, preferred_element_type=F32) + jnp.dot(m, a2, preferred_element_type=F32)
            + jnp.dot(m, a3, preferred_element_type=F32))


def _dot2(a, b):
    ah = a.astype(BF16)
    al = (a - ah.astype(F32)).astype(BF16)
    bh = b.astype(BF16)
    bl = (b - bh.astype(F32)).astype(BF16)
    return (jnp.dot(ah, bh, preferred_element_type=F32) + jnp.dot(ah, bl, preferred_element_type=F32)
            + jnp.dot(al, bh, preferred_element_type=F32))


def _xtranspose(a, eye):
    a1, a2, a3 = _split3(a)
    e = eye.astype(BF16)
    dn = (((1,), (1,)), ((), ()))
    return (lax.dot_general(e, a1, dn, preferred_element_type=F32)
            + lax.dot_general(e, a2, dn, preferred_element_type=F32)
            + lax.dot_general(e, a3, dn, preferred_element_type=F32))


def _pack_kernel(x_ref, o_ref, *, transpose):
    x = x_ref[...]
    if transpose:
        x = x.T
    o_ref[...] = pltpu.bitcast(x.astype(BF16), jnp.uint32)


def _pack_rows(w, transpose=False, layer=None):
    r, c = w.shape[-2:]
    rb = min(r, 1024)
    if layer is None:
        in_spec = pl.BlockSpec((rb, c), lambda i: (i, 0))
    else:
        in_spec = pl.BlockSpec((None, rb, c), lambda i: (layer, i, 0))
    if transpose:
        out_shape, out_spec = (c // 2, r), pl.BlockSpec((c // 2, rb), lambda i: (0, i))
    else:
        out_shape, out_spec = (r // 2, c), pl.BlockSpec((rb // 2, c), lambda i: (i, 0))
    return pl.pallas_call(
        functools.partial(_pack_kernel, transpose=transpose),
        out_shape=jax.ShapeDtypeStruct(out_shape, jnp.uint32),
        grid=(r // rb,),
        in_specs=[in_spec],
        out_specs=out_spec,
        compiler_params=_cparams(("parallel",)),
        name="pack_t" if transpose else "pack",
    )(w)


def _rows_bf16(words):
    return pltpu.bitcast(words, BF16)


def _iota(shape, axis):
    return lax.broadcasted_iota(jnp.int32, shape, axis)


def _tri_incl(c):
    return jnp.where(_iota((c, c), 1) <= _iota((c, c), 0), 1.0, 0.0).astype(F32)


def _eye(n):
    return jnp.where(_iota((n, n), 0) == _iota((n, n), 1), 1.0, 0.0).astype(F32)


def _block_ones(n, blk):
    return jnp.where(_iota((n, n), 0) // blk == _iota((n, n), 1) // blk, 1.0, 0.0).astype(F32)


def _rms(x, g):
    return x * lax.rsqrt(jnp.mean(x * x, axis=-1, keepdims=True) + EPS) * g


def _sigmoid(x):
    return jax.nn.sigmoid(x)


def _silu(x):
    return x * jax.nn.sigmoid(x)


def _inproj_kernel(x_ref, g_ref, w_ref, o_ref):
    xn = _rms(x_ref[...], g_ref[...])
    o_ref[...] = jnp.dot(xn.astype(BF16), _rows_bf16(w_ref[...]), preferred_element_type=F32)


def _inproj(h, g, w_bf16, tm):
    t = h.shape[0]
    return pl.pallas_call(
        _inproj_kernel,
        out_shape=jax.ShapeDtypeStruct((t, ZW), F32),
        grid=(t // tm,),
        in_specs=[pl.BlockSpec((tm, D_MODEL), lambda i: (i, 0)),
                  pl.BlockSpec((1, D_MODEL), lambda i: (0, 0)),
                  pl.BlockSpec((D_MODEL // 2, ZW), lambda i: (0, 0))],
        out_specs=pl.BlockSpec((tm, ZW), lambda i: (i, 0)),
        compiler_params=_cparams(("parallel",)),
        name="inproj",
    )(h, g, w_bf16)


def _outproj_kernel(h_ref, a_ref, b_ref, c_ref, d_ref, w_ref, o_ref):
    acc = h_ref[...]
    for i, r in enumerate((a_ref, b_ref, c_ref, d_ref)):
        acc = acc + jnp.dot(r[...].astype(BF16), _rows_bf16(w_ref[i * GW // 2:(i + 1) * GW // 2, :]),
                            preferred_element_type=F32)
    o_ref[...] = acc


def _outproj(h, oa, ob, oc, od, w_bf16, tm):
    t = h.shape[0]
    tok = lambda w: pl.BlockSpec((tm, w), lambda i: (i, 0))
    return pl.pallas_call(
        _outproj_kernel,
        out_shape=jax.ShapeDtypeStruct((t, D_MODEL), F32),
        grid=(t // tm,),
        in_specs=[tok(D_MODEL), tok(GW), tok(GW), tok(GW), tok(GW),
                  pl.BlockSpec((D_MODEL // 2, D_MODEL), lambda i: (0, 0))],
        out_specs=tok(D_MODEL),
        compiler_params=_cparams(("parallel",)),
        name="outproj",
    )(h, oa, ob, oc, od, w_bf16)


def _ple_kernel(h_ref, p_ref, g_ref, wg_ref, wp_ref, gf_ref, o_ref, *, final):
    h = h_ref[...]
    xn = _rms(h, g_ref[...])
    gate = _sigmoid(jnp.dot(xn.astype(BF16), _rows_bf16(wg_ref[...]), preferred_element_type=F32))
    proj = jnp.dot(p_ref[...].astype(BF16), _rows_bf16(wp_ref[...]), preferred_element_type=F32)
    out = h + gate * proj
    if final:
        out = _rms(out, gf_ref[...])
    o_ref[...] = out


def _ple(h, p, layer, g, wg_bf16, wp_bf16, gfinal, tm, final):
    t = h.shape[0]
    tok = lambda w: pl.BlockSpec((tm, w), lambda i: (i, 0))
    ptok = pl.BlockSpec((None, tm, D_PLE), lambda i: (layer, i, 0))
    full = lambda r, c: pl.BlockSpec((r, c), lambda i: (0, 0))
    return pl.pallas_call(
        functools.partial(_ple_kernel, final=final),
        out_shape=jax.ShapeDtypeStruct((t, D_MODEL), F32),
        grid=(t // tm,),
        in_specs=[tok(D_MODEL), ptok, full(1, D_MODEL), full(D_MODEL // 2, D_MODEL), full(D_PLE // 2, D_MODEL),
                  full(1, D_MODEL)],
        out_specs=tok(D_MODEL),
        compiler_params=_cparams(("parallel",)),
        name="ple_final" if final else "ple",
    )(h, p, g, wg_bf16, wp_bf16, gfinal)


def _hgrn_kernel(z_ref, lb_ref, nw_ref, s0_ref, o_ref, sf_ref, st_scr, la_scr, q_scr, k_scr, v_scr, p_scr, *, nb, c):
    ci = pl.program_id(1)

    @pl.when(ci == 0)
    def _():
        st_scr[...] = s0_ref[...]

    lb = lb_ref[...]
    bones = _block_ones(GW, DH)
    tri = _tri_incl(c)
    rows = _iota((c, GW), 0)
    seqs = range(nb)
    for b in seqs:
        zf = z_ref[b, :, 1 * GW:2 * GW]
        q_scr[b] = _silu(z_ref[b, :, 0 * GW:1 * GW])
        k_scr[b] = (1.0 - lb) * _sigmoid(-zf)
        v_scr[b] = z_ref[b, :, 2 * GW:3 * GW]
        f = lb + (1.0 - lb) * _sigmoid(zf)
        la_scr[b] = _xdot_left(tri, jnp.log(f))

    def build(s, carry):
        for b in seqs:
            la_s = la_scr[b, pl.ds(s, 1), :]
            k_s = k_scr[b, pl.ds(s, 1), :]
            dec = jnp.exp(jnp.minimum(la_scr[b] - la_s, 0.0))
            p_scr[b, pl.ds(pl.multiple_of(s * c, c), c), :] = jnp.where(rows >= s, q_scr[b] * k_s * dec, 0.0)
        return carry

    lax.fori_loop(0, c, build, 0, unroll=min(4, c))
    for b in seqs:
        p_scr[b] = jnp.dot(p_scr[b].astype(BF16), bones.astype(BF16), preferred_element_type=F32)

    def consume(s, accs):
        return tuple(acc + p_scr[b, pl.ds(pl.multiple_of(s * c, c), c), :] * v_scr[b, pl.ds(s, 1), :]
                     for b, acc in zip(seqs, accs))

    os_ = lax.fori_loop(0, c, consume, tuple(jnp.zeros((c, GW), F32) for _ in seqs), unroll=min(4, c))
    sts = [st_scr[b] for b in seqs]
    las = [la_scr[b] for b in seqs]
    os_ = [o + _bdot_nt(q_scr[b] * jnp.exp(la), st) for b, o, la, st in zip(seqs, os_, las, sts)]
    upds = []
    for b, la in zip(seqs, las):
        k_dec = k_scr[b] * jnp.exp(la[c - 1:c, :] - la)
        upds.append(lax.dot_general(v_scr[b].astype(BF16), k_dec.astype(BF16), (((0,), (0,)), ((), ())),
                                    preferred_element_type=F32))
    for b, la, st, upd in zip(seqs, las, sts, upds):
        st_scr[b] = st * jnp.exp(la[c - 1:c, :]) + upd * bones
    mss = [_xdot(o * o, bones) * (1.0 / DH) for o in os_]
    for b, o, ms in zip(seqs, os_, mss):
        o_ref[b] = o * lax.rsqrt(ms + EPS) * nw_ref[...] * _sigmoid(z_ref[b, :, 3 * GW:4 * GW])

    @pl.when(ci == pl.num_programs(1) - 1)
    def _():
        sf_ref[...] = st_scr[...]


def _hgrn(z3, lb, nw, s0t, nb, c):
    bsz, l, _ = z3.shape
    return pl.pallas_call(
        functools.partial(_hgrn_kernel, nb=nb, c=c),
        out_shape=(jax.ShapeDtypeStruct((bsz, l, GW), F32), jax.ShapeDtypeStruct((bsz, GW, GW), F32)),
        grid=(bsz // nb, l // c),
        in_specs=[pl.BlockSpec((nb, c, 4 * GW), lambda b, i: (b, i, 0)),
                  pl.BlockSpec((1, GW), lambda b, i: (0, 0)),
                  pl.BlockSpec((1, GW), lambda b, i: (0, 0)),
                  pl.BlockSpec((nb, GW, GW), lambda b, i: (b, 0, 0))],
        out_specs=(pl.BlockSpec((nb, c, GW), lambda b, i: (b, i, 0)),
                   pl.BlockSpec((nb, GW, GW), lambda b, i: (b, 0, 0))),
        scratch_shapes=[pltpu.VMEM((nb, GW, GW), F32)] + [pltpu.VMEM((nb, c, GW), F32)] * 4
                       + [pltpu.VMEM((nb, c * c, GW), F32)],
        compiler_params=_cparams(("parallel", "arbitrary")),
        name="hgrn",
    )(z3, lb, nw, s0t)


def _gdn_kernel(zq_ref, zk_ref, zv_ref, zg_ref, zb_ref, za_ref, cw_ref, alog_ref, dtb_ref, nw_ref, buf_ref, s0_ref,
                o_ref, sf_ref, s_scr, xp_scr, *, nb, c):
    ci = pl.program_id(1)

    @pl.when(ci == 0)
    def _():
        s_scr[...] = s0_ref[...]
        xp_scr[:, 0:8, :] = buf_ref[...]

    bones = _block_ones(GW, DH)
    tri = _tri_incl(c)
    eye_c = _eye(c)
    eye_h = _eye(DH)
    tt = _iota((c, c), 0)
    ss = _iota((c, c), 1)
    causal = tt >= ss
    strict = tt > ss
    items = []
    for b in range(nb):
        for j, r in enumerate((zq_ref, zk_ref, zv_ref)):
            xp_scr[b, 8:8 + c, j * GW:(j + 1) * GW] = r[b]
        conv = cw_ref[3:4, :] * xp_scr[b, 8:8 + c, :]
        for j in range(B_CONV - 1):
            conv = conv + cw_ref[j:j + 1, :] * xp_scr[b, 5 + j:5 + j + c, :]
        xp_scr[b, 0:8, :] = xp_scr[b, c:c + 8, :]
        qkv = _silu(conv)
        q = qkv[:, 0:GW]
        k = qkv[:, GW:2 * GW]
        v = qkv[:, 2 * GW:3 * GW]
        q = q * lax.rsqrt(_xdot(q * q, bones) + EPS) * (DH ** -0.5)
        k = k * lax.rsqrt(_xdot(k * k, bones) + EPS)
        beta = _sigmoid(zb_ref[b])
        loga = -jnp.exp(alog_ref[...]) * jax.nn.softplus(za_ref[b] + dtb_ref[...])
        g = _xdot_left(tri, loga)
        for h in range(HEADS):
            hs = slice(h * DH, (h + 1) * DH)
            items.append(dict(b=b, h=h, hs=hs, q=q[:, hs], k=k[:, hs], v=v[:, hs], beta=beta[:, hs], g=g[:, hs]))
    for it in items:
        it['gcol'] = it['g'][:, 0:c]
    for it in items:
        it['grow'] = _xtranspose(it['gcol'], eye_c)
    for it in items:
        it['kk'] = _bdot_nt(it['k'], it['k'])
        it['qk'] = _bdot_nt(it['q'], it['k'])
    for it in items:
        it['dec'] = jnp.exp(jnp.where(causal, it['gcol'] - it['grow'], NEG_BIG))
        it['a'] = jnp.where(strict, it['beta'][:, 0:c] * it['kk'] * it['dec'], 0.0)
        it['r'] = jnp.concatenate([it['beta'] * it['v'], it['beta'] * jnp.exp(it['g']) * it['k']], axis=1)
    blk = min(SOLVE_BLOCK, c)
    for it in items:
        it['solved'] = []
    for j0 in range(0, c, blk):
        rbs = [it['r'][j0:j0 + blk, :] for it in items]
        abs_ = [it['a'][j0:j0 + blk, j0:j0 + blk] for it in items]
        for s in range(blk - 1):
            rbs = [rb - ab[:, s:s + 1] * rb[s:s + 1, :] for rb, ab in zip(rbs, abs_)]
        for it, rb in zip(items, rbs):
            it['solved'].append(rb)
        if j0 + blk < c:
            belows = [_dot2(it['a'][j0 + blk:, j0:j0 + blk], rb) for it, rb in zip(items, rbs)]
            for it, below in zip(items, belows):
                it['r'] = jnp.concatenate([it['r'][:j0 + blk, :], it['r'][j0 + blk:, :] - below], axis=0)
    for it in items:
        w = it['solved'][0] if len(it['solved']) == 1 else jnp.concatenate(it['solved'], axis=0)
        it['w1'] = w[:, 0:DH]
        it['w2'] = w[:, DH:2 * DH]
        it['s'] = s_scr[it['b'], it['h']]
    for it in items:
        it['u'] = it['w1'] - _bdot(it['w2'], it['s'])
    for it in items:
        it['o'] = _bdot(it['q'] * jnp.exp(it['g']), it['s']) + _bdot(it['qk'] * it['dec'], it['u'])
    for it in items:
        g_last = it['g'][c - 1:c, :]
        kd = it['k'] * jnp.exp(g_last - it['g'])
        kd_t = lax.dot_general(eye_h.astype(BF16), kd.astype(BF16), (((1,), (1,)), ((), ())),
                               preferred_element_type=F32)
        it['snew'] = jnp.exp(g_last) * it['s'] + _bdot(kd_t, it['u'])
    for it in items:
        s_scr[it['b'], it['h']] = it['snew']
        o = it['o']
        zg = zg_ref[it['b'], :, it['hs']]
        o = o * lax.rsqrt(jnp.mean(o * o, axis=-1, keepdims=True) + EPS) * nw_ref[...] * _silu(zg)
        o_ref[it['b'], :, it['hs']] = o

    @pl.when(ci == pl.num_programs(1) - 1)
    def _():
        sf_ref[...] = s_scr[...]


def _gdn(z3, cw, alog_x, dtb_x, nw, buf8, s0, nb, c):
    bsz, l, _ = z3.shape
    zblk = lambda j: pl.BlockSpec((nb, c, GW), lambda b, i, j=j: (b, i, j))
    full = lambda r, w: pl.BlockSpec((r, w), lambda b, i: (0, 0))
    return pl.pallas_call(
        functools.partial(_gdn_kernel, nb=nb, c=c),
        out_shape=(jax.ShapeDtypeStruct((bsz, l, GW), F32), jax.ShapeDtypeStruct((bsz, HEADS, DH, DH), F32)),
        grid=(bsz // nb, l // c),
        in_specs=[zblk(ZB_BQ), zblk(ZB_BK), zblk(ZB_BV), zblk(ZB_BG), zblk(ZB_BB), zblk(ZB_BA),
                  full(B_CONV, 3 * GW), full(1, GW), full(1, GW), full(1, DH),
                  pl.BlockSpec((nb, 8, 3 * GW), lambda b, i: (b, 0, 0)),
                  pl.BlockSpec((nb, HEADS, DH, DH), lambda b, i: (b, 0, 0, 0))],
        out_specs=(pl.BlockSpec((nb, c, GW), lambda b, i: (b, i, 0)),
                   pl.BlockSpec((nb, HEADS, DH, DH), lambda b, i: (b, 0, 0, 0))),
        scratch_shapes=[pltpu.VMEM((nb, HEADS, DH, DH), F32), pltpu.VMEM((nb, 8 + c, 3 * GW), F32)],
        compiler_params=_cparams(("parallel", "arbitrary")),
        name="gdn",
    )(z3, z3, z3, z3, z3, z3, cw, alog_x, dtb_x, nw, buf8, s0)


CPAD = 32
DPAD = 16


def _convpool_kernel(za_ref, zb_ref, zd_ref, dw_ref, db_ref, lg_ref, lb_ref, pw_ref, ps_ref, cbuf_ref, dbuf_ref,
                     oc_ref, u_ref, od_ref, xc_scr, xd_scr, *, nb, c, pos0):
    ci = pl.program_id(1)

    @pl.when(ci == 0)
    def _():
        xc_scr[:, 0:CPAD, :] = cbuf_ref[...]
        xd_scr[:, 0:DPAD, :] = dbuf_ref[...]

    lane = _iota((c, GW), 1)
    wl = jnp.where(lane < 64, 2.0, jnp.where(lane < 128, 4.0, jnp.where(lane < 192, 8.0, 16.0)))
    pos = (_iota((c, GW), 0) + (ci * c + pos0 + 1)).astype(F32)
    cnt = jnp.minimum(wl, pos)
    for b in range(nb):
        u = za_ref[b] * _sigmoid(zb_ref[b])
        u_ref[b] = u
        xc_scr[b, CPAD:CPAD + c, :] = u
        y = dw_ref[C_CONV - 1:C_CONV, :] * u
        for j in range(C_CONV - 1):
            y = y + dw_ref[j:j + 1, :] * xc_scr[b, 2 + j:2 + j + c, :]
        xc_scr[b, 0:CPAD, :] = xc_scr[b, c:c + CPAD, :]
        y = y + db_ref[...]
        mu = jnp.mean(y, axis=-1, keepdims=True)
        yc = y - mu
        var = jnp.mean(yc * yc, axis=-1, keepdims=True)
        oc_ref[b] = _silu(yc * lax.rsqrt(var + EPS) * lg_ref[...] + lb_ref[...])
        x = zd_ref[b]
        xd_scr[b, DPAD:DPAD + c, :] = x
        acc = x
        sums = {}
        for i in range(1, 16):
            acc = acc + xd_scr[b, DPAD - i:DPAD - i + c, :]
            if i + 1 in POOL_WINDOWS:
                sums[i + 1] = acc
        xd_scr[b, 0:DPAD, :] = xd_scr[b, c:c + DPAD, :]
        ssel = jnp.where(lane < 64, sums[2], jnp.where(lane < 128, sums[4], jnp.where(lane < 192, sums[8], sums[16])))
        diff = ssel / cnt - x
        od_ref[b] = _bdot(diff, pw_ref[...]) * ps_ref[...]


def _convpool(z3, dw, db, lg, lb, pwbd, ps, cbuf, dbuf, nb, c, pos0):
    bsz, l, _ = z3.shape
    zblk = lambda j: pl.BlockSpec((nb, c, GW), lambda b, i, j=j: (b, i, j))
    full = lambda r, w: pl.BlockSpec((r, w), lambda b, i: (0, 0))
    oblk = pl.BlockSpec((nb, c, GW), lambda b, i: (b, i, 0))
    osh = jax.ShapeDtypeStruct((bsz, l, GW), F32)
    return pl.pallas_call(
        functools.partial(_convpool_kernel, nb=nb, c=c, pos0=pos0),
        out_shape=(osh, osh, osh),
        grid=(bsz // nb, l // c),
        in_specs=[zblk(ZB_CA), zblk(ZB_CB), zblk(ZB_D), full(C_CONV, GW), full(1, GW), full(1, GW), full(1, GW),
                  full(GW, GW), full(1, GW),
                  pl.BlockSpec((nb, CPAD, GW), lambda b, i: (b, 0, 0)),
                  pl.BlockSpec((nb, DPAD, GW), lambda b, i: (b, 0, 0))],
        out_specs=(oblk, oblk, oblk),
        scratch_shapes=[pltpu.VMEM((nb, CPAD + c, GW), F32), pltpu.VMEM((nb, DPAD + c, GW), F32)],
        compiler_params=_cparams(("parallel", "arbitrary")),
        name="convpool",
    )(z3, z3, z3, dw, db, lg, lb, pwbd, ps, cbuf, dbuf)


NKH = 2 * PEER_HEADS * PEER_NKEYS
_CANDS = [(a, b) for a in range(PEER_TOPK) for b in range(PEER_TOPK) if (a + 1) * (b + 1) <= PEER_TOPK]


def _oddeven_merge_sort_pairs(n):
    pairs = []

    def merge(lo, hi, r):
        step = r * 2
        if step < hi - lo:
            merge(lo, hi, step)
            merge(lo + r, hi, step)
            pairs.extend((i, i + r) for i in range(lo + r, hi - r, step))
        else:
            pairs.append((lo, lo + r))

    def sort(lo, hi):
        if hi - lo >= 1:
            mid = lo + (hi - lo) // 2
            sort(lo, mid)
            sort(mid + 1, hi)
            merge(lo, hi, 1)

    sort(0, n - 1)
    return pairs


_SORT16 = _oddeven_merge_sort_pairs(PEER_NKEYS // 8)


def _peer1_kernel(h_ref, g_ref, wq_ref, kbd_ref, xn_ref, e1_ref, c_ref, e2_ref, r2_ref, s_scr, sv_scr, d_scr, zi_scr,
                  *, nlt):
    xn = _rms(h_ref[...], g_ref[...]).T.astype(BF16)
    xn_ref[...] = pltpu.bitcast(xn, jnp.uint32)
    qt = jnp.dot(_rows_bf16(wq_ref[...]), xn, preferred_element_type=F32).astype(BF16)
    for hp in range(2 * PEER_HEADS):
        sc = jnp.dot(kbd_ref[hp], qt[hp * PEER_HALF:(hp + 1) * PEER_HALF, :], preferred_element_type=F32)
        for lt in range(nlt):
            s_scr[lt, hp * PEER_NKEYS:(hp + 1) * PEER_NKEYS, :] = sc[:, lt * LANES:(lt + 1) * LANES]

    def lane_tile(lt, carry):
        for h in range(PEER_HEADS):
            for p in range(2):
                s = s_scr[lt, (2 * h + p) * PEER_NKEYS:(2 * h + p + 1) * PEER_NKEYS, :]
                lists = [s[8 * i:8 * (i + 1), :] for i in range(PEER_NKEYS // 8)]
                for i, j in _SORT16:
                    lists[i], lists[j] = jnp.maximum(lists[i], lists[j]), jnp.minimum(lists[i], lists[j])
                svs = []
                for a in range(PEER_TOPK):
                    m = jnp.max(lists[0], axis=0, keepdims=True)
                    sv_scr[p, a, h:h + 1, :] = m
                    svs.append(m)
                    popped = lists[0] == m
                    for i in range(PEER_TOPK - 1 - a):
                        lists[i] = jnp.where(popped, lists[i + 1], lists[i])
                if p == 1:
                    rank = jnp.full(s.shape, float(PEER_TOPK), F32)
                    for a in reversed(range(PEER_TOPK)):
                        rank = jnp.where(s >= svs[a], float(a), rank)
                    r2_ref[lt, h] = pltpu.bitcast(rank.astype(BF16), jnp.uint32)
                    e2_ref[lt, h] = pltpu.bitcast(jnp.exp(s - svs[0]).astype(BF16), jnp.uint32)
        sv1 = [sv_scr[0, a] for a in range(PEER_TOPK)]
        sv2 = [sv_scr[1, a] for a in range(PEER_TOPK)]
        vals = [sv1[a] + sv2[b] for a, b in _CANDS]
        n = len(_CANDS)
        before = [jnp.zeros((PEER_HEADS, LANES), F32) for _ in range(n)]
        for i in range(n):
            ai, bi = _CANDS[i]
            for j in range(i + 1, n):
                aj, bj = _CANDS[j]
                if ai <= aj and bi <= bj:
                    before[j] = before[j] + 1.0
                else:
                    t = jnp.where(vals[i] >= vals[j], 1.0, 0.0)
                    before[j] = before[j] + t
                    before[i] = before[i] + (1.0 - t)
        ex1 = [jnp.exp(sv1[a] - sv1[0]) for a in range(PEER_TOPK)]
        ex2 = [jnp.exp(sv2[b] - sv2[0]) for b in range(PEER_TOPK)]
        cnt = [jnp.zeros((PEER_HEADS, LANES), F32) for _ in range(PEER_TOPK)]
        zsum = jnp.zeros((PEER_HEADS, LANES), F32)
        for i, (a, b) in enumerate(_CANDS):
            sel = jnp.where(before[i] < float(PEER_TOPK), 1.0, 0.0)
            cnt[a] = cnt[a] + sel
            zsum = zsum + sel * ex1[a] * ex2[b]
        for a in range(PEER_TOPK):
            d_scr[a] = cnt[a]
        zi_scr[...] = 1.0 / zsum
        for h in range(PEER_HEADS):
            s = s_scr[lt, 2 * h * PEER_NKEYS:(2 * h + 1) * PEER_NKEYS, :]
            cc = jnp.zeros(s.shape, F32)
            for a in reversed(range(PEER_TOPK)):
                cc = jnp.where(s >= sv_scr[0, a, h:h + 1, :], d_scr[a, h:h + 1, :], cc)
            c_ref[lt, h] = cc
            e1_ref[lt, h] = jnp.exp(s - sv_scr[0, 0, h:h + 1, :]) * zi_scr[h:h + 1, :]
        return carry

    lax.fori_loop(0, nlt, lane_tile, 0)


def _peer1(h, g, wq_bf16, kbd_bf16, tt):
    t = h.shape[0]
    nlt = tt // LANES
    gsh = jax.ShapeDtypeStruct((t // LANES, PEER_HEADS, PEER_NKEYS, LANES), F32)
    gblk = pl.BlockSpec((nlt, PEER_HEADS, PEER_NKEYS, LANES), lambda i: (i, 0, 0, 0))
    psh = jax.ShapeDtypeStruct((t // LANES, PEER_HEADS, PEER_NKEYS // 2, LANES), jnp.uint32)
    pblk = pl.BlockSpec((nlt, PEER_HEADS, PEER_NKEYS // 2, LANES), lambda i: (i, 0, 0, 0))
    return pl.pallas_call(
        functools.partial(_peer1_kernel, nlt=nlt),
        out_shape=(jax.ShapeDtypeStruct((D_MODEL // 2, t), jnp.uint32), gsh, gsh, psh, psh),
        grid=(t // tt,),
        in_specs=[pl.BlockSpec((tt, D_MODEL), lambda i: (i, 0)),
                  pl.BlockSpec((1, D_MODEL), lambda i: (0, 0)),
                  pl.BlockSpec((D_MODEL // 2, D_MODEL), lambda i: (0, 0)),
                  pl.BlockSpec((2 * PEER_HEADS, PEER_NKEYS, PEER_HALF), lambda i: (0, 0, 0))],
        out_specs=(pl.BlockSpec((D_MODEL // 2, tt), lambda i: (0, i)), gblk, gblk, pblk, pblk),
        scratch_shapes=[pltpu.VMEM((nlt, NKH, LANES), F32),
                        pltpu.VMEM((2, PEER_TOPK, PEER_HEADS, LANES), F32),
                        pltpu.VMEM((PEER_TOPK, PEER_HEADS, LANES), F32),
                        pltpu.VMEM((PEER_HEADS, LANES), F32)],
        compiler_params=_cparams(("parallel",)),
        name="peer1",
    )(h, g, wq_bf16, kbd_bf16)


NE1 = 8
ETILE = NE1 * PEER_NKEYS


def _peer2_step(u_ref, xn_ref, vt_ref, e1_ref, c_ref, e2_ref, r2_ref, acc_scr, act_w, act_r, w_w, w_r, nlt,
                do_value=True, do_gate=True, do_act=True):
    tile = (PEER_NKEYS, LANES)
    zero = jnp.zeros(tile, BF16)
    grp = 2
    nmb = 4
    mblk = D_MODEL // nmb

    def value_rows(m):
        rs = slice(m * mblk, (m + 1) * mblk)
        vrows = _rows_bf16(vt_ref[m * mblk // 2:(m + 1) * mblk // 2, :])
        acc_scr[rs, :] += jnp.dot(vrows, _rows_bf16(w_r[...]), preferred_element_type=F32)

    def act_rows(m):
        urows = _rows_bf16(u_ref[m * mblk // 2:(m + 1) * mblk // 2, :])
        a = jnp.dot(urows, _rows_bf16(xn_ref[...]), preferred_element_type=F32).astype(BF16)
        a = 0.5 * a * (1.0 + lax.erf(a * 0.7071067811865476))
        act_w[m * mblk // 2:(m + 1) * mblk // 2, :] = pltpu.bitcast(a, jnp.uint32)

    def gate_group(lt, g0):
        ls = slice(lt * LANES, (lt + 1) * LANES)
        gates = [zero] * grp
        for h in range(PEER_HEADS):
            rank = pltpu.bitcast(r2_ref[lt, h], BF16)
            wkey = pltpu.bitcast(e2_ref[lt, h], BF16)
            for j in range(grp):
                i1 = g0 + j
                cnt = jnp.broadcast_to(c_ref[lt, h, i1:i1 + 1, :], tile).astype(BF16)
                wgt = jnp.broadcast_to(e1_ref[lt, h, i1:i1 + 1, :], tile).astype(BF16)
                gates[j] = gates[j] + jnp.where(rank < cnt, wkey, zero) * wgt
        for j in range(grp):
            ps = slice((g0 + j) * PEER_NKEYS // 2, (g0 + j + 1) * PEER_NKEYS // 2)
            w_w[ps, ls] = pltpu.bitcast(gates[j] * pltpu.bitcast(act_r[ps, ls], BF16), jnp.uint32)

    groups = [(lt, g0) for lt in range(nlt) for g0 in range(0, NE1, grp)] if do_gate else []
    mxu_work = [f for m in range(nmb) for f, on in ((functools.partial(value_rows, m), do_value),
                                                  (functools.partial(act_rows, m), do_act)) if on]
    if not mxu_work:
        mxu_work = [lambda: None]
    per = max(1, len(groups) // len(mxu_work))
    gi = 0
    for k, mm in enumerate(mxu_work):
        mm()
        take = len(groups) - gi if k == len(mxu_work) - 1 else per
        for _ in range(take):
            if gi < len(groups):
                gate_group(*groups[gi])
                gi += 1


def _peer2_kernel(h_ref, xn_ref, u_ref, vt_ref, e1_ref, c_ref, e2_ref, r2_ref, o_ref, acc_scr, act0, act1, w0, w1, *,
                  nlt):
    e = pl.program_id(1)

    last = pl.num_programs(1) - 1
    args = (u_ref, xn_ref, vt_ref, e1_ref, c_ref, e2_ref, r2_ref, acc_scr)
    even = (act0, act1, w1, w0)
    odd = (act1, act0, w0, w1)
    steady = jnp.logical_and(e >= 2, e <= last - 2)

    @pl.when(e == 0)
    def _():
        acc_scr[...] = jnp.zeros(acc_scr.shape, F32)
        _peer2_step(*args, *even, nlt, do_value=False, do_gate=False)

    @pl.when(e == 1)
    def _():
        _peer2_step(*args, *odd, nlt, do_value=False)

    @pl.when(jnp.logical_and(steady, e % 2 == 0))
    def _():
        _peer2_step(*args, *even, nlt)

    @pl.when(jnp.logical_and(steady, e % 2 == 1))
    def _():
        _peer2_step(*args, *odd, nlt)

    @pl.when(e == last - 1)
    def _():
        _peer2_step(*args, *even, nlt, do_act=False)

    @pl.when(e == last)
    def _():
        _peer2_step(*args, *odd, nlt, do_act=False, do_gate=False)
        o_ref[...] = h_ref[...] + acc_scr[...].T


def _peer2(h, xn, u_bf16, vt_bf16, e1, cc, e2, r2, tt):
    t = h.shape[0]
    nlt = tt // LANES
    ne = PEER_NKEYS // NE1
    assert ne % 2 == 0
    last = ne - 1
    pfull = pl.BlockSpec((nlt, PEER_HEADS, PEER_NKEYS // 2, LANES), lambda i, e: (i, 0, 0, 0))
    gtile = pl.BlockSpec((nlt, PEER_HEADS, NE1, LANES), lambda i, e: (i, 0, jnp.clip(e - 1, 0, last), 0))
    slot = pltpu.VMEM((ETILE // 2, tt), jnp.uint32)
    return pl.pallas_call(
        functools.partial(_peer2_kernel, nlt=nlt),
        out_shape=jax.ShapeDtypeStruct((t, D_MODEL), F32),
        grid=(t // tt, ne + 2),
        in_specs=[pl.BlockSpec((tt, D_MODEL), lambda i, e: (i, 0)),
                  pl.BlockSpec((D_MODEL // 2, tt), lambda i, e: (0, i)),
                  pl.BlockSpec((ETILE // 2, D_MODEL), lambda i, e: (jnp.minimum(e, last), 0)),
                  pl.BlockSpec((D_MODEL // 2, ETILE), lambda i, e: (0, jnp.clip(e - 2, 0, last))),
                  gtile, gtile, pfull, pfull],
        out_specs=pl.BlockSpec((tt, D_MODEL), lambda i, e: (i, 0)),
        scratch_shapes=[pltpu.VMEM((D_MODEL, tt), F32), slot, slot, slot, slot],
        compiler_params=_cparams(("parallel", "arbitrary")),
        name="peer2",
    )(h, xn, u_bf16, vt_bf16, e1, cc, e2, r2)


def _prep_layer(i, w_in, lbs, hgrn_norm, gdn_conv_w, gdn_a_log, gdn_dt_bias, gdn_norm, conf_dw_w, conf_dw_b,
                conf_ln_g, conf_ln_b, pool_w, pool_scale, w_out, peer_wq, peer_keys, peer_u, peer_v, ple_gate_w,
                ple_proj_w):
    wi = w_in[i]
    hk = HEADS * DH
    c0 = 4 * hk
    qkv = wi[:, c0:c0 + 3 * hk]
    zbg = wi[:, c0 + 3 * hk:c0 + 4 * hk]
    zbb = wi[:, c0 + 4 * hk:c0 + 4 * hk + HEADS]
    zba = wi[:, c0 + 4 * hk + HEADS:c0 + 4 * hk + 2 * HEADS]
    rest = wi[:, c0 + 4 * hk + 2 * HEADS:]
    w_perm = jnp.concatenate([wi[:, :c0], qkv, zbg, rest, jnp.repeat(zbb, DH, axis=1), jnp.repeat(zba, DH, axis=1)],
                             axis=1)
    kbd = peer_keys[i].reshape(2 * PEER_HEADS, PEER_NKEYS, PEER_HALF).astype(BF16)
    pw = pool_w[i]
    eye4 = jnp.eye(4, dtype=F32)
    pwbd = (pw[:, :, None, :] * eye4[:, None, :, None]).reshape(GW, GW)
    return dict(
        w_in=_pack_rows(w_perm), lb=lbs[i][None, :], hgrn_norm=hgrn_norm[i][None, :],
        gdn_conv_w=gdn_conv_w[i], gdn_a_log=jnp.repeat(gdn_a_log[i], DH)[None, :],
        gdn_dt_bias=jnp.repeat(gdn_dt_bias[i], DH)[None, :], gdn_norm=gdn_norm[i][None, :],
        conf_dw_w=conf_dw_w[i], conf_dw_b=conf_dw_b[i][None, :], conf_ln_g=conf_ln_g[i][None, :],
        conf_ln_b=conf_ln_b[i][None, :], pool_w=pwbd, pool_scale=pool_scale[i][None, :],
        w_out=_pack_rows(w_out, layer=i), peer_wq=_pack_rows(peer_wq, transpose=True, layer=i), kbd=kbd,
        peer_u=_pack_rows(peer_u, layer=i), peer_vt=_pack_rows(peer_v, transpose=True, layer=i),
        ple_gate_w=_pack_rows(ple_gate_w, layer=i), ple_proj_w=_pack_rows(ple_proj_w, layer=i))


def _tail(buf, x, n):
    l = x.shape[1]
    if l >= n:
        return x[:, l - n:]
    return jnp.concatenate([buf[:, l:], x], axis=1)


def _trunk(x, p, pos0, s_a, s_b, c_b, c_c, c_d, layers, norms, norm_final):
    bsz, l, _ = x.shape
    t = bsz * l
    prompt = l >= 64
    c = 64 if prompt else l
    nb = 1 if prompt else 8
    cc = min(256, l)
    tm = min(512, t)
    tt = min(512, t)
    h = x.reshape(t, D_MODEL)
    n_a, n_b, nb_b, nb_c, nb_d = [], [], [], [], []
    for i, lw in enumerate(layers):
        z = _inproj(h, norms['mix'][i][None, :], lw['w_in'], tm)
        z3 = z.reshape(bsz, l, ZW)
        eye4 = jnp.eye(HEADS, dtype=F32)
        s0t = (jnp.swapaxes(s_a[i], 2, 3)[:, :, :, None, :] * eye4[None, :, None, :, None]).reshape(bsz, GW, GW)
        o_a, st = _hgrn(z3, lw['lb'], lw['hgrn_norm'], s0t, min(8, bsz) if prompt else nb, min(c, 32))
        st = st.reshape(bsz, HEADS, DH, HEADS, DH)
        n_a.append(jnp.stack([jnp.swapaxes(st[:, j, :, j, :], 1, 2) for j in range(HEADS)], axis=1))
        buf8 = jnp.pad(c_b[i], ((0, 0), (8 - (B_CONV - 1), 0), (0, 0)))
        o_b, sb = _gdn(z3, lw['gdn_conv_w'], lw['gdn_a_log'], lw['gdn_dt_bias'], lw['gdn_norm'], buf8, s_b[i],
                       min(8, bsz) if prompt else nb, c)
        n_b.append(sb)
        nb_b.append(_tail(c_b[i], z3[:, :, ZB_BQ * GW:(ZB_BV + 1) * GW], B_CONV - 1))
        cbuf = jnp.pad(c_c[i], ((0, 0), (CPAD - (C_CONV - 1), 0), (0, 0)))
        dbuf = jnp.pad(c_d[i], ((0, 0), (DPAD - POOL_BUF, 0), (0, 0)))
        o_c, u_c, o_d = _convpool(z3, lw['conf_dw_w'], lw['conf_dw_b'], lw['conf_ln_g'], lw['conf_ln_b'],
                                  lw['pool_w'], lw['pool_scale'], cbuf, dbuf, nb, cc, pos0)
        nb_c.append(_tail(c_c[i], u_c, C_CONV - 1))
        nb_d.append(_tail(c_d[i], z3[:, :, ZB_D * GW:(ZB_D + 1) * GW], POOL_BUF))
        h = _outproj(h, o_a.reshape(t, GW), o_b.reshape(t, GW), o_c.reshape(t, GW), o_d.reshape(t, GW),
                     lw['w_out'], tm)
        xn, e1, cnt, e2, r2 = _peer1(h, norms['ffn'][i][None, :], lw['peer_wq'], lw['kbd'], tt)
        h = _peer2(h, xn, lw['peer_u'], lw['peer_vt'], e1, cnt, e2, r2, min(1024, t))
        h = _ple(h, p.reshape(p.shape[0], t, D_PLE), i, norms['ple'][i][None, :], lw['ple_gate_w'], lw['ple_proj_w'],
                 norm_final[None, :], tm, final=(i == len(layers) - 1))
    st = lambda lst: jnp.stack(lst, axis=0)
    return h.reshape(bsz, l, D_MODEL), st(n_a), st(n_b), st(nb_b), st(nb_c), st(nb_d)


def kernel(x_prompt, x_sample, state_hgrn, state_gdn, state_gdn_conv, state_conf_conv, state_pool, p_prompt, p_sample, norm_mix, w_in, hgrn_lb_logits, hgrn_norm, gdn_conv_w, gdn_a_log, gdn_dt_bias, gdn_norm, conf_dw_w, conf_dw_b, conf_ln_g, conf_ln_b, pool_w, pool_scale, w_out, norm_ffn, peer_wq, peer_keys, peer_u, peer_v, norm_ple, ple_gate_w, ple_proj_w, norm_final):
    sm = jax.nn.softmax(hgrn_lb_logits.astype(F32), axis=0)
    lbs = jnp.cumsum(sm, axis=0) - sm[0:1]
    layers = [_prep_layer(i, w_in, lbs, hgrn_norm, gdn_conv_w, gdn_a_log, gdn_dt_bias, gdn_norm, conf_dw_w,
                          conf_dw_b, conf_ln_g, conf_ln_b, pool_w, pool_scale, w_out, peer_wq, peer_keys, peer_u,
                          peer_v, ple_gate_w, ple_proj_w) for i in range(DEPTH)]
    norms = dict(mix=norm_mix, ffn=norm_ffn, ple=norm_ple)
    bp = x_prompt.shape[0]
    z = lambda *s: jnp.zeros((DEPTH, bp) + s, F32)
    yp, hp, gp, gcp, ccp, pp = _trunk(x_prompt, p_prompt, 0, z(HEADS, DH, DH), z(HEADS, DH, DH),
                                      z(B_CONV - 1, 3 * GW), z(C_CONV - 1, GW), z(POOL_BUF, GW),
                                      layers, norms, norm_final)
    ys, hs, gs, gcs, ccs, ps = _trunk(x_sample, p_sample, PAST_LEN, state_hgrn, state_gdn, state_gdn_conv,
                                      state_conf_conv, state_pool, layers, norms, norm_final)
    return (yp, ys, hp, gp, gcp, ccp, pp, hs, gs, gcs, ccs, ps)
```

```python
import functools

import numpy as np
import jax
import jax.numpy as jnp
from jax import lax
from jax.experimental import pallas as pl
from jax.experimental.pallas import tpu as pltpu

F32 = jnp.float32
BF16 = jnp.bfloat16

D_MODEL = 1024
DEPTH = 4
PAST_LEN = 16384
EPS = 1e-6
NEG_BIG = -1e30
D_PLE = 256
HEADS = 4
DH = 64
GW = 256
B_CONV = 4
C_CONV = 31
POOL_WINDOWS = (2, 4, 8, 16)
POOL_BUF = 15
PEER_HEADS = 8
PEER_NKEYS = 128
PEER_HALF = 64
PEER_TOPK = 16

ZB_AQ, ZB_AF, ZB_AI, ZB_AG, ZB_BQ, ZB_BK, ZB_BV, ZB_BG, ZB_CA, ZB_CB, ZB_D, ZB_BB, ZB_BA = range(13)
ZW = 13 * GW

LANES = 128
SOLVE_BLOCK = 16
VMEM_LIMIT = 56 * 1024 * 1024


def _cparams(sem, flags=None):
    return pltpu.CompilerParams(dimension_semantics=sem, vmem_limit_bytes=VMEM_LIMIT, flags=flags)


def _bdot(a, b):
    return jnp.dot(a.astype(BF16), b.astype(BF16), preferred_element_type=F32)


def _bdot_nt(a, b):
    return lax.dot_general(a.astype(BF16), b.astype(BF16), (((1,), (1,)), ((), ())),
                           preferred_element_type=F32)


def _split3(a):
    a1 = a.astype(BF16)
    r1 = a - a1.astype(F32)
    a2 = r1.astype(BF16)
    a3 = (r1 - a2.astype(F32)).astype(BF16)
    return a1, a2, a3


def _xdot(a, m01):
    a1, a2, a3 = _split3(a)
    m = m01.astype(BF16)
    return (jnp.dot(a1, m, preferred_element_type=F32) + jnp.dot(a2, m, preferred_element_type=F32)
            + jnp.dot(a3, m, preferred_element_type=F32))


def _xdot_left(m01, a):
    a1, a2, a3 = _split3(a)
    m = m01.astype(BF16)
    return (jnp.dot(m, a1, preferred_element_type=F32) + jnp.dot(m, a2, preferred_element_type=F32)
            + jnp.dot(m, a3, preferred_element_type=F32))


def _dot2(a, b):
    ah = a.astype(BF16)
    al = (a - ah.astype(F32)).astype(BF16)
    bh = b.astype(BF16)
    bl = (b - bh.astype(F32)).astype(BF16)
    return (jnp.dot(ah, bh, preferred_element_type=F32) + jnp.dot(ah, bl, preferred_element_type=F32)
            + jnp.dot(al, bh, preferred_element_type=F32))


def _xtranspose(a, eye):
    a1, a2, a3 = _split3(a)
    e = eye.astype(BF16)
    dn = (((1,), (1,)), ((), ()))
    return (lax.dot_general(e, a1, dn, preferred_element_type=F32)
            + lax.dot_general(e, a2, dn, preferred_element_type=F32)
            + lax.dot_general(e, a3, dn, preferred_element_type=F32))


def _pack_kernel(x_ref, o_ref, *, transpose):
    x = x_ref[...]
    if transpose:
        x = x.T
    o_ref[...] = pltpu.bitcast(x.astype(BF16), jnp.uint32)


def _pack_rows(w, transpose=False, layer=None):
    r, c = w.shape[-2:]
    rb = min(r, 1024)
    if layer is None:
        in_spec = pl.BlockSpec((rb, c), lambda i: (i, 0))
    else:
        in_spec = pl.BlockSpec((None, rb, c), lambda i: (layer, i, 0))
    if transpose:
        out_shape, out_spec = (c // 2, r), pl.BlockSpec((c // 2, rb), lambda i: (0, i))
    else:
        out_shape, out_spec = (r // 2, c), pl.BlockSpec((rb // 2, c), lambda i: (i, 0))
    return pl.pallas_call(
        functools.partial(_pack_kernel, transpose=transpose),
        out_shape=jax.ShapeDtypeStruct(out_shape, jnp.uint32),
        grid=(r // rb,),
        in_specs=[in_spec],
        out_specs=out_spec,
        compiler_params=_cparams(("parallel",)),
        name="pack_t" if transpose else "pack",
    )(w)


def _rows_bf16(words):
    return pltpu.bitcast(words, BF16)


def _iota(shape, axis):
    return lax.broadcasted_iota(jnp.int32, shape, axis)


def _tri_incl(c):
    return jnp.where(_iota((c, c), 1) <= _iota((c, c), 0), 1.0, 0.0).astype(F32)


def _eye(n):
    return jnp.where(_iota((n, n), 0) == _iota((n, n), 1), 1.0, 0.0).astype(F32)


def _block_ones(n, blk):
    return jnp.where(_iota((n, n), 0) // blk == _iota((n, n), 1) // blk, 1.0, 0.0).astype(F32)


def _rms(x, g):
    return x * lax.rsqrt(jnp.mean(x * x, axis=-1, keepdims=True) + EPS) * g


def _sigmoid(x):
    return jax.nn.sigmoid(x)


def _silu(x):
    return x * jax.nn.sigmoid(x)


def _inproj_kernel(x_ref, g_ref, w_ref, o_ref):
    xn = _rms(x_ref[...], g_ref[...])
    o_ref[...] = jnp.dot(xn.astype(BF16), _rows_bf16(w_ref[...]), preferred_element_type=F32)


def _inproj(h, g, w_bf16, tm):
    t = h.shape[0]
    return pl.pallas_call(
        _inproj_kernel,
        out_shape=jax.ShapeDtypeStruct((t, ZW), F32),
        grid=(t // tm,),
        in_specs=[pl.BlockSpec((tm, D_MODEL), lambda i: (i, 0)),
                  pl.BlockSpec((1, D_MODEL), lambda i: (0, 0)),
                  pl.BlockSpec((D_MODEL // 2, ZW), lambda i: (0, 0))],
        out_specs=pl.BlockSpec((tm, ZW), lambda i: (i, 0)),
        compiler_params=_cparams(("parallel",)),
        name="inproj",
    )(h, g, w_bf16)


def _outproj_kernel(h_ref, a_ref, b_ref, c_ref, d_ref, w_ref, o_ref):
    acc = h_ref[...]
    for i, r in enumerate((a_ref, b_ref, c_ref, d_ref)):
        acc = acc + jnp.dot(r[...].astype(BF16), _rows_bf16(w_ref[i * GW // 2:(i + 1) * GW // 2, :]),
                            preferred_element_type=F32)
    o_ref[...] = acc


def _outproj(h, oa, ob, oc, od, w_bf16, tm):
    t = h.shape[0]
    tok = lambda w: pl.BlockSpec((tm, w), lambda i: (i, 0))
    return pl.pallas_call(
        _outproj_kernel,
        out_shape=jax.ShapeDtypeStruct((t, D_MODEL), F32),
        grid=(t // tm,),
        in_specs=[tok(D_MODEL), tok(GW), tok(GW), tok(GW), tok(GW),
                  pl.BlockSpec((D_MODEL // 2, D_MODEL), lambda i: (0, 0))],
        out_specs=tok(D_MODEL),
        compiler_params=_cparams(("parallel",)),
        name="outproj",
    )(h, oa, ob, oc, od, w_bf16)


def _ple_kernel(h_ref, p_ref, g_ref, wg_ref, wp_ref, gf_ref, o_ref, *, final):
    h = h_ref[...]
    xn = _rms(h, g_ref[...])
    gate = _sigmoid(jnp.dot(xn.astype(BF16), _rows_bf16(wg_ref[...]), preferred_element_type=F32))
    proj = jnp.dot(p_ref[...].astype(BF16), _rows_bf16(wp_ref[...]), preferred_element_type=F32)
    out = h + gate * proj
    if final:
        out = _rms(out, gf_ref[...])
    o_ref[...] = out


def _ple(h, p, layer, g, wg_bf16, wp_bf16, gfinal, tm, final):
    t = h.shape[0]
    tok = lambda w: pl.BlockSpec((tm, w), lambda i: (i, 0))
    ptok = pl.BlockSpec((None, tm, D_PLE), lambda i: (layer, i, 0))
    full = lambda r, c: pl.BlockSpec((r, c), lambda i: (0, 0))
    return pl.pallas_call(
        functools.partial(_ple_kernel, final=final),
        out_shape=jax.ShapeDtypeStruct((t, D_MODEL), F32),
        grid=(t // tm,),
        in_specs=[tok(D_MODEL), ptok, full(1, D_MODEL), full(D_MODEL // 2, D_MODEL), full(D_PLE // 2, D_MODEL),
                  full(1, D_MODEL)],
        out_specs=tok(D_MODEL),
        compiler_params=_cparams(("parallel",)),
        name="ple_final" if final else "ple",
    )(h, p, g, wg_bf16, wp_bf16, gfinal)


def _hgrn_kernel(z_ref, lb_ref, nw_ref, s0_ref, o_ref, sf_ref, st_scr, la_scr, q_scr, k_scr, v_scr, p_scr, *, nb, c):
    ci = pl.program_id(1)

    @pl.when(ci == 0)
    def _():
        st_scr[...] = s0_ref[...]

    lb = lb_ref[...]
    bones = _block_ones(GW, DH)
    tri = _tri_incl(c)
    rows = _iota((c, GW), 0)
    seqs = range(nb)
    for b in seqs:
        zf = z_ref[b, :, 1 * GW:2 * GW]
        q_scr[b] = _silu(z_ref[b, :, 0 * GW:1 * GW])
        k_scr[b] = (1.0 - lb) * _sigmoid(-zf)
        v_scr[b] = z_ref[b, :, 2 * GW:3 * GW]
        f = lb + (1.0 - lb) * _sigmoid(zf)
        la_scr[b] = _xdot_left(tri, jnp.log(f))

    def build(s, carry):
        for b in seqs:
            la_s = la_scr[b, pl.ds(s, 1), :]
            k_s = k_scr[b, pl.ds(s, 1), :]
            dec = jnp.exp(jnp.minimum(la_scr[b] - la_s, 0.0))
            p_scr[b, pl.ds(pl.multiple_of(s * c, c), c), :] = jnp.where(rows >= s, q_scr[b] * k_s * dec, 0.0)
        return carry

    lax.fori_loop(0, c, build, 0, unroll=min(4, c))
    for b in seqs:
        p_scr[b] = jnp.dot(p_scr[b].astype(BF16), bones.astype(BF16), preferred_element_type=F32)

    def consume(s, accs):
        return tuple(acc + p_scr[b, pl.ds(pl.multiple_of(s * c, c), c), :] * v_scr[b, pl.ds(s, 1), :]
                     for b, acc in zip(seqs, accs))

    os_ = lax.fori_loop(0, c, consume, tuple(jnp.zeros((c, GW), F32) for _ in seqs), unroll=min(4, c))
    sts = [st_scr[b] for b in seqs]
    las = [la_scr[b] for b in seqs]
    os_ = [o + _bdot_nt(q_scr[b] * jnp.exp(la), st) for b, o, la, st in zip(seqs, os_, las, sts)]
    upds = []
    for b, la in zip(seqs, las):
        k_dec = k_scr[b] * jnp.exp(la[c - 1:c, :] - la)
        upds.append(lax.dot_general(v_scr[b].astype(BF16), k_dec.astype(BF16), (((0,), (0,)), ((), ())),
                                    preferred_element_type=F32))
    for b, la, st, upd in zip(seqs, las, sts, upds):
        st_scr[b] = st * jnp.exp(la[c - 1:c, :]) + upd * bones
    mss = [_xdot(o * o, bones) * (1.0 / DH) for o in os_]
    for b, o, ms in zip(seqs, os_, mss):
        o_ref[b] = o * lax.rsqrt(ms + EPS) * nw_ref[...] * _sigmoid(z_ref[b, :, 3 * GW:4 * GW])

    @pl.when(ci == pl.num_programs(1) - 1)
    def _():
        sf_ref[...] = st_scr[...]


def _hgrn(z3, lb, nw, s0t, nb, c):
    bsz, l, _ = z3.shape
    return pl.pallas_call(
        functools.partial(_hgrn_kernel, nb=nb, c=c),
        out_shape=(jax.ShapeDtypeStruct((bsz, l, GW), F32), jax.ShapeDtypeStruct((bsz, GW, GW), F32)),
        grid=(bsz // nb, l // c),
        in_specs=[pl.BlockSpec((nb, c, 4 * GW), lambda b, i: (b, i, 0)),
                  pl.BlockSpec((1, GW), lambda b, i: (0, 0)),
                  pl.BlockSpec((1, GW), lambda b, i: (0, 0)),
                  pl.BlockSpec((nb, GW, GW), lambda b, i: (b, 0, 0))],
        out_specs=(pl.BlockSpec((nb, c, GW), lambda b, i: (b, i, 0)),
                   pl.BlockSpec((nb, GW, GW), lambda b, i: (b, 0, 0))),
        scratch_shapes=[pltpu.VMEM((nb, GW, GW), F32)] + [pltpu.VMEM((nb, c, GW), F32)] * 4
                       + [pltpu.VMEM((nb, c * c, GW), F32)],
        compiler_params=_cparams(("parallel", "arbitrary")),
        name="hgrn",
    )(z3, lb, nw, s0t)


def _gdn_kernel(zq_ref, zk_ref, zv_ref, zg_ref, zb_ref, za_ref, cw_ref, alog_ref, dtb_ref, nw_ref, buf_ref, s0_ref,
                o_ref, sf_ref, s_scr, xp_scr, *, nb, c):
    ci = pl.program_id(1)

    @pl.when(ci == 0)
    def _():
        s_scr[...] = s0_ref[...]
        xp_scr[:, 0:8, :] = buf_ref[...]

    bones = _block_ones(GW, DH)
    tri = _tri_incl(c)
    eye_c = _eye(c)
    eye_h = _eye(DH)
    tt = _iota((c, c), 0)
    ss = _iota((c, c), 1)
    causal = tt >= ss
    strict = tt > ss
    items = []
    for b in range(nb):
        for j, r in enumerate((zq_ref, zk_ref, zv_ref)):
            xp_scr[b, 8:8 + c, j * GW:(j + 1) * GW] = r[b]
        conv = cw_ref[3:4, :] * xp_scr[b, 8:8 + c, :]
        for j in range(B_CONV - 1):
            conv = conv + cw_ref[j:j + 1, :] * xp_scr[b, 5 + j:5 + j + c, :]
        xp_scr[b, 0:8, :] = xp_scr[b, c:c + 8, :]
        qkv = _silu(conv)
        q = qkv[:, 0:GW]
        k = qkv[:, GW:2 * GW]
        v = qkv[:, 2 * GW:3 * GW]
        q = q * lax.rsqrt(_xdot(q * q, bones) + EPS) * (DH ** -0.5)
        k = k * lax.rsqrt(_xdot(k * k, bones) + EPS)
        beta = _sigmoid(zb_ref[b])
        loga = -jnp.exp(alog_ref[...]) * jax.nn.softplus(za_ref[b] + dtb_ref[...])
        g = _xdot_left(tri, loga)
        for h in range(HEADS):
            hs = slice(h * DH, (h + 1) * DH)
            items.append(dict(b=b, h=h, hs=hs, q=q[:, hs], k=k[:, hs], v=v[:, hs], beta=beta[:, hs], g=g[:, hs]))
    for it in items:
        it['gcol'] = it['g'][:, 0:c]
    for it in items:
        it['grow'] = _xtranspose(it['gcol'], eye_c)
    for it in items:
        it['kk'] = _bdot_nt(it['k'], it['k'])
        it['qk'] = _bdot_nt(it['q'], it['k'])
    for it in items:
        it['dec'] = jnp.exp(jnp.where(causal, it['gcol'] - it['grow'], NEG_BIG))
        it['a'] = jnp.where(strict, it['beta'][:, 0:c] * it['kk'] * it['dec'], 0.0)
        it['r'] = jnp.concatenate([it['beta'] * it['v'], it['beta'] * jnp.exp(it['g']) * it['k']], axis=1)
    blk = min(SOLVE_BLOCK, c)
    for it in items:
        it['solved'] = []
    for j0 in range(0, c, blk):
        rbs = [it['r'][j0:j0 + blk, :] for it in items]
        abs_ = [it['a'][j0:j0 + blk, j0:j0 + blk] for it in items]
        for s in range(blk - 1):
            rbs = [rb - ab[:, s:s + 1] * rb[s:s + 1, :] for rb, ab in zip(rbs, abs_)]
        for it, rb in zip(items, rbs):
            it['solved'].append(rb)
        if j0 + blk < c:
            belows = [_dot2(it['a'][j0 + blk:, j0:j0 + blk], rb) for it, rb in zip(items, rbs)]
            for it, below in zip(items, belows):
                it['r'] = jnp.concatenate([it['r'][:j0 + blk, :], it['r'][j0 + blk:, :] - below], axis=0)
    for it in items:
        w = it['solved'][0] if len(it['solved']) == 1 else jnp.concatenate(it['solved'], axis=0)
        it['w1'] = w[:, 0:DH]
        it['w2'] = w[:, DH:2 * DH]
        it['s'] = s_scr[it['b'], it['h']]
    for it in items:
        it['u'] = it['w1'] - _bdot(it['w2'], it['s'])
    for it in items:
        it['o'] = _bdot(it['q'] * jnp.exp(it['g']), it['s']) + _bdot(it['qk'] * it['dec'], it['u'])
    for it in items:
        g_last = it['g'][c - 1:c, :]
        kd = it['k'] * jnp.exp(g_last - it['g'])
        kd_t = lax.dot_general(eye_h.astype(BF16), kd.astype(BF16), (((1,), (1,)), ((), ())),
                               preferred_element_type=F32)
        it['snew'] = jnp.exp(g_last) * it['s'] + _bdot(kd_t, it['u'])
    for it in items:
        s_scr[it['b'], it['h']] = it['snew']
        o = it['o']
        zg = zg_ref[it['b'], :, it['hs']]
        o = o * lax.rsqrt(jnp.mean(o * o, axis=-1, keepdims=True) + EPS) * nw_ref[...] * _silu(zg)
        o_ref[it['b'], :, it['hs']] = o

    @pl.when(ci == pl.num_programs(1) - 1)
    def _():
        sf_ref[...] = s_scr[...]


def _gdn(z3, cw, alog_x, dtb_x, nw, buf8, s0, nb, c):
    bsz, l, _ = z3.shape
    zblk = lambda j: pl.BlockSpec((nb, c, GW), lambda b, i, j=j: (b, i, j))
    full = lambda r, w: pl.BlockSpec((r, w), lambda b, i: (0, 0))
    return pl.pallas_call(
        functools.partial(_gdn_kernel, nb=nb, c=c),
        out_shape=(jax.ShapeDtypeStruct((bsz, l, GW), F32), jax.ShapeDtypeStruct((bsz, HEADS, DH, DH), F32)),
        grid=(bsz // nb, l // c),
        in_specs=[zblk(ZB_BQ), zblk(ZB_BK), zblk(ZB_BV), zblk(ZB_BG), zblk(ZB_BB), zblk(ZB_BA),
                  full(B_CONV, 3 * GW), full(1, GW), full(1, GW), full(1, DH),
                  pl.BlockSpec((nb, 8, 3 * GW), lambda b, i: (b, 0, 0)),
                  pl.BlockSpec((nb, HEADS, DH, DH), lambda b, i: (b, 0, 0, 0))],
        out_specs=(pl.BlockSpec((nb, c, GW), lambda b, i: (b, i, 0)),
                   pl.BlockSpec((nb, HEADS, DH, DH), lambda b, i: (b, 0, 0, 0))),
        scratch_shapes=[pltpu.VMEM((nb, HEADS, DH, DH), F32), pltpu.VMEM((nb, 8 + c, 3 * GW), F32)],
        compiler_params=_cparams(("parallel", "arbitrary")),
        name="gdn",
    )(z3, z3, z3, z3, z3, z3, cw, alog_x, dtb_x, nw, buf8, s0)


CPAD = 32
DPAD = 16


def _convpool_kernel(za_ref, zb_ref, zd_ref, dw_ref, db_ref, lg_ref, lb_ref, pw_ref, ps_ref, cbuf_ref, dbuf_ref,
                     oc_ref, u_ref, od_ref, xc_scr, xd_scr, *, nb, c, pos0):
    ci = pl.program_id(1)

    @pl.when(ci == 0)
    def _():
        xc_scr[:, 0:CPAD, :] = cbuf_ref[...]
        xd_scr[:, 0:DPAD, :] = dbuf_ref[...]

    lane = _iota((c, GW), 1)
    wl = jnp.where(lane < 64, 2.0, jnp.where(lane < 128, 4.0, jnp.where(lane < 192, 8.0, 16.0)))
    pos = (_iota((c, GW), 0) + (ci * c + pos0 + 1)).astype(F32)
    cnt = jnp.minimum(wl, pos)
    for b in range(nb):
        u = za_ref[b] * _sigmoid(zb_ref[b])
        u_ref[b] = u
        xc_scr[b, CPAD:CPAD + c, :] = u
        y = dw_ref[C_CONV - 1:C_CONV, :] * u
        for j in range(C_CONV - 1):
            y = y + dw_ref[j:j + 1, :] * xc_scr[b, 2 + j:2 + j + c, :]
        xc_scr[b, 0:CPAD, :] = xc_scr[b, c:c + CPAD, :]
        y = y + db_ref[...]
        mu = jnp.mean(y, axis=-1, keepdims=True)
        yc = y - mu
        var = jnp.mean(yc * yc, axis=-1, keepdims=True)
        oc_ref[b] = _silu(yc * lax.rsqrt(var + EPS) * lg_ref[...] + lb_ref[...])
        x = zd_ref[b]
        xd_scr[b, DPAD:DPAD + c, :] = x
        acc = x
        sums = {}
        for i in range(1, 16):
            acc = acc + xd_scr[b, DPAD - i:DPAD - i + c, :]
            if i + 1 in POOL_WINDOWS:
                sums[i + 1] = acc
        xd_scr[b, 0:DPAD, :] = xd_scr[b, c:c + DPAD, :]
        ssel = jnp.where(lane < 64, sums[2], jnp.where(lane < 128, sums[4], jnp.where(lane < 192, sums[8], sums[16])))
        diff = ssel / cnt - x
        od_ref[b] = _bdot(diff, pw_ref[...]) * ps_ref[...]


def _convpool(z3, dw, db, lg, lb, pwbd, ps, cbuf, dbuf, nb, c, pos0):
    bsz, l, _ = z3.shape
    zblk = lambda j: pl.BlockSpec((nb, c, GW), lambda b, i, j=j: (b, i, j))
    full = lambda r, w: pl.BlockSpec((r, w), lambda b, i: (0, 0))
    oblk = pl.BlockSpec((nb, c, GW), lambda b, i: (b, i, 0))
    osh = jax.ShapeDtypeStruct((bsz, l, GW), F32)
    return pl.pallas_call(
        functools.partial(_convpool_kernel, nb=nb, c=c, pos0=pos0),
        out_shape=(osh, osh, osh),
        grid=(bsz // nb, l // c),
        in_specs=[zblk(ZB_CA), zblk(ZB_CB), zblk(ZB_D), full(C_CONV, GW), full(1, GW), full(1, GW), full(1, GW),
                  full(GW, GW), full(1, GW),
                  pl.BlockSpec((nb, CPAD, GW), lambda b, i: (b, 0, 0)),
                  pl.BlockSpec((nb, DPAD, GW), lambda b, i: (b, 0, 0))],
        out_specs=(oblk, oblk, oblk),
        scratch_shapes=[pltpu.VMEM((nb, CPAD + c, GW), F32), pltpu.VMEM((nb, DPAD + c, GW), F32)],
        compiler_params=_cparams(("parallel", "arbitrary")),
        name="convpool",
    )(z3, z3, z3, dw, db, lg, lb, pwbd, ps, cbuf, dbuf)


NKH = 2 * PEER_HEADS * PEER_NKEYS
_CANDS = [(a, b) for a in range(PEER_TOPK) for b in range(PEER_TOPK) if (a + 1) * (b + 1) <= PEER_TOPK]


def _oddeven_merge_sort_pairs(n):
    pairs = []

    def merge(lo, hi, r):
        step = r * 2
        if step < hi - lo:
            merge(lo, hi, step)
            merge(lo + r, hi, step)
            pairs.extend((i, i + r) for i in range(lo + r, hi - r, step))
        else:
            pairs.append((lo, lo + r))

    def sort(lo, hi):
        if hi - lo >= 1:
            mid = lo + (hi - lo) // 2
            sort(lo, mid)
            sort(mid + 1, hi)
            merge(lo, hi, 1)

    sort(0, n - 1)
    return pairs


_SORT16 = _oddeven_merge_sort_pairs(PEER_NKEYS // 8)


def _peer1_kernel(h_ref, g_ref, wq_ref, kbd_ref, xn_ref, e1_ref, c_ref, e2_ref, r2_ref, s_scr, sv_scr, d_scr, zi_scr,
                  *, nlt):
    xn = _rms(h_ref[...], g_ref[...]).T.astype(BF16)
    xn_ref[...] = pltpu.bitcast(xn, jnp.uint32)
    qt = jnp.dot(_rows_bf16(wq_ref[...]), xn, preferred_element_type=F32).astype(BF16)
    for hp in range(2 * PEER_HEADS):
        sc = jnp.dot(kbd_ref[hp], qt[hp * PEER_HALF:(hp + 1) * PEER_HALF, :], preferred_element_type=F32)
        for lt in range(nlt):
            s_scr[lt, hp * PEER_NKEYS:(hp + 1) * PEER_NKEYS, :] = sc[:, lt * LANES:(lt + 1) * LANES]

    def lane_tile(lt, carry):
        for h in range(PEER_HEADS):
            for p in range(2):
                s = s_scr[lt, (2 * h + p) * PEER_NKEYS:(2 * h + p + 1) * PEER_NKEYS, :]
                lists = [s[8 * i:8 * (i + 1), :] for i in range(PEER_NKEYS // 8)]
                for i, j in _SORT16:
                    lists[i], lists[j] = jnp.maximum(lists[i], lists[j]), jnp.minimum(lists[i], lists[j])
                svs = []
                for a in range(PEER_TOPK):
                    m = jnp.max(lists[0], axis=0, keepdims=True)
                    sv_scr[p, a, h:h + 1, :] = m
                    svs.append(m)
                    popped = lists[0] == m
                    for i in range(PEER_TOPK - 1 - a):
                        lists[i] = jnp.where(popped, lists[i + 1], lists[i])
                if p == 1:
                    rank = jnp.full(s.shape, float(PEER_TOPK), F32)
                    for a in reversed(range(PEER_TOPK)):
                        rank = jnp.where(s >= svs[a], float(a), rank)
                    r2_ref[lt, h] = pltpu.bitcast(rank.astype(BF16), jnp.uint32)
                    e2_ref[lt, h] = pltpu.bitcast(jnp.exp(s - svs[0]).astype(BF16), jnp.uint32)
        sv1 = [sv_scr[0, a] for a in range(PEER_TOPK)]
        sv2 = [sv_scr[1, a] for a in range(PEER_TOPK)]
        vals = [sv1[a] + sv2[b] for a, b in _CANDS]
        n = len(_CANDS)
        before = [jnp.zeros((PEER_HEADS, LANES), F32) for _ in range(n)]
        for i in range(n):
            ai, bi = _CANDS[i]
            for j in range(i + 1, n):
                aj, bj = _CANDS[j]
                if ai <= aj and bi <= bj:
                    before[j] = before[j] + 1.0
                else:
                    t = jnp.where(vals[i] >= vals[j], 1.0, 0.0)
                    before[j] = before[j] + t
                    before[i] = before[i] + (1.0 - t)
        ex1 = [jnp.exp(sv1[a] - sv1[0]) for a in range(PEER_TOPK)]
        ex2 = [jnp.exp(sv2[b] - sv2[0]) for b in range(PEER_TOPK)]
        cnt = [jnp.zeros((PEER_HEADS, LANES), F32) for _ in range(PEER_TOPK)]
        zsum = jnp.zeros((PEER_HEADS, LANES), F32)
        for i, (a, b) in enumerate(_CANDS):
            sel = jnp.where(before[i] < float(PEER_TOPK), 1.0, 0.0)
            cnt[a] = cnt[a] + sel
            zsum = zsum + sel * ex1[a] * ex2[b]
        for a in range(PEER_TOPK):
            d_scr[a] = cnt[a]
        zi_scr[...] = 1.0 / zsum
        for h in range(PEER_HEADS):
            s = s_scr[lt, 2 * h * PEER_NKEYS:(2 * h + 1) * PEER_NKEYS, :]
            cc = jnp.zeros(s.shape, F32)
            for a in reversed(range(PEER_TOPK)):
                cc = jnp.where(s >= sv_scr[0, a, h:h + 1, :], d_scr[a, h:h + 1, :], cc)
            c_ref[lt, h] = cc
            e1_ref[lt, h] = jnp.exp(s - sv_scr[0, 0, h:h + 1, :]) * zi_scr[h:h + 1, :]
        return carry

    lax.fori_loop(0, nlt, lane_tile, 0)


def _peer1(h, g, wq_bf16, kbd_bf16, tt):
    t = h.shape[0]
    nlt = tt // LANES
    gsh = jax.ShapeDtypeStruct((t // LANES, PEER_HEADS, PEER_NKEYS, LANES), F32)
    gblk = pl.BlockSpec((nlt, PEER_HEADS, PEER_NKEYS, LANES), lambda i: (i, 0, 0, 0))
    psh = jax.ShapeDtypeStruct((t // LANES, PEER_HEADS, PEER_NKEYS // 2, LANES), jnp.uint32)
    pblk = pl.BlockSpec((nlt, PEER_HEADS, PEER_NKEYS // 2, LANES), lambda i: (i, 0, 0, 0))
    return pl.pallas_call(
        functools.partial(_peer1_kernel, nlt=nlt),
        out_shape=(jax.ShapeDtypeStruct((D_MODEL // 2, t), jnp.uint32), gsh, gsh, psh, psh),
        grid=(t // tt,),
        in_specs=[pl.BlockSpec((tt, D_MODEL), lambda i: (i, 0)),
                  pl.BlockSpec((1, D_MODEL), lambda i: (0, 0)),
                  pl.BlockSpec((D_MODEL // 2, D_MODEL), lambda i: (0, 0)),
                  pl.BlockSpec((2 * PEER_HEADS, PEER_NKEYS, PEER_HALF), lambda i: (0, 0, 0))],
        out_specs=(pl.BlockSpec((D_MODEL // 2, tt), lambda i: (0, i)), gblk, gblk, pblk, pblk),
        scratch_shapes=[pltpu.VMEM((nlt, NKH, LANES), F32),
                        pltpu.VMEM((2, PEER_TOPK, PEER_HEADS, LANES), F32),
                        pltpu.VMEM((PEER_TOPK, PEER_HEADS, LANES), F32),
                        pltpu.VMEM((PEER_HEADS, LANES), F32)],
        compiler_params=_cparams(("parallel",)),
        name="peer1",
    )(h, g, wq_bf16, kbd_bf16)


def _outpeer1_kernel(h_ref, a_ref, b_ref, c_ref, d_ref, w_ref, g_ref, wq_ref, kbd_ref, hn_ref, *rest, nlt):
    _outproj_kernel(h_ref, a_ref, b_ref, c_ref, d_ref, w_ref, hn_ref)
    _peer1_kernel(hn_ref, g_ref, wq_ref, kbd_ref, *rest, nlt=nlt)


def _outpeer1(h, oa, ob, oc, od, w_out, g, wq_bf16, kbd_bf16, tt):
    t = h.shape[0]
    nlt = tt // LANES
    tok = lambda w: pl.BlockSpec((tt, w), lambda i: (i, 0))
    gsh = jax.ShapeDtypeStruct((t // LANES, PEER_HEADS, PEER_NKEYS, LANES), F32)
    gblk = pl.BlockSpec((nlt, PEER_HEADS, PEER_NKEYS, LANES), lambda i: (i, 0, 0, 0))
    psh = jax.ShapeDtypeStruct((t // LANES, PEER_HEADS, PEER_NKEYS // 2, LANES), jnp.uint32)
    pblk = pl.BlockSpec((nlt, PEER_HEADS, PEER_NKEYS // 2, LANES), lambda i: (i, 0, 0, 0))
    return pl.pallas_call(
        functools.partial(_outpeer1_kernel, nlt=nlt),
        out_shape=(jax.ShapeDtypeStruct((t, D_MODEL), F32), jax.ShapeDtypeStruct((D_MODEL // 2, t), jnp.uint32),
                   gsh, gsh, psh, psh),
        grid=(t // tt,),
        in_specs=[tok(D_MODEL), tok(GW), tok(GW), tok(GW), tok(GW),
                  pl.BlockSpec((D_MODEL // 2, D_MODEL), lambda i: (0, 0)),
                  pl.BlockSpec((1, D_MODEL), lambda i: (0, 0)),
                  pl.BlockSpec((D_MODEL // 2, D_MODEL), lambda i: (0, 0)),
                  pl.BlockSpec((2 * PEER_HEADS, PEER_NKEYS, PEER_HALF), lambda i: (0, 0, 0))],
        out_specs=(tok(D_MODEL), pl.BlockSpec((D_MODEL // 2, tt), lambda i: (0, i)), gblk, gblk, pblk, pblk),
        scratch_shapes=[pltpu.VMEM((nlt, NKH, LANES), F32),
                        pltpu.VMEM((2, PEER_TOPK, PEER_HEADS, LANES), F32),
                        pltpu.VMEM((PEER_TOPK, PEER_HEADS, LANES), F32),
                        pltpu.VMEM((PEER_HEADS, LANES), F32)],
        compiler_params=_cparams(("parallel",)),
        name="outpeer1",
    )(h, oa, ob, oc, od, w_out, g, wq_bf16, kbd_bf16)


NE1 = 8
ETILE = NE1 * PEER_NKEYS


def _peer2_step(u_ref, xn_ref, vt_ref, e1_ref, c_ref, e2_ref, r2_ref, acc_scr, act_w, act_r, w_w, w_r, nlt,
                do_value=True, do_gate=True, do_act=True):
    tile = (PEER_NKEYS, LANES)
    zero = jnp.zeros(tile, BF16)
    grp = 2
    nmb = 4
    mblk = D_MODEL // nmb

    def value_rows(m):
        rs = slice(m * mblk, (m + 1) * mblk)
        vrows = _rows_bf16(vt_ref[m * mblk // 2:(m + 1) * mblk // 2, :])
        acc_scr[rs, :] += jnp.dot(vrows, _rows_bf16(w_r[...]), preferred_element_type=F32)

    def act_rows(m):
        urows = _rows_bf16(u_ref[m * mblk // 2:(m + 1) * mblk // 2, :])
        a = jnp.dot(urows, _rows_bf16(xn_ref[...]), preferred_element_type=F32).astype(BF16)
        a = 0.5 * a * (1.0 + lax.erf(a * 0.7071067811865476))
        act_w[m * mblk // 2:(m + 1) * mblk // 2, :] = pltpu.bitcast(a, jnp.uint32)

    def gate_group(lt, g0):
        ls = slice(lt * LANES, (lt + 1) * LANES)
        gates = [zero] * grp
        for h in range(PEER_HEADS):
            rank = pltpu.bitcast(r2_ref[lt, h], BF16)
            wkey = pltpu.bitcast(e2_ref[lt, h], BF16)
            for j in range(grp):
                i1 = g0 + j
                cnt = jnp.broadcast_to(c_ref[lt, h, i1:i1 + 1, :], tile).astype(BF16)
                wgt = jnp.broadcast_to(e1_ref[lt, h, i1:i1 + 1, :], tile).astype(BF16)
                gates[j] = gates[j] + jnp.where(rank < cnt, wkey, zero) * wgt
        for j in range(grp):
            ps = slice((g0 + j) * PEER_NKEYS // 2, (g0 + j + 1) * PEER_NKEYS // 2)
            w_w[ps, ls] = pltpu.bitcast(gates[j] * pltpu.bitcast(act_r[ps, ls], BF16), jnp.uint32)

    groups = [(lt, g0) for lt in range(nlt) for g0 in range(0, NE1, grp)] if do_gate else []
    mxu_work = [f for m in range(nmb) for f, on in ((functools.partial(value_rows, m), do_value),
                                                  (functools.partial(act_rows, m), do_act)) if on]
    if not mxu_work:
        mxu_work = [lambda: None]
    per = max(1, len(groups) // len(mxu_work))
    gi = 0
    for k, mm in enumerate(mxu_work):
        mm()
        take = len(groups) - gi if k == len(mxu_work) - 1 else per
        for _ in range(take):
            if gi < len(groups):
                gate_group(*groups[gi])
                gi += 1


def _peer2_kernel(h_ref, xn_ref, u_ref, vt_ref, e1_ref, c_ref, e2_ref, r2_ref, o_ref, acc_scr, act0, act1, w0, w1, *,
                  nlt):
    e = pl.program_id(1)

    last = pl.num_programs(1) - 1
    args = (u_ref, xn_ref, vt_ref, e1_ref, c_ref, e2_ref, r2_ref, acc_scr)
    even = (act0, act1, w1, w0)
    odd = (act1, act0, w0, w1)
    steady = jnp.logical_and(e >= 2, e <= last - 2)

    @pl.when(e == 0)
    def _():
        acc_scr[...] = jnp.zeros(acc_scr.shape, F32)
        _peer2_step(*args, *even, nlt, do_value=False, do_gate=False)

    @pl.when(e == 1)
    def _():
        _peer2_step(*args, *odd, nlt, do_value=False)

    @pl.when(jnp.logical_and(steady, e % 2 == 0))
    def _():
        _peer2_step(*args, *even, nlt)

    @pl.when(jnp.logical_and(steady, e % 2 == 1))
    def _():
        _peer2_step(*args, *odd, nlt)

    @pl.when(e == last - 1)
    def _():
        _peer2_step(*args, *even, nlt, do_act=False)

    @pl.when(e == last)
    def _():
        _peer2_step(*args, *odd, nlt, do_act=False, do_gate=False)
        o_ref[...] = h_ref[...] + acc_scr[...].T


def _peer2(h, xn, u_bf16, vt_bf16, e1, cc, e2, r2, tt):
    t = h.shape[0]
    nlt = tt // LANES
    ne = PEER_NKEYS // NE1
    assert ne % 2 == 0
    last = ne - 1
    pfull = pl.BlockSpec((nlt, PEER_HEADS, PEER_NKEYS // 2, LANES), lambda i, e: (i, 0, 0, 0))
    gtile = pl.BlockSpec((nlt, PEER_HEADS, NE1, LANES), lambda i, e: (i, 0, jnp.clip(e - 1, 0, last), 0))
    slot = pltpu.VMEM((ETILE // 2, tt), jnp.uint32)
    return pl.pallas_call(
        functools.partial(_peer2_kernel, nlt=nlt),
        out_shape=jax.ShapeDtypeStruct((t, D_MODEL), F32),
        grid=(t // tt, ne + 2),
        in_specs=[pl.BlockSpec((tt, D_MODEL), lambda i, e: (i, 0)),
                  pl.BlockSpec((D_MODEL // 2, tt), lambda i, e: (0, i)),
                  pl.BlockSpec((ETILE // 2, D_MODEL), lambda i, e: (jnp.minimum(e, last), 0)),
                  pl.BlockSpec((D_MODEL // 2, ETILE), lambda i, e: (0, jnp.clip(e - 2, 0, last))),
                  gtile, gtile, pfull, pfull],
        out_specs=pl.BlockSpec((tt, D_MODEL), lambda i, e: (i, 0)),
        scratch_shapes=[pltpu.VMEM((D_MODEL, tt), F32), slot, slot, slot, slot],
        compiler_params=_cparams(("parallel", "arbitrary")),
        name="peer2",
    )(h, xn, u_bf16, vt_bf16, e1, cc, e2, r2)


def _prep_layer(i, w_in, lbs, hgrn_norm, gdn_conv_w, gdn_a_log, gdn_dt_bias, gdn_norm, conf_dw_w, conf_dw_b,
                conf_ln_g, conf_ln_b, pool_w, pool_scale, w_out, peer_wq, peer_keys, peer_u, peer_v, ple_gate_w,
                ple_proj_w):
    wi = w_in[i]
    hk = HEADS * DH
    c0 = 4 * hk
    qkv = wi[:, c0:c0 + 3 * hk]
    zbg = wi[:, c0 + 3 * hk:c0 + 4 * hk]
    zbb = wi[:, c0 + 4 * hk:c0 + 4 * hk + HEADS]
    zba = wi[:, c0 + 4 * hk + HEADS:c0 + 4 * hk + 2 * HEADS]
    rest = wi[:, c0 + 4 * hk + 2 * HEADS:]
    w_perm = jnp.concatenate([wi[:, :c0], qkv, zbg, rest, jnp.repeat(zbb, DH, axis=1), jnp.repeat(zba, DH, axis=1)],
                             axis=1)
    kbd = peer_keys[i].reshape(2 * PEER_HEADS, PEER_NKEYS, PEER_HALF).astype(BF16)
    pw = pool_w[i]
    eye4 = jnp.eye(4, dtype=F32)
    pwbd = (pw[:, :, None, :] * eye4[:, None, :, None]).reshape(GW, GW)
    return dict(
        w_in=_pack_rows(w_perm), lb=lbs[i][None, :], hgrn_norm=hgrn_norm[i][None, :],
        gdn_conv_w=gdn_conv_w[i], gdn_a_log=jnp.repeat(gdn_a_log[i], DH)[None, :],
        gdn_dt_bias=jnp.repeat(gdn_dt_bias[i], DH)[None, :], gdn_norm=gdn_norm[i][None, :],
        conf_dw_w=conf_dw_w[i], conf_dw_b=conf_dw_b[i][None, :], conf_ln_g=conf_ln_g[i][None, :],
        conf_ln_b=conf_ln_b[i][None, :], pool_w=pwbd, pool_scale=pool_scale[i][None, :],
        w_out=_pack_rows(w_out, layer=i), peer_wq=_pack_rows(peer_wq, transpose=True, layer=i), kbd=kbd,
        peer_u=_pack_rows(peer_u, layer=i), peer_vt=_pack_rows(peer_v, transpose=True, layer=i),
        ple_gate_w=_pack_rows(ple_gate_w, layer=i), ple_proj_w=_pack_rows(ple_proj_w, layer=i))


def _tail(buf, x, n):
    l = x.shape[1]
    if l >= n:
        return x[:, l - n:]
    return jnp.concatenate([buf[:, l:], x], axis=1)


def _trunk(x, p, pos0, s_a, s_b, c_b, c_c, c_d, layers, norms, norm_final):
    bsz, l, _ = x.shape
    t = bsz * l
    prompt = l >= 64
    c = 64 if prompt else l
    nb = 1 if prompt else 8
    cc = min(256, l)
    tm = min(512, t)
    tt = min(512, t)
    h = x.reshape(t, D_MODEL)
    n_a, n_b, nb_b, nb_c, nb_d = [], [], [], [], []
    for i, lw in enumerate(layers):
        z = _inproj(h, norms['mix'][i][None, :], lw['w_in'], tm)
        z3 = z.reshape(bsz, l, ZW)
        eye4 = jnp.eye(HEADS, dtype=F32)
        s0t = (jnp.swapaxes(s_a[i], 2, 3)[:, :, :, None, :] * eye4[None, :, None, :, None]).reshape(bsz, GW, GW)
        o_a, st = _hgrn(z3, lw['lb'], lw['hgrn_norm'], s0t, min(8, bsz) if prompt else nb, min(c, 32))
        st = st.reshape(bsz, HEADS, DH, HEADS, DH)
        n_a.append(jnp.stack([jnp.swapaxes(st[:, j, :, j, :], 1, 2) for j in range(HEADS)], axis=1))
        buf8 = jnp.pad(c_b[i], ((0, 0), (8 - (B_CONV - 1), 0), (0, 0)))
        o_b, sb = _gdn(z3, lw['gdn_conv_w'], lw['gdn_a_log'], lw['gdn_dt_bias'], lw['gdn_norm'], buf8, s_b[i],
                       min(4, bsz) if prompt else nb, c)
        n_b.append(sb)
        nb_b.append(_tail(c_b[i], z3[:, :, ZB_BQ * GW:(ZB_BV + 1) * GW], B_CONV - 1))
        cbuf = jnp.pad(c_c[i], ((0, 0), (CPAD - (C_CONV - 1), 0), (0, 0)))
        dbuf = jnp.pad(c_d[i], ((0, 0), (DPAD - POOL_BUF, 0), (0, 0)))
        o_c, u_c, o_d = _convpool(z3, lw['conf_dw_w'], lw['conf_dw_b'], lw['conf_ln_g'], lw['conf_ln_b'],
                                  lw['pool_w'], lw['pool_scale'], cbuf, dbuf, nb, cc, pos0)
        nb_c.append(_tail(c_c[i], u_c, C_CONV - 1))
        nb_d.append(_tail(c_d[i], z3[:, :, ZB_D * GW:(ZB_D + 1) * GW], POOL_BUF))
        h, xn, e1, cnt, e2, r2 = _outpeer1(h, o_a.reshape(t, GW), o_b.reshape(t, GW), o_c.reshape(t, GW),
                                           o_d.reshape(t, GW), lw['w_out'], norms['ffn'][i][None, :],
                                           lw['peer_wq'], lw['kbd'], tt)
        h = _peer2(h, xn, lw['peer_u'], lw['peer_vt'], e1, cnt, e2, r2, min(1024, t))
        h = _ple(h, p.reshape(p.shape[0], t, D_PLE), i, norms['ple'][i][None, :], lw['ple_gate_w'], lw['ple_proj_w'],
                 norm_final[None, :], tm, final=(i == len(layers) - 1))
    st = lambda lst: jnp.stack(lst, axis=0)
    return h.reshape(bsz, l, D_MODEL), st(n_a), st(n_b), st(nb_b), st(nb_c), st(nb_d)


def kernel(x_prompt, x_sample, state_hgrn, state_gdn, state_gdn_conv, state_conf_conv, state_pool, p_prompt, p_sample, norm_mix, w_in, hgrn_lb_logits, hgrn_norm, gdn_conv_w, gdn_a_log, gdn_dt_bias, gdn_norm, conf_dw_w, conf_dw_b, conf_ln_g, conf_ln_b, pool_w, pool_scale, w_out, norm_ffn, peer_wq, peer_keys, peer_u, peer_v, norm_ple, ple_gate_w, ple_proj_w, norm_final):
    sm = jax.nn.softmax(hgrn_lb_logits.astype(F32), axis=0)
    lbs = jnp.cumsum(sm, axis=0) - sm[0:1]
    layers = [_prep_layer(i, w_in, lbs, hgrn_norm, gdn_conv_w, gdn_a_log, gdn_dt_bias, gdn_norm, conf_dw_w,
                          conf_dw_b, conf_ln_g, conf_ln_b, pool_w, pool_scale, w_out, peer_wq, peer_keys, peer_u,
                          peer_v, ple_gate_w, ple_proj_w) for i in range(DEPTH)]
    norms = dict(mix=norm_mix, ffn=norm_ffn, ple=norm_ple)
    bp = x_prompt.shape[0]
    z = lambda *s: jnp.zeros((DEPTH, bp) + s, F32)
    yp, hp, gp, gcp, ccp, pp = _trunk(x_prompt, p_prompt, 0, z(HEADS, DH, DH), z(HEADS, DH, DH),
                                      z(B_CONV - 1, 3 * GW), z(C_CONV - 1, GW), z(POOL_BUF, GW),
                                      layers, norms, norm_final)
    ys, hs, gs, gcs, ccs, ps = _trunk(x_sample, p_sample, PAST_LEN, state_hgrn, state_gdn, state_gdn_conv,
                                      state_conf_conv, state_pool, layers, norms, norm_final)
    return (yp, ys, hp, gp, gcp, ccp, pp, hs, gs, gcs, ccs, ps)
```
